```python
import jax, jax.numpy as jnp
from jax import lax
import numpy as np

D_MODEL = 1024
BATCH = 32
SEQ = 2048
DEPTH = 4

GRID_W = 64
HEAD_DIM = 64
N_HEADS_A = D_MODEL // (2 * HEAD_DIM)
N_HEADS_B = D_MODEL // (4 * HEAD_DIM)
N_GROUPS_C = D_MODEL // (4 * HEAD_DIM)
WIDTH_A = N_HEADS_A * HEAD_DIM
WIDTH_B = N_HEADS_B * HEAD_DIM
WIDTH_C = N_GROUPS_C * HEAD_DIM
MIX_WIDTH = WIDTH_A + WIDTH_B + WIDTH_C
N_HEADS_TOTAL = MIX_WIDTH // HEAD_DIM
IN_WIDTH = 3 * WIDTH_A + 2 * WIDTH_B + WIDTH_C
NA_ROWS_MAX = 8
NA_COLS = 16
CHUNK = 128
D_FF = -(-8 * D_MODEL // (3 * 256)) * 256
EPS = 1e-6
NEG_INF = -1e9

kernel_name = "hybrid_natten_gmlp_fnet_encoder"


def rms_norm(x, g):
    xf = x.astype(jnp.float32)
    y = xf * lax.rsqrt(jnp.mean(xf * xf, axis=-1, keepdims=True) + EPS)
    return (y * g.astype(jnp.float32)).astype(x.dtype)


def layer_norm(x, g, b):
    xf = x.astype(jnp.float32)
    mu = jnp.mean(xf, axis=-1, keepdims=True)
    var = jnp.mean(jnp.square(xf - mu), axis=-1, keepdims=True)
    y = (xf - mu) * lax.rsqrt(var + EPS)
    return (y * g.astype(jnp.float32) + b.astype(jnp.float32)).astype(x.dtype)


def head_rms_norm(y, g):
    b_, s, _ = y.shape
    yf = y.astype(jnp.float32).reshape(b_, s, N_HEADS_TOTAL, HEAD_DIM)
    yf = yf * lax.rsqrt(jnp.mean(yf * yf, axis=-1, keepdims=True) + EPS)
    return (yf.reshape(b_, s, MIX_WIDTH) * g.astype(jnp.float32)).astype(y.dtype)


def neighbourhood_attention(q, k, v, rpb):
    b_, s, h, dh = q.shape
    rows = s // GRID_W
    kh = min(NA_ROWS_MAX, rows)
    kw = NA_COLS
    band = 2 * kw
    nqb = GRID_W // kw
    scale = HEAD_DIM ** -0.5

    qg = q.reshape(b_, rows, nqb, kw, h, dh)
    kg = k.reshape(b_, rows, GRID_W, h, dh)
    vg = v.reshape(b_, rows, GRID_W, h, dh)

    qcol = np.arange(GRID_W).reshape(nqb, kw)
    band_start = np.clip(np.arange(nqb) * kw - kw // 2, 0, GRID_W - band)
    band_cols = band_start[:, None] + np.arange(band)[None, :]
    win_start = np.clip(qcol - kw // 2, 0, GRID_W - kw)
    kc = band_cols[:, None, :]
    in_win = (kc >= win_start[..., None]) & (kc < win_start[..., None] + kw)
    dcol_idx = np.clip(kc - qcol[..., None] + NA_COLS - 1, 0, 2 * NA_COLS - 2)
    col_bias = rpb.astype(jnp.float32)[:, :, dcol_idx]
    col_bias = jnp.where(jnp.asarray(in_win)[None, None], col_bias, NEG_INF)

    def row_step(r):
        rs = jnp.clip(r - kh // 2, 0, rows - kh)
        k_slab = lax.dynamic_slice_in_dim(kg, rs, kh, axis=1)
        v_slab = lax.dynamic_slice_in_dim(vg, rs, kh, axis=1)
        k_band = k_slab[:, :, band_cols]
        v_band = v_slab[:, :, band_cols]
        q_row = lax.dynamic_index_in_dim(qg, r, axis=1, keepdims=False)
        sc = jnp.einsum('bjqhd,bkjnhd->bhjqkn', q_row, k_band).astype(jnp.float32) * scale
        drow_idx = rs + jnp.arange(kh) - r + NA_ROWS_MAX - 1
        bias = col_bias[:, drow_idx].transpose(0, 2, 3, 1, 4)
        logits = (sc + bias[None]).reshape(b_, h, nqb, kw, kh * band)
        p = jax.nn.softmax(logits, axis=-1).reshape(b_, h, nqb, kw, kh, band).astype(v.dtype)
        o = jnp.einsum('bhjqkn,bkjnhd->bjqhd', p, v_band)
        return o.reshape(b_, GRID_W, h, dh)

    out = lax.map(row_step, jnp.arange(rows))
    return out.transpose(1, 0, 2, 3, 4).reshape(b_, s, h * dh)


def spatial_gating(z, ln_g, ln_b, w_s, b_s):
    b_, s, _ = z.shape
    z = jax.nn.gelu(z)
    u, v = jnp.split(z, 2, axis=-1)
    v = layer_norm(v, ln_g, ln_b)
    vc = v.reshape(b_, s // CHUNK, CHUNK, N_HEADS_B, HEAD_DIM)
    mixed = jnp.einsum('hpq,bnqhd->bnphd', w_s, vc) + b_s.T[None, None, :, :, None]
    return u * mixed.reshape(b_, s, WIDTH_B)


def fourier_mix(xc):
    b_, s, _ = xc.shape
    xf = xc.astype(jnp.float32).reshape(b_, s, N_GROUPS_C, HEAD_DIM)
    y = jnp.real(jnp.fft.fft2(xf, axes=(1, 3), norm="ortho"))
    return y.astype(xc.dtype).reshape(b_, s, WIDTH_C)


def setup_inputs(seed: int = 0) -> dict:
    key = jax.random.key(seed)
    ks = jax.random.split(key, 16)
    f32 = jnp.float32
    nrm = lambda k, shp, sc: jax.random.normal(k, shp, f32) * sc
    return {
        "x": jax.random.normal(ks[0], (BATCH, SEQ, D_MODEL), f32),
        "norm_mix_g": 1.0 + nrm(ks[1], (DEPTH, D_MODEL), 0.05),
        "w_in": nrm(ks[2], (DEPTH, D_MODEL, IN_WIDTH), D_MODEL ** -0.5),
        "rpb": nrm(ks[3], (DEPTH, N_HEADS_A, 2 * NA_ROWS_MAX - 1, 2 * NA_COLS - 1), 0.5),
        "gmlp_ln_g": 1.0 + nrm(ks[4], (DEPTH, WIDTH_B), 0.05),
        "gmlp_ln_b": nrm(ks[5], (DEPTH, WIDTH_B), 0.02),
        "w_spatial": nrm(ks[6], (DEPTH, N_HEADS_B, CHUNK, CHUNK), CHUNK ** -0.5),
        "b_spatial": 1.0 + nrm(ks[7], (DEPTH, N_HEADS_B, CHUNK), 0.1),
        "head_norm_g": 1.0 + nrm(ks[8], (DEPTH, MIX_WIDTH), 0.05),
        "w_out": nrm(ks[9], (DEPTH, MIX_WIDTH, D_MODEL), MIX_WIDTH ** -0.5),
        "norm_ffn_g": 1.0 + nrm(ks[10], (DEPTH, D_MODEL), 0.05),
        "w_gate": nrm(ks[11], (DEPTH, D_MODEL, D_FF), D_MODEL ** -0.5),
        "w_up": nrm(ks[12], (DEPTH, D_MODEL, D_FF), D_MODEL ** -0.5),
        "w_down": nrm(ks[13], (DEPTH, D_FF, D_MODEL), D_FF ** -0.5),
        "final_norm_g": 1.0 + nrm(ks[14], (D_MODEL,), 0.05),
    }


def reference(x, norm_mix_g, w_in, rpb, gmlp_ln_g, gmlp_ln_b, w_spatial, b_spatial,
              head_norm_g, w_out, norm_ffn_g, w_gate, w_up, w_down, final_norm_g):
    b_, s, _ = x.shape
    splits = [WIDTH_A, 2 * WIDTH_A, 3 * WIDTH_A, 3 * WIDTH_A + 2 * WIDTH_B]
    for l in range(DEPTH):
        h = rms_norm(x, norm_mix_g[l])
        proj = h @ w_in[l]
        q, k, v, z_b, x_c = jnp.split(proj, splits, axis=-1)
        mix_a = neighbourhood_attention(q.reshape(b_, s, N_HEADS_A, HEAD_DIM),
                                        k.reshape(b_, s, N_HEADS_A, HEAD_DIM),
                                        v.reshape(b_, s, N_HEADS_A, HEAD_DIM), rpb[l])
        mix_b = spatial_gating(z_b, gmlp_ln_g[l], gmlp_ln_b[l], w_spatial[l], b_spatial[l])
        mix_c = fourier_mix(x_c)
        mix = head_rms_norm(jnp.concatenate([mix_a, mix_b, mix_c], axis=-1), head_norm_g[l])
        x = x + mix @ w_out[l]
        h = rms_norm(x, norm_ffn_g[l])
        x = x + (jax.nn.silu(h @ w_gate[l]) * (h @ w_up[l])) @ w_down[l]
    return rms_norm(x, final_norm_g)
```

```python
import functools
import math

import numpy as np
import jax
import jax.numpy as jnp
from jax import lax
from jax.experimental import pallas as pl
from jax.experimental.pallas import tpu as pltpu

F32 = jnp.float32
BF16 = jnp.bfloat16

D_MODEL = 1024
GRID_W = 64
HEAD_DIM = 64
WIDTH_A = D_MODEL // 2
WIDTH_B = D_MODEL // 4
WIDTH_C = D_MODEL // 4
N_HEADS_A = WIDTH_A // HEAD_DIM
N_HEADS_B = WIDTH_B // HEAD_DIM
IN_WIDTH = 3 * WIDTH_A + 2 * WIDTH_B + WIDTH_C
NA_ROWS = 8
NA_COLS = 16
CHUNK = 128
D_FF = -(-8 * D_MODEL // (3 * 256)) * 256
EPS = 1e-6
NEG_INF = -1e9

LANES = 128
VMEM_LIMIT_BYTES = 56 * 1024 * 1024

QROWS = 4
KROWS = 12
QBLK = QROWS * GRID_W
KBLK = KROWS * GRID_W

IN_TM = 512
FFN_TM = 512
N_CHUNK = 256


def _compiler_params(n_grid_dims):
    return pltpu.CompilerParams(
        dimension_semantics=("arbitrary",) * n_grid_dims,
        vmem_limit_bytes=VMEM_LIMIT_BYTES,
    )


def _resident():
    return pl.BlockSpec(memory_space=pltpu.VMEM)


def _rms_norm_rows(x, g):
    ms = jnp.mean(x * x, axis=-1, keepdims=True)
    return x * lax.rsqrt(ms + EPS) * g


def _head_norm_lane_pairs(y, g):
    n_tiles = y.shape[-1] // LANES
    lane = lax.broadcasted_iota(jnp.int32, (1, LANES), 1)
    first = lane < HEAD_DIM
    outs = []
    for t in range(n_tiles):
        yt = y[:, t * LANES:(t + 1) * LANES]
        sq = yt * yt
        ss_a = jnp.sum(jnp.where(first, sq, 0.0), axis=-1, keepdims=True)
        ss_b = jnp.sum(jnp.where(first, 0.0, sq), axis=-1, keepdims=True)
        ms = jnp.where(first, ss_a, ss_b) * (1.0 / HEAD_DIM)
        outs.append(yt * lax.rsqrt(ms + EPS) * g[:, t * LANES:(t + 1) * LANES])
    return outs[0] if n_tiles == 1 else jnp.concatenate(outs, axis=-1)


def _in_proj_kernel(x_ref, g_ref, w_ref, o_ref):
    h = _rms_norm_rows(x_ref[...], g_ref[...]).astype(BF16)
    for c in range(IN_WIDTH // N_CHUNK):
        cols = slice(c * N_CHUNK, (c + 1) * N_CHUNK)
        acc = jnp.dot(h, w_ref[:, cols], preferred_element_type=F32)
        if (c + 1) * N_CHUNK <= WIDTH_A:
            acc = acc * (HEAD_DIM ** -0.5)
        o_ref[:, cols] = acc.astype(o_ref.dtype)


def _in_proj(x2d, g, w_bf16):
    n = x2d.shape[0]
    return pl.pallas_call(
        _in_proj_kernel,
        grid=(n // IN_TM,),
        in_specs=[
            pl.BlockSpec((IN_TM, D_MODEL), lambda i: (i, 0)),
            _resident(),
            _resident(),
        ],
        out_specs=pl.BlockSpec((IN_TM, IN_WIDTH), lambda i: (i, 0)),
        out_shape=jax.ShapeDtypeStruct((n, IN_WIDTH), BF16),
        compiler_params=_compiler_params(1),
        name="in_proj",
    )(x2d, g.reshape(1, D_MODEL), w_bf16)


def _attn_kernel(q_ref, k_ref, v_ref, bias_ref, g_ref, o_ref, *, n_blocks):
    lane = lax.broadcasted_iota(jnp.int32, (1, LANES), 1)
    first = lane < HEAD_DIM
    g = g_ref[...]

    def block(blk, carry):
        start_row = jnp.clip(blk * QROWS - NA_ROWS // 2, 0, n_blocks * QROWS - KROWS)
        variant = jnp.where(blk == 0, 0, jnp.where(blk == n_blocks - 1, 2, 1))
        q0 = pl.multiple_of(blk * QBLK, QBLK)
        k0 = pl.multiple_of(start_row * GRID_W, GRID_W)
        q = q_ref[pl.ds(q0, QBLK), :]
        k = k_ref[pl.ds(k0, KBLK), :]
        v = v_ref[pl.ds(k0, KBLK), :]
        outs = []
        for hh in range(2):
            sel = first if hh == 0 else jnp.logical_not(first)
            qm = jnp.where(sel, q, jnp.zeros_like(q))
            s = lax.dot_general(qm, k, (((1,), (1,)), ((), ())), preferred_element_type=F32)
            s = s + bias_ref[variant, hh]
            m = jnp.max(s, axis=-1, keepdims=True)
            p = jnp.exp(s - m)
            l = jnp.sum(p, axis=-1, keepdims=True)
            o = jnp.dot(p.astype(BF16), v, preferred_element_type=F32)
            outs.append(o * (1.0 / l))
        o = jnp.where(first, outs[0], outs[1])
        o_ref[pl.ds(q0, QBLK), :] = _head_norm_lane_pairs(o, g).astype(o_ref.dtype)
        return carry

    lax.fori_loop(0, n_blocks, block, 0)


def _attention(proj, bias_tab, g, batch, seq):
    n_blocks = seq // QBLK
    n_pairs = WIDTH_A // LANES
    k_off = WIDTH_A // LANES
    v_off = 2 * WIDTH_A // LANES
    return pl.pallas_call(
        functools.partial(_attn_kernel, n_blocks=n_blocks),
        grid=(batch, n_pairs),
        in_specs=[
            pl.BlockSpec((seq, LANES), lambda b, hp: (b, hp)),
            pl.BlockSpec((seq, LANES), lambda b, hp: (b, k_off + hp)),
            pl.BlockSpec((seq, LANES), lambda b, hp: (b, v_off + hp)),
            pl.BlockSpec((3, 2, QBLK, KBLK), lambda b, hp: (0, hp, 0, 0)),
            pl.BlockSpec((1, LANES), lambda b, hp: (0, hp)),
        ],
        out_specs=pl.BlockSpec((seq, LANES), lambda b, hp: (b, hp)),
        out_shape=jax.ShapeDtypeStruct((batch * seq, WIDTH_A), BF16),
        compiler_params=_compiler_params(2),
        name="nbr_attention",
    )(proj, proj, proj, bias_tab, g.reshape(1, D_MODEL)[:, :WIDTH_A])


def _attention_bias_table(rpb, seq):
    rows = seq // GRID_W
    n_blocks = rows // QROWS
    qc = np.arange(GRID_W)[:, None]
    kc = np.arange(GRID_W)[None, :]
    win = np.clip(qc - NA_COLS // 2, 0, GRID_W - NA_COLS)
    col_ok = (kc >= win) & (kc < win + NA_COLS)
    dcol = np.clip(kc - qc + NA_COLS - 1, 0, 2 * NA_COLS - 2)
    tiles = jnp.where(jnp.asarray(col_ok)[None, None], rpb.astype(F32)[:, :, dcol], NEG_INF)
    masked = jnp.full((rpb.shape[0], GRID_W, GRID_W), NEG_INF, F32)
    variants = []
    for blk in (0, 1, n_blocks - 1):
        start = int(np.clip(blk * QROWS - NA_ROWS // 2, 0, rows - KROWS))
        q_rows = []
        for a in range(QROWS):
            qr = blk * QROWS + a
            rs = int(np.clip(qr - NA_ROWS // 2, 0, rows - NA_ROWS))
            k_tiles = []
            for b in range(KROWS):
                kr = start + b
                k_tiles.append(tiles[:, kr - qr + NA_ROWS - 1] if rs <= kr < rs + NA_ROWS else masked)
            q_rows.append(jnp.concatenate(k_tiles, axis=-1))
        variants.append(jnp.concatenate(q_rows, axis=-2))
    return jnp.stack(variants, axis=0)


def _gelu_tanh(x):
    c = math.sqrt(2.0 / math.pi)
    return x * (0.5 * (1.0 + jnp.tanh(c * (x + 0.044715 * (x * x * x)))))


def _gate_kernel(z_ref, lng_ref, lnb_ref, wcat_ref, bs_ref, g_ref, o_ref, *, n_chunks):
    lane = lax.broadcasted_iota(jnp.int32, (1, WIDTH_B), 1)
    head_of_lane = lane // HEAD_DIM

    def chunk(c, carry):
        t0 = pl.multiple_of(c * CHUNK, CHUNK)
        z = _gelu_tanh(z_ref[pl.ds(t0, CHUNK), :].astype(F32))
        u = z[:, :WIDTH_B]
        v = z[:, WIDTH_B:]
        mu = jnp.mean(v, axis=-1, keepdims=True)
        d = v - mu
        var = jnp.mean(d * d, axis=-1, keepdims=True)
        vn = (d * lax.rsqrt(var + EPS) * lng_ref[...] + lnb_ref[...]).astype(BF16)
        stacked = jnp.concatenate(
            [jnp.where(head_of_lane == h, vn, jnp.zeros_like(vn)) for h in range(N_HEADS_B)], axis=0)
        mixed = jnp.dot(wcat_ref[...], stacked, preferred_element_type=F32) + bs_ref[...]
        o_ref[pl.ds(t0, CHUNK), :] = _head_norm_lane_pairs(u * mixed, g_ref[...]).astype(o_ref.dtype)
        return carry

    lax.fori_loop(0, n_chunks, chunk, 0)


def _spatial_gate(proj, ln_g, ln_b, w_s, b_s, g, batch, seq):
    z_off = 3 * WIDTH_A // (2 * WIDTH_B)
    wcat = jnp.transpose(w_s, (1, 0, 2)).reshape(CHUNK, N_HEADS_B * CHUNK).astype(BF16)
    bs_full = jnp.repeat(b_s.T.astype(F32), HEAD_DIM, axis=1)
    g_b = g.reshape(1, D_MODEL)[:, WIDTH_A:WIDTH_A + WIDTH_B]
    return pl.pallas_call(
        functools.partial(_gate_kernel, n_chunks=seq // CHUNK),
        grid=(batch,),
        in_specs=[
            pl.BlockSpec((seq, 2 * WIDTH_B), lambda b: (b, z_off)),
            _resident(), _resident(), _resident(), _resident(), _resident(),
        ],
        out_specs=pl.BlockSpec((seq, WIDTH_B), lambda b: (b, 0)),
        out_shape=jax.ShapeDtypeStruct((batch * seq, WIDTH_B), BF16),
        compiler_params=_compiler_params(1),
        name="spatial_gate",
    )(proj, ln_g.reshape(1, WIDTH_B), ln_b.reshape(1, WIDTH_B), wcat, bs_full, g_b)


def _dft_tables(seq):
    def cos_sin(n):
        i = lax.broadcasted_iota(jnp.int32, (n, n), 0)
        j = lax.broadcasted_iota(jnp.int32, (n, n), 1)
        ang = ((i * j) % n).astype(F32) * (2.0 * math.pi / n)
        return jnp.cos(ang), jnp.sin(ang)

    c_seq, s_seq = cos_sin(seq)
    seq_tab = jnp.concatenate([c_seq, -s_seq], axis=1).astype(BF16)
    c_ch, s_ch = cos_sin(HEAD_DIM)
    eye = jnp.eye(WIDTH_C // HEAD_DIM, dtype=F32)
    ch_tab = jnp.concatenate([jnp.kron(eye, c_ch), jnp.kron(eye, s_ch)], axis=1).astype(BF16)
    return seq_tab, ch_tab


def _fourier_kernel(x_ref, seq_tab_ref, ch_tab_ref, g_ref, o_ref, *, seq):
    xcs = jnp.dot(x_ref[...], ch_tab_ref[...], preferred_element_type=F32)
    stacked = jnp.concatenate([xcs[:, :WIDTH_C], xcs[:, WIDTH_C:]], axis=0).astype(BF16)
    scale = 1.0 / math.sqrt(seq * HEAD_DIM)
    for t in range(seq // N_CHUNK):
        rows = slice(t * N_CHUNK, (t + 1) * N_CHUNK)
        y = jnp.dot(seq_tab_ref[rows, :], stacked, preferred_element_type=F32) * scale
        o_ref[rows, :] = _head_norm_lane_pairs(y, g_ref[...]).astype(o_ref.dtype)


def _fourier_mix(proj, seq_tab, ch_tab, g, batch, seq):
    c_off = (3 * WIDTH_A + 2 * WIDTH_B) // WIDTH_C
    g_c = g.reshape(1, D_MODEL)[:, WIDTH_A + WIDTH_B:]
    return pl.pallas_call(
        functools.partial(_fourier_kernel, seq=seq),
        grid=(batch,),
        in_specs=[
            pl.BlockSpec((seq, WIDTH_C), lambda b: (b, c_off)),
            _resident(), _resident(), _resident(),
        ],
        out_specs=pl.BlockSpec((seq, WIDTH_C), lambda b: (b, 0)),
        out_shape=jax.ShapeDtypeStruct((batch * seq, WIDTH_C), BF16),
        compiler_params=_compiler_params(1),
        name="fourier_mix",
    )(proj, seq_tab, ch_tab, g_c)


def _ffn_kernel(ma_ref, mb_ref, mc_ref, x_ref, wo_ref, gf_ref, wg_ref, wu_ref, wd_ref, gfin_ref,
                o_ref, x1_ref, h_ref, act_ref, *, final):
    mix = jnp.concatenate([ma_ref[...], mb_ref[...], mc_ref[...]], axis=-1)
    for c in range(D_MODEL // N_CHUNK):
        cols = slice(c * N_CHUNK, (c + 1) * N_CHUNK)
        x1_ref[:, cols] = x_ref[:, cols] + jnp.dot(mix, wo_ref[:, cols], preferred_element_type=F32)
    h_ref[...] = _rms_norm_rows(x1_ref[...], gf_ref[...]).astype(BF16)
    for c in range(D_FF // N_CHUNK):
        cols = slice(c * N_CHUNK, (c + 1) * N_CHUNK)
        gate = jnp.dot(h_ref[...], wg_ref[:, cols], preferred_element_type=F32)
        up = jnp.dot(h_ref[...], wu_ref[:, cols], preferred_element_type=F32)
        act_ref[:, cols] = (gate * jax.nn.sigmoid(gate) * up).astype(BF16)
    for c in range(D_MODEL // N_CHUNK):
        cols = slice(c * N_CHUNK, (c + 1) * N_CHUNK)
        x1_ref[:, cols] = x1_ref[:, cols] + jnp.dot(act_ref[...], wd_ref[:, cols],
                                                    preferred_element_type=F32)
    if final:
        o_ref[...] = _rms_norm_rows(x1_ref[...], gfin_ref[...])
    else:
        o_ref[...] = x1_ref[...]


def _out_proj_ffn(mix_a, mix_b, mix_c, x2d, w_out, g_ffn, w_gate, w_up, w_down, g_final, final):
    n = x2d.shape[0]
    row = lambda width: pl.BlockSpec((FFN_TM, width), lambda i: (i, 0))
    return pl.pallas_call(
        functools.partial(_ffn_kernel, final=final),
        grid=(n // FFN_TM,),
        in_specs=[row(WIDTH_A), row(WIDTH_B), row(WIDTH_C), row(D_MODEL),
                  _resident(), _resident(), _resident(), _resident(), _resident(), _resident()],
        out_specs=row(D_MODEL),
        out_shape=jax.ShapeDtypeStruct((n, D_MODEL), F32),
        scratch_shapes=[
            pltpu.VMEM((FFN_TM, D_MODEL), F32),
            pltpu.VMEM((FFN_TM, D_MODEL), BF16),
            pltpu.VMEM((FFN_TM, D_FF), BF16),
        ],
        compiler_params=_compiler_params(1),
        name="out_proj_ffn",
    )(mix_a, mix_b, mix_c, x2d, w_out, g_ffn.reshape(1, D_MODEL), w_gate, w_up, w_down,
      g_final.reshape(1, D_MODEL))


def kernel(x, norm_mix_g, w_in, rpb, gmlp_ln_g, gmlp_ln_b, w_spatial, b_spatial, head_norm_g, w_out,
           norm_ffn_g, w_gate, w_up, w_down, final_norm_g):
    batch, seq, d_model = x.shape
    assert d_model == D_MODEL and seq % QBLK == 0 and seq // GRID_W >= KROWS
    depth = w_in.shape[0]
    seq_tab, ch_tab = _dft_tables(seq)
    x2d = x.reshape(batch * seq, D_MODEL)
    for l in range(depth):
        proj = _in_proj(x2d, norm_mix_g[l], w_in[l].astype(BF16))
        bias_tab = _attention_bias_table(rpb[l], seq)
        mix_a = _attention(proj, bias_tab, head_norm_g[l], batch, seq)
        mix_b = _spatial_gate(proj, gmlp_ln_g[l], gmlp_ln_b[l], w_spatial[l], b_spatial[l],
                              head_norm_g[l], batch, seq)
        mix_c = _fourier_mix(proj, seq_tab, ch_tab, head_norm_g[l], batch, seq)
        x2d = _out_proj_ffn(mix_a, mix_b, mix_c, x2d, w_out[l].astype(BF16), norm_ffn_g[l],
                            w_gate[l].astype(BF16), w_up[l].astype(BF16), w_down[l].astype(BF16),
                            final_norm_g, final=(l == depth - 1))
    return x2d.reshape(batch, seq, D_MODEL)
```

```python
import functools
import math

import numpy as np
import jax
import jax.numpy as jnp
from jax import lax
from jax.experimental import pallas as pl
from jax.experimental.pallas import tpu as pltpu

F32 = jnp.float32
BF16 = jnp.bfloat16

D_MODEL = 1024
GRID_W = 64
HEAD_DIM = 64
WIDTH_A = D_MODEL // 2
WIDTH_B = D_MODEL // 4
WIDTH_C = D_MODEL // 4
N_HEADS_A = WIDTH_A // HEAD_DIM
N_HEADS_B = WIDTH_B // HEAD_DIM
IN_WIDTH = 3 * WIDTH_A + 2 * WIDTH_B + WIDTH_C
NA_ROWS = 8
NA_COLS = 16
CHUNK = 128
D_FF = -(-8 * D_MODEL // (3 * 256)) * 256
EPS = 1e-6
NEG_INF = -1e9

LANES = 128
VMEM_LIMIT_BYTES = 56 * 1024 * 1024

HEADS_PER_STEP = 4
KEY_SLAB = NA_ROWS * GRID_W
ATTN_GROUP = 4
GATE_UNROLL = 4
LOG2_E = math.log2(math.e)
Q_SCALE = HEAD_DIM ** -0.5 * LOG2_E

IN_TM = 512
FFN_TM = 512
N_CHUNK = 256


def _compiler_params(n_grid_dims):
    return pltpu.CompilerParams(
        dimension_semantics=("arbitrary",) * n_grid_dims,
        vmem_limit_bytes=VMEM_LIMIT_BYTES,
    )


def _resident():
    return pl.BlockSpec(memory_space=pltpu.VMEM)


def _rms_norm_rows(x, g):
    ms = jnp.mean(x * x, axis=-1, keepdims=True)
    return x * lax.rsqrt(ms + EPS) * g


def _head_norm_lane_pairs(y, g):
    n_tiles = y.shape[-1] // LANES
    lane = lax.broadcasted_iota(jnp.int32, (1, LANES), 1)
    first = lane < HEAD_DIM
    outs = []
    for t in range(n_tiles):
        yt = y[:, t * LANES:(t + 1) * LANES]
        sq = yt * yt
        ss_a = jnp.sum(jnp.where(first, sq, 0.0), axis=-1, keepdims=True)
        ss_b = jnp.sum(jnp.where(first, 0.0, sq), axis=-1, keepdims=True)
        ms = jnp.where(first, ss_a, ss_b) * (1.0 / HEAD_DIM)
        outs.append(yt * lax.rsqrt(ms + EPS) * g[:, t * LANES:(t + 1) * LANES])
    return outs[0] if n_tiles == 1 else jnp.concatenate(outs, axis=-1)


def _in_proj_kernel(x_ref, g_ref, w_ref, o_ref):
    h = _rms_norm_rows(x_ref[...], g_ref[...]).astype(BF16)
    for c in range(IN_WIDTH // N_CHUNK):
        cols = slice(c * N_CHUNK, (c + 1) * N_CHUNK)
        acc = jnp.dot(h, w_ref[:, cols], preferred_element_type=F32)
        if (c + 1) * N_CHUNK <= WIDTH_A:
            acc = acc * Q_SCALE
        o_ref[:, cols] = acc.astype(o_ref.dtype)


def _in_proj(x2d, g, w_bf16):
    n = x2d.shape[0]
    return pl.pallas_call(
        _in_proj_kernel,
        grid=(n // IN_TM,),
        in_specs=[
            pl.BlockSpec((IN_TM, D_MODEL), lambda i: (i, 0)),
            _resident(),
            _resident(),
        ],
        out_specs=pl.BlockSpec((IN_TM, IN_WIDTH), lambda i: (i, 0)),
        out_shape=jax.ShapeDtypeStruct((n, IN_WIDTH), BF16),
        compiler_params=_compiler_params(1),
        name="in_proj",
    )(x2d, g.reshape(1, D_MODEL), w_bf16)


def _attn_kernel(q_ref, k_ref, v_ref, bias_ref, g_ref, o_ref, p0_ref, p1_ref, l0_ref, l1_ref, *,
                 n_rows):
    width = HEADS_PER_STEP * HEAD_DIM
    head_of_lane = lax.broadcasted_iota(jnp.int32, (1, width), 1) // HEAD_DIM
    g = g_ref[...]
    n_groups = n_rows // ATTN_GROUP

    def key_start(r):
        rs = jnp.clip(r - NA_ROWS // 2, 0, n_rows - NA_ROWS)
        return rs, pl.multiple_of(rs * GRID_W, GRID_W)

    def probs(group, p_ref, l_ref):
        for j in range(ATTN_GROUP):
            r = group * ATTN_GROUP + j
            rs, k0 = key_start(r)
            q = q_ref[pl.ds(pl.multiple_of(r * GRID_W, GRID_W), GRID_W), :]
            k = k_ref[pl.ds(k0, KEY_SLAB), :]
            zero = jnp.zeros_like(q)
            qm = jnp.concatenate(
                [jnp.where(head_of_lane == h, q, zero) for h in range(HEADS_PER_STEP)], axis=0)
            s = lax.dot_general(qm, k, (((1,), (1,)), ((), ())), preferred_element_type=F32)
            s = s + bias_ref[r - rs]
            m = jnp.max(s, axis=-1, keepdims=True)
            p = jnp.exp2(s - m)
            l_ref[j] = jnp.sum(p, axis=-1, keepdims=True)
            p_ref[j] = p.astype(BF16)

    def outputs(group, p_ref, l_ref):
        for j in range(ATTN_GROUP):
            r = group * ATTN_GROUP + j
            _, k0 = key_start(r)
            v = v_ref[pl.ds(k0, KEY_SLAB), :]
            o4 = jnp.dot(p_ref[j], v, preferred_element_type=F32) * (1.0 / l_ref[j])
            o = o4[:GRID_W]
            for h in range(1, HEADS_PER_STEP):
                o = jnp.where(head_of_lane == h, o4[h * GRID_W:(h + 1) * GRID_W], o)
            q0 = pl.multiple_of(r * GRID_W, GRID_W)
            o_ref[pl.ds(q0, GRID_W), :] = _head_norm_lane_pairs(o, g).astype(o_ref.dtype)

    probs(0, p0_ref, l0_ref)

    def pair(i, carry):
        probs(2 * i + 1, p1_ref, l1_ref)
        outputs(2 * i, p0_ref, l0_ref)
        probs(2 * i + 2, p0_ref, l0_ref)
        outputs(2 * i + 1, p1_ref, l1_ref)
        return carry

    lax.fori_loop(0, n_groups // 2 - 1, pair, 0)
    probs(n_groups - 1, p1_ref, l1_ref)
    outputs(n_groups - 2, p0_ref, l0_ref)
    outputs(n_groups - 1, p1_ref, l1_ref)


def _attention(proj, bias_tab, g, batch, seq):
    width = HEADS_PER_STEP * HEAD_DIM
    n_groups = WIDTH_A // width
    k_off = WIDTH_A // width
    v_off = 2 * WIDTH_A // width
    return pl.pallas_call(
        functools.partial(_attn_kernel, n_rows=seq // GRID_W),
        grid=(n_groups, batch),
        in_specs=[
            pl.BlockSpec((seq, width), lambda hg, b: (b, hg)),
            pl.BlockSpec((seq, width), lambda hg, b: (b, k_off + hg)),
            pl.BlockSpec((seq, width), lambda hg, b: (b, v_off + hg)),
            pl.BlockSpec((NA_ROWS, HEADS_PER_STEP * GRID_W, KEY_SLAB), lambda hg, b: (0, hg, 0)),
            pl.BlockSpec((1, width), lambda hg, b: (0, hg)),
        ],
        out_specs=pl.BlockSpec((seq, width), lambda hg, b: (b, hg)),
        out_shape=jax.ShapeDtypeStruct((batch * seq, WIDTH_A), BF16),
        scratch_shapes=[
            pltpu.VMEM((ATTN_GROUP, HEADS_PER_STEP * GRID_W, KEY_SLAB), BF16),
            pltpu.VMEM((ATTN_GROUP, HEADS_PER_STEP * GRID_W, KEY_SLAB), BF16),
            pltpu.VMEM((ATTN_GROUP, HEADS_PER_STEP * GRID_W, 1), F32),
            pltpu.VMEM((ATTN_GROUP, HEADS_PER_STEP * GRID_W, 1), F32),
        ],
        compiler_params=_compiler_params(2),
        name="nbr_attention",
    )(proj, proj, proj, bias_tab, g.reshape(1, D_MODEL)[:, :WIDTH_A])


def _attention_bias_table(rpb):
    n_heads = rpb.shape[0]
    qc = np.arange(GRID_W)[:, None]
    kc = np.arange(GRID_W)[None, :]
    win = np.clip(qc - NA_COLS // 2, 0, GRID_W - NA_COLS)
    col_ok = (kc >= win) & (kc < win + NA_COLS)
    dcol = np.clip(kc - qc + NA_COLS - 1, 0, 2 * NA_COLS - 2)
    tiles = jnp.where(jnp.asarray(col_ok)[None, None], rpb.astype(F32)[:, :, dcol], NEG_INF) * LOG2_E
    variants = [
        jnp.concatenate([tiles[:, kr - d + NA_ROWS - 1] for kr in range(NA_ROWS)], axis=-1)
        for d in range(NA_ROWS)
    ]
    return jnp.stack(variants, axis=0).reshape(NA_ROWS, n_heads * GRID_W, KEY_SLAB)


def _gelu_tanh(x):
    c = math.sqrt(2.0 / math.pi)
    return x * (0.5 * (1.0 + jnp.tanh(c * (x + 0.044715 * (x * x * x)))))


def _gate_kernel(z_ref, lng_ref, lnb_ref, wcat_ref, bs_ref, g_ref, o_ref, *, n_chunks):
    lane = lax.broadcasted_iota(jnp.int32, (1, WIDTH_B), 1)
    head_of_lane = lane // HEAD_DIM

    def chunk(c, carry):
        t0 = pl.multiple_of(c * CHUNK, CHUNK)
        z = _gelu_tanh(z_ref[pl.ds(t0, CHUNK), :].astype(F32))
        u = z[:, :WIDTH_B]
        v = z[:, WIDTH_B:]
        mu = jnp.mean(v, axis=-1, keepdims=True)
        d = v - mu
        var = jnp.mean(d * d, axis=-1, keepdims=True)
        vn = (d * lax.rsqrt(var + EPS) * lng_ref[...] + lnb_ref[...]).astype(BF16)
        stacked = jnp.concatenate(
            [jnp.where(head_of_lane == h, vn, jnp.zeros_like(vn)) for h in range(N_HEADS_B)], axis=0)
        mixed = jnp.dot(wcat_ref[...], stacked, preferred_element_type=F32) + bs_ref[...]
        o_ref[pl.ds(t0, CHUNK), :] = _head_norm_lane_pairs(u * mixed, g_ref[...]).astype(o_ref.dtype)
        return carry

    lax.fori_loop(0, n_chunks, chunk, 0, unroll=GATE_UNROLL)


def _spatial_gate(proj, ln_g, ln_b, w_s, b_s, g, batch, seq):
    z_off = 3 * WIDTH_A // (2 * WIDTH_B)
    wcat = jnp.transpose(w_s, (1, 0, 2)).reshape(CHUNK, N_HEADS_B * CHUNK).astype(BF16)
    bs_full = jnp.repeat(b_s.T.astype(F32), HEAD_DIM, axis=1)
    g_b = g.reshape(1, D_MODEL)[:, WIDTH_A:WIDTH_A + WIDTH_B]
    return pl.pallas_call(
        functools.partial(_gate_kernel, n_chunks=seq // CHUNK),
        grid=(batch,),
        in_specs=[
            pl.BlockSpec((seq, 2 * WIDTH_B), lambda b: (b, z_off)),
            _resident(), _resident(), _resident(), _resident(), _resident(),
        ],
        out_specs=pl.BlockSpec((seq, WIDTH_B), lambda b: (b, 0)),
        out_shape=jax.ShapeDtypeStruct((batch * seq, WIDTH_B), BF16),
        compiler_params=_compiler_params(1),
        name="spatial_gate",
    )(proj, ln_g.reshape(1, WIDTH_B), ln_b.reshape(1, WIDTH_B), wcat, bs_full, g_b)


def _dft_tables(seq):
    def cos_sin(n):
        i = lax.broadcasted_iota(jnp.int32, (n, n), 0)
        j = lax.broadcasted_iota(jnp.int32, (n, n), 1)
        ang = ((i * j) % n).astype(F32) * (2.0 * math.pi / n)
        return jnp.cos(ang), jnp.sin(ang)

    c_seq, s_seq = cos_sin(seq)
    seq_tab = jnp.concatenate([c_seq, -s_seq], axis=1).astype(BF16)
    c_ch, s_ch = cos_sin(HEAD_DIM)
    eye = jnp.eye(WIDTH_C // HEAD_DIM, dtype=F32)
    ch_tab = jnp.concatenate([jnp.kron(eye, c_ch), jnp.kron(eye, s_ch)], axis=1).astype(BF16)
    return seq_tab, ch_tab


def _fourier_kernel(x_ref, seq_tab_ref, ch_tab_ref, g_ref, o_ref, *, seq):
    xcs = jnp.dot(x_ref[...], ch_tab_ref[...], preferred_element_type=F32)
    stacked = jnp.concatenate([xcs[:, :WIDTH_C], xcs[:, WIDTH_C:]], axis=0).astype(BF16)
    scale = 1.0 / math.sqrt(seq * HEAD_DIM)
    for t in range(seq // N_CHUNK):
        rows = slice(t * N_CHUNK, (t + 1) * N_CHUNK)
        y = jnp.dot(seq_tab_ref[rows, :], stacked, preferred_element_type=F32) * scale
        o_ref[rows, :] = _head_norm_lane_pairs(y, g_ref[...]).astype(o_ref.dtype)


def _fourier_mix(proj, seq_tab, ch_tab, g, batch, seq):
    c_off = (3 * WIDTH_A + 2 * WIDTH_B) // WIDTH_C
    g_c = g.reshape(1, D_MODEL)[:, WIDTH_A + WIDTH_B:]
    return pl.pallas_call(
        functools.partial(_fourier_kernel, seq=seq),
        grid=(batch,),
        in_specs=[
            pl.BlockSpec((seq, WIDTH_C), lambda b: (b, c_off)),
            _resident(), _resident(), _resident(),
        ],
        out_specs=pl.BlockSpec((seq, WIDTH_C), lambda b: (b, 0)),
        out_shape=jax.ShapeDtypeStruct((batch * seq, WIDTH_C), BF16),
        compiler_params=_compiler_params(1),
        name="fourier_mix",
    )(proj, seq_tab, ch_tab, g_c)


def _ffn_kernel(ma_ref, mb_ref, mc_ref, x_ref, wo_ref, gf_ref, wg_ref, wu_ref, wd_ref, gfin_ref,
                o_ref, x1_ref, h_ref, act_ref, *, final):
    mix = jnp.concatenate([ma_ref[...], mb_ref[...], mc_ref[...]], axis=-1)
    for c in range(D_MODEL // N_CHUNK):
        cols = slice(c * N_CHUNK, (c + 1) * N_CHUNK)
        x1_ref[:, cols] = x_ref[:, cols] + jnp.dot(mix, wo_ref[:, cols], preferred_element_type=F32)
    h_ref[...] = _rms_norm_rows(x1_ref[...], gf_ref[...]).astype(BF16)
    for c in range(D_FF // N_CHUNK):
        cols = slice(c * N_CHUNK, (c + 1) * N_CHUNK)
        gate = jnp.dot(h_ref[...], wg_ref[:, cols], preferred_element_type=F32)
        up = jnp.dot(h_ref[...], wu_ref[:, cols], preferred_element_type=F32)
        act_ref[:, cols] = (gate * jax.nn.sigmoid(gate) * up).astype(BF16)
    for c in range(D_MODEL // N_CHUNK):
        cols = slice(c * N_CHUNK, (c + 1) * N_CHUNK)
        x1_ref[:, cols] = x1_ref[:, cols] + jnp.dot(act_ref[...], wd_ref[:, cols],
                                                    preferred_element_type=F32)
    if final:
        o_ref[...] = _rms_norm_rows(x1_ref[...], gfin_ref[...])
    else:
        o_ref[...] = x1_ref[...]


def _out_proj_ffn(mix_a, mix_b, mix_c, x2d, w_out, g_ffn, w_gate, w_up, w_down, g_final, final):
    n = x2d.shape[0]
    row = lambda width: pl.BlockSpec((FFN_TM, width), lambda i: (i, 0))
    return pl.pallas_call(
        functools.partial(_ffn_kernel, final=final),
        grid=(n // FFN_TM,),
        in_specs=[row(WIDTH_A), row(WIDTH_B), row(WIDTH_C), row(D_MODEL),
                  _resident(), _resident(), _resident(), _resident(), _resident(), _resident()],
        out_specs=row(D_MODEL),
        out_shape=jax.ShapeDtypeStruct((n, D_MODEL), F32),
        scratch_shapes=[
            pltpu.VMEM((FFN_TM, D_MODEL), F32),
            pltpu.VMEM((FFN_TM, D_MODEL), BF16),
            pltpu.VMEM((FFN_TM, D_FF), BF16),
        ],
        compiler_params=_compiler_params(1),
        name="out_proj_ffn",
    )(mix_a, mix_b, mix_c, x2d, w_out, g_ffn.reshape(1, D_MODEL), w_gate, w_up, w_down,
      g_final.reshape(1, D_MODEL))


def kernel(x, norm_mix_g, w_in, rpb, gmlp_ln_g, gmlp_ln_b, w_spatial, b_spatial, head_norm_g, w_out,
           norm_ffn_g, w_gate, w_up, w_down, final_norm_g):
    batch, seq, d_model = x.shape
    assert d_model == D_MODEL and seq % CHUNK == 0 and seq // GRID_W >= NA_ROWS
    depth = w_in.shape[0]
    seq_tab, ch_tab = _dft_tables(seq)
    x2d = x.reshape(batch * seq, D_MODEL)
    for l in range(depth):
        proj = _in_proj(x2d, norm_mix_g[l], w_in[l].astype(BF16))
        bias_tab = _attention_bias_table(rpb[l])
        mix_a = _attention(proj, bias_tab, head_norm_g[l], batch, seq)
        mix_b = _spatial_gate(proj, gmlp_ln_g[l], gmlp_ln_b[l], w_spatial[l], b_spatial[l],
                              head_norm_g[l], batch, seq)
        mix_c = _fourier_mix(proj, seq_tab, ch_tab, head_norm_g[l], batch, seq)
        x2d = _out_proj_ffn(mix_a, mix_b, mix_c, x2d, w_out[l].astype(BF16), norm_ffn_g[l],
                            w_gate[l].astype(BF16), w_up[l].astype(BF16), w_down[l].astype(BF16),
                            final_norm_g, final=(l == depth - 1))
    return x2d.reshape(batch, seq, D_MODEL)
```

```python
import functools
import math

import numpy as np
import jax
import jax.numpy as jnp
from jax import lax
from jax.experimental import pallas as pl
from jax.experimental.pallas import tpu as pltpu

F32 = jnp.float32
BF16 = jnp.bfloat16

D_MODEL = 1024
GRID_W = 64
HEAD_DIM = 64
WIDTH_A = D_MODEL // 2
WIDTH_B = D_MODEL // 4
WIDTH_C = D_MODEL // 4
N_HEADS_A = WIDTH_A // HEAD_DIM
N_HEADS_B = WIDTH_B // HEAD_DIM
IN_WIDTH = 3 * WIDTH_A + 2 * WIDTH_B + WIDTH_C
NA_ROWS = 8
NA_COLS = 16
CHUNK = 128
D_FF = -(-8 * D_MODEL // (3 * 256)) * 256
EPS = 1e-6
NEG_INF = -1e9

LANES = 128
VMEM_LIMIT_BYTES = 56 * 1024 * 1024

HEADS_PER_STEP = 4
KEY_SLAB = NA_ROWS * GRID_W
ATTN_GROUP = 2
GATE_UNROLL = 4
LOG2_E = math.log2(math.e)
Q_SCALE = HEAD_DIM ** -0.5 * LOG2_E

IN_TM = 512
FFN_TM = 512
N_CHUNK = 256


def _compiler_params(n_grid_dims):
    return pltpu.CompilerParams(
        dimension_semantics=("arbitrary",) * n_grid_dims,
        vmem_limit_bytes=VMEM_LIMIT_BYTES,
    )


def _resident():
    return pl.BlockSpec(memory_space=pltpu.VMEM)


def _layer_slab(stacked, layer, n_grid_dims):
    zeros = (0,) * (stacked.ndim - 1)
    index_map = {1: lambda i: (layer,) + zeros, 2: lambda i, j: (layer,) + zeros}[n_grid_dims]
    return pl.BlockSpec((None,) + stacked.shape[1:], index_map, pipeline_mode=pl.Buffered(1))


def _rms_norm_rows(x, g):
    ms = jnp.mean(x * x, axis=-1, keepdims=True)
    return x * lax.rsqrt(ms + EPS) * g


def _head_norm_lane_pairs(y, g):
    n_tiles = y.shape[-1] // LANES
    lane = lax.broadcasted_iota(jnp.int32, (1, LANES), 1)
    first = lane < HEAD_DIM
    outs = []
    for t in range(n_tiles):
        yt = y[:, t * LANES:(t + 1) * LANES]
        sq = yt * yt
        ss_a = jnp.sum(jnp.where(first, sq, 0.0), axis=-1, keepdims=True)
        ss_b = jnp.sum(jnp.where(first, 0.0, sq), axis=-1, keepdims=True)
        ms = jnp.where(first, ss_a, ss_b) * (1.0 / HEAD_DIM)
        outs.append(yt * lax.rsqrt(ms + EPS) * g[:, t * LANES:(t + 1) * LANES])
    return outs[0] if n_tiles == 1 else jnp.concatenate(outs, axis=-1)


def _in_proj_kernel(x_ref, g_ref, w_ref, o_ref):
    h = _rms_norm_rows(x_ref[...], g_ref[...]).astype(BF16)
    for c in range(IN_WIDTH // N_CHUNK):
        cols = slice(c * N_CHUNK, (c + 1) * N_CHUNK)
        acc = jnp.dot(h, w_ref[:, cols], preferred_element_type=F32)
        if (c + 1) * N_CHUNK <= WIDTH_A:
            acc = acc * Q_SCALE
        o_ref[:, cols] = acc.astype(o_ref.dtype)


def _in_proj(x2d, g, w_stack, layer):
    n = x2d.shape[0]
    return pl.pallas_call(
        _in_proj_kernel,
        grid=(n // IN_TM,),
        in_specs=[
            pl.BlockSpec((IN_TM, D_MODEL), lambda i: (i, 0)),
            _resident(),
            _layer_slab(w_stack, layer, 1),
        ],
        out_specs=pl.BlockSpec((IN_TM, IN_WIDTH), lambda i: (i, 0)),
        out_shape=jax.ShapeDtypeStruct((n, IN_WIDTH), BF16),
        compiler_params=_compiler_params(1),
        name="in_proj",
    )(x2d, g.reshape(1, D_MODEL), w_stack)


def _attn_kernel(q_ref, k_ref, v_ref, bias_ref, g_ref, o_ref, s_ref, p_ref, l_ref, o4_ref, *,
                 n_rows):
    width = HEADS_PER_STEP * HEAD_DIM
    head_of_lane = lax.broadcasted_iota(jnp.int32, (1, width), 1) // HEAD_DIM
    g = g_ref[...]
    n_groups = n_rows // ATTN_GROUP

    def rows_of(group):
        for j in range(ATTN_GROUP):
            r = group * ATTN_GROUP + j
            rs = jnp.clip(r - NA_ROWS // 2, 0, n_rows - NA_ROWS)
            yield j, r, rs, pl.multiple_of(r * GRID_W, GRID_W), pl.multiple_of(rs * GRID_W, GRID_W)

    def scores(group, slot):
        for j, _, _, q0, k0 in rows_of(group):
            q = q_ref[pl.ds(q0, GRID_W), :]
            zero = jnp.zeros_like(q)
            qm = jnp.concatenate(
                [jnp.where(head_of_lane == h, q, zero) for h in range(HEADS_PER_STEP)], axis=0)
            s_ref[slot, j] = lax.dot_general(qm, k_ref[pl.ds(k0, KEY_SLAB), :],
                                             (((1,), (1,)), ((), ())), preferred_element_type=F32)

    def softmax(group, slot):
        for j, r, rs, _, _ in rows_of(group):
            a0 = NA_ROWS - 1 - (r - rs)
            bias = jnp.concatenate(
                [jnp.concatenate([bias_ref[h, a0 + 2 * i] for i in range(NA_ROWS // 2)], axis=-1)
                 for h in range(HEADS_PER_STEP)], axis=0)
            s = s_ref[slot, j] + bias
            m = jnp.max(s, axis=-1, keepdims=True)
            p = jnp.exp2(s - m)
            l_ref[slot, j] = jnp.sum(p, axis=-1, keepdims=True)
            p_ref[slot, j] = p.astype(BF16)

    def weighted_values(group, slot):
        for j, _, _, _, k0 in rows_of(group):
            o4 = jnp.dot(p_ref[slot, j], v_ref[pl.ds(k0, KEY_SLAB), :], preferred_element_type=F32)
            o4_ref[slot, j] = o4 * (1.0 / l_ref[slot, j])

    def finish(group, slot):
        for j, _, _, q0, _ in rows_of(group):
            o = o4_ref[slot, j, :GRID_W]
            for h in range(1, HEADS_PER_STEP):
                o = jnp.where(head_of_lane == h, o4_ref[slot, j, h * GRID_W:(h + 1) * GRID_W], o)
            o_ref[pl.ds(q0, GRID_W), :] = _head_norm_lane_pairs(o, g).astype(o_ref.dtype)

    stages = (scores, softmax, weighted_values, finish)

    def step(t, parity, first_stage=0, last_stage=len(stages) - 1):
        for d in range(last_stage, first_stage - 1, -1):
            stages[d](t - d, (parity + d) % 2)

    n_fill = len(stages) - 1
    for t in range(n_fill):
        step(t, t % 2, last_stage=t)

    def two_steps(i, carry):
        t = n_fill + 2 * i
        step(t, n_fill % 2)
        step(t + 1, (n_fill + 1) % 2)
        return carry

    n_steady = n_groups - n_fill
    lax.fori_loop(0, n_steady // 2, two_steps, 0)
    for t in range(n_fill + 2 * (n_steady // 2), n_groups):
        step(t, t % 2)
    for t in range(n_groups, n_groups + n_fill):
        step(t, t % 2, first_stage=t - n_groups + 1)


def _attention(proj, bias_tab, g, layer, batch, seq):
    width = HEADS_PER_STEP * HEAD_DIM
    n_groups = WIDTH_A // width
    k_off = WIDTH_A // width
    v_off = 2 * WIDTH_A // width
    return pl.pallas_call(
        functools.partial(_attn_kernel, n_rows=seq // GRID_W),
        grid=(n_groups, batch),
        in_specs=[
            pl.BlockSpec((seq, width), lambda hg, b: (b, hg)),
            pl.BlockSpec((seq, width), lambda hg, b: (b, k_off + hg)),
            pl.BlockSpec((seq, width), lambda hg, b: (b, v_off + hg)),
            pl.BlockSpec((None, HEADS_PER_STEP) + bias_tab.shape[2:],
                         lambda hg, b: (layer, hg, 0, 0, 0)),
            pl.BlockSpec((1, width), lambda hg, b: (0, hg)),
        ],
        out_specs=pl.BlockSpec((seq, width), lambda hg, b: (b, hg)),
        out_shape=jax.ShapeDtypeStruct((batch * seq, WIDTH_A), BF16),
        scratch_shapes=[
            pltpu.VMEM((2, ATTN_GROUP, HEADS_PER_STEP * GRID_W, KEY_SLAB), F32),
            pltpu.VMEM((2, ATTN_GROUP, HEADS_PER_STEP * GRID_W, KEY_SLAB), BF16),
            pltpu.VMEM((2, ATTN_GROUP, HEADS_PER_STEP * GRID_W, 1), F32),
            pltpu.VMEM((2, ATTN_GROUP, HEADS_PER_STEP * GRID_W, width), F32),
        ],
        compiler_params=_compiler_params(2),
        name="nbr_attention",
    )(proj, proj, proj, bias_tab, g.reshape(1, D_MODEL)[:, :WIDTH_A])


def _attention_bias_table(rpb):
    qc = np.arange(GRID_W)[:, None]
    kc = np.arange(GRID_W)[None, :]
    win = np.clip(qc - NA_COLS // 2, 0, GRID_W - NA_COLS)
    col_ok = (kc >= win) & (kc < win + NA_COLS)
    pad = GRID_W - NA_COLS
    padded = jnp.pad(rpb.astype(F32), ((0, 0), (0, 0), (0, 0), (pad, pad)))
    tiles = jnp.stack([padded[..., GRID_W - 1 - c:2 * GRID_W - 1 - c] for c in range(GRID_W)], axis=-2)
    tiles = jnp.where(jnp.asarray(col_ok), tiles, NEG_INF) * LOG2_E
    return jnp.concatenate([tiles[:, :, :-1], tiles[:, :, 1:]], axis=-1)


def _gelu_tanh(x):
    c = math.sqrt(2.0 / math.pi)
    return x * (0.5 * (1.0 + jnp.tanh(c * (x + 0.044715 * (x * x * x)))))


def _gate_kernel(z_ref, lng_ref, lnb_ref, wcat_ref, bs_ref, g_ref, o_ref, *, n_chunks):
    lane = lax.broadcasted_iota(jnp.int32, (1, WIDTH_B), 1)
    head_of_lane = lane // HEAD_DIM

    def chunk(c, carry):
        t0 = pl.multiple_of(c * CHUNK, CHUNK)
        z = _gelu_tanh(z_ref[pl.ds(t0, CHUNK), :].astype(F32))
        u = z[:, :WIDTH_B]
        v = z[:, WIDTH_B:]
        mu = jnp.mean(v, axis=-1, keepdims=True)
        d = v - mu
        var = jnp.mean(d * d, axis=-1, keepdims=True)
        vn = (d * lax.rsqrt(var + EPS) * lng_ref[...] + lnb_ref[...]).astype(BF16)
        stacked = jnp.concatenate(
            [jnp.where(head_of_lane == h, vn, jnp.zeros_like(vn)) for h in range(N_HEADS_B)], axis=0)
        mixed = jnp.dot(wcat_ref[...], stacked, preferred_element_type=F32) + bs_ref[...]
        o_ref[pl.ds(t0, CHUNK), :] = _head_norm_lane_pairs(u * mixed, g_ref[...]).astype(o_ref.dtype)
        return carry

    lax.fori_loop(0, n_chunks, chunk, 0, unroll=GATE_UNROLL)


def _spatial_gate(proj, ln_g, ln_b, w_s, b_s, g, batch, seq):
    z_off = 3 * WIDTH_A // (2 * WIDTH_B)
    wcat = jnp.transpose(w_s, (1, 0, 2)).reshape(CHUNK, N_HEADS_B * CHUNK).astype(BF16)
    bs_full = jnp.repeat(b_s.T.astype(F32), HEAD_DIM, axis=1)
    g_b = g.reshape(1, D_MODEL)[:, WIDTH_A:WIDTH_A + WIDTH_B]
    return pl.pallas_call(
        functools.partial(_gate_kernel, n_chunks=seq // CHUNK),
        grid=(batch,),
        in_specs=[
            pl.BlockSpec((seq, 2 * WIDTH_B), lambda b: (b, z_off)),
            _resident(), _resident(), _resident(), _resident(), _resident(),
        ],
        out_specs=pl.BlockSpec((seq, WIDTH_B), lambda b: (b, 0)),
        out_shape=jax.ShapeDtypeStruct((batch * seq, WIDTH_B), BF16),
        compiler_params=_compiler_params(1),
        name="spatial_gate",
    )(proj, ln_g.reshape(1, WIDTH_B), ln_b.reshape(1, WIDTH_B), wcat, bs_full, g_b)


def _dft_tables(seq):
    def cos_sin(n):
        i = lax.broadcasted_iota(jnp.int32, (n, n), 0)
        j = lax.broadcasted_iota(jnp.int32, (n, n), 1)
        ang = ((i * j) % n).astype(F32) * (2.0 * math.pi / n)
        return jnp.cos(ang), jnp.sin(ang)

    def cos_sin_rows(row_step, n_rows):
        i = lax.broadcasted_iota(jnp.int32, (n_rows, seq), 0) * row_step
        j = lax.broadcasted_iota(jnp.int32, (n_rows, seq), 1)
        ang = ((i * j) % seq).astype(F32) * (2.0 * math.pi / seq)
        return jnp.cos(ang), jnp.sin(ang)

    ca, sa = (t[:, None, :] for t in cos_sin_rows(GRID_W, seq // GRID_W))
    cb, sb = (t[None, :, :] for t in cos_sin_rows(1, GRID_W))
    c_seq = (ca * cb - sa * sb).reshape(seq, seq)
    s_seq = (sa * cb + ca * sb).reshape(seq, seq)
    seq_tab = jnp.concatenate([c_seq, -s_seq], axis=1).astype(BF16)
    c_ch, s_ch = cos_sin(HEAD_DIM)
    eye = jnp.eye(WIDTH_C // HEAD_DIM, dtype=F32)
    ch_tab = jnp.concatenate([jnp.kron(eye, c_ch), jnp.kron(eye, s_ch)], axis=1).astype(BF16)
    return seq_tab, ch_tab


def _fourier_kernel(x_ref, seq_tab_ref, ch_tab_ref, g_ref, o_ref, *, seq):
    xcs = jnp.dot(x_ref[...], ch_tab_ref[...], preferred_element_type=F32)
    stacked = jnp.concatenate([xcs[:, :WIDTH_C], xcs[:, WIDTH_C:]], axis=0).astype(BF16)
    scale = 1.0 / math.sqrt(seq * HEAD_DIM)
    for t in range(seq // N_CHUNK):
        rows = slice(t * N_CHUNK, (t + 1) * N_CHUNK)
        y = jnp.dot(seq_tab_ref[rows, :], stacked, preferred_element_type=F32) * scale
        o_ref[rows, :] = _head_norm_lane_pairs(y, g_ref[...]).astype(o_ref.dtype)


def _fourier_mix(proj, seq_tab, ch_tab, g, batch, seq):
    c_off = (3 * WIDTH_A + 2 * WIDTH_B) // WIDTH_C
    g_c = g.reshape(1, D_MODEL)[:, WIDTH_A + WIDTH_B:]
    return pl.pallas_call(
        functools.partial(_fourier_kernel, seq=seq),
        grid=(batch,),
        in_specs=[
            pl.BlockSpec((seq, WIDTH_C), lambda b: (b, c_off)),
            _resident(), _resident(), _resident(),
        ],
        out_specs=pl.BlockSpec((seq, WIDTH_C), lambda b: (b, 0)),
        out_shape=jax.ShapeDtypeStruct((batch * seq, WIDTH_C), BF16),
        compiler_params=_compiler_params(1),
        name="fourier_mix",
    )(proj, seq_tab, ch_tab, g_c)


def _ffn_kernel(ma_ref, mb_ref, mc_ref, x_ref, wo_ref, gf_ref, wg_ref, wu_ref, wd_ref, gfin_ref,
                o_ref, x1_ref, h_ref, act_ref, *, final):
    mix = jnp.concatenate([ma_ref[...], mb_ref[...], mc_ref[...]], axis=-1)
    for c in range(D_MODEL // N_CHUNK):
        cols = slice(c * N_CHUNK, (c + 1) * N_CHUNK)
        x1_ref[:, cols] = x_ref[:, cols] + jnp.dot(mix, wo_ref[:, cols], preferred_element_type=F32)
    h_ref[...] = _rms_norm_rows(x1_ref[...], gf_ref[...]).astype(BF16)
    for c in range(D_FF // N_CHUNK):
        cols = slice(c * N_CHUNK, (c + 1) * N_CHUNK)
        gate = jnp.dot(h_ref[...], wg_ref[:, cols], preferred_element_type=F32)
        up = jnp.dot(h_ref[...], wu_ref[:, cols], preferred_element_type=F32)
        act_ref[:, cols] = (gate * jax.nn.sigmoid(gate) * up).astype(BF16)
    for c in range(D_MODEL // N_CHUNK):
        cols = slice(c * N_CHUNK, (c + 1) * N_CHUNK)
        x1_ref[:, cols] = x1_ref[:, cols] + jnp.dot(act_ref[...], wd_ref[:, cols],
                                                    preferred_element_type=F32)
    if final:
        o_ref[...] = _rms_norm_rows(x1_ref[...], gfin_ref[...])
    else:
        o_ref[...] = x1_ref[...]


def _out_proj_ffn(mix_a, mix_b, mix_c, x2d, w_out, g_ffn, w_gate, w_up, w_down, g_final, layer,
                  final):
    n = x2d.shape[0]
    row = lambda width: pl.BlockSpec((FFN_TM, width), lambda i: (i, 0))
    slab = lambda w: _layer_slab(w, layer, 1)
    return pl.pallas_call(
        functools.partial(_ffn_kernel, final=final),
        grid=(n // FFN_TM,),
        in_specs=[row(WIDTH_A), row(WIDTH_B), row(WIDTH_C), row(D_MODEL),
                  slab(w_out), _resident(), slab(w_gate), slab(w_up), slab(w_down), _resident()],
        out_specs=row(D_MODEL),
        out_shape=jax.ShapeDtypeStruct((n, D_MODEL), F32),
        scratch_shapes=[
            pltpu.VMEM((FFN_TM, D_MODEL), F32),
            pltpu.VMEM((FFN_TM, D_MODEL), BF16),
            pltpu.VMEM((FFN_TM, D_FF), BF16),
        ],
        compiler_params=_compiler_params(1),
        name="out_proj_ffn",
    )(mix_a, mix_b, mix_c, x2d, w_out, g_ffn.reshape(1, D_MODEL), w_gate, w_up, w_down,
      g_final.reshape(1, D_MODEL))


def kernel(x, norm_mix_g, w_in, rpb, gmlp_ln_g, gmlp_ln_b, w_spatial, b_spatial, head_norm_g, w_out,
           norm_ffn_g, w_gate, w_up, w_down, final_norm_g):
    batch, seq, d_model = x.shape
    assert d_model == D_MODEL and seq % CHUNK == 0 and seq // GRID_W >= NA_ROWS
    depth = w_in.shape[0]
    seq_tab, ch_tab = _dft_tables(seq)
    bias_tab = _attention_bias_table(rpb)
    w_in, w_out, w_gate, w_up, w_down = (w.astype(BF16) for w in (w_in, w_out, w_gate, w_up, w_down))
    x2d = x.reshape(batch * seq, D_MODEL)
    for l in range(depth):
        proj = _in_proj(x2d, norm_mix_g[l], w_in, l)
        mix_a = _attention(proj, bias_tab, head_norm_g[l], l, batch, seq)
        mix_b = _spatial_gate(proj, gmlp_ln_g[l], gmlp_ln_b[l], w_spatial[l], b_spatial[l],
                              head_norm_g[l], batch, seq)
        mix_c = _fourier_mix(proj, seq_tab, ch_tab, head_norm_g[l], batch, seq)
        x2d = _out_proj_ffn(mix_a, mix_b, mix_c, x2d, w_out, norm_ffn_g[l], w_gate, w_up, w_down,
                            final_norm_g, l, final=(l == depth - 1))
    return x2d.reshape(batch, seq, D_MODEL)
```

```python
import functools
import math

import numpy as np
import jax
import jax.numpy as jnp
from jax import lax
from jax.experimental import pallas as pl
from jax.experimental.pallas import tpu as pltpu

F32 = jnp.float32
BF16 = jnp.bfloat16

D_MODEL = 1024
GRID_W = 64
HEAD_DIM = 64
WIDTH_A = D_MODEL // 2
WIDTH_B = D_MODEL // 4
WIDTH_C = D_MODEL // 4
N_HEADS_A = WIDTH_A // HEAD_DIM
N_HEADS_B = WIDTH_B // HEAD_DIM
IN_WIDTH = 3 * WIDTH_A + 2 * WIDTH_B + WIDTH_C
NA_ROWS = 8
NA_COLS = 16
CHUNK = 128
D_FF = -(-8 * D_MODEL // (3 * 256)) * 256
EPS = 1e-6
NEG_INF = -1e9

LANES = 128
VMEM_LIMIT_BYTES = 56 * 1024 * 1024

HEADS_PER_STEP = 4
KEY_SLAB = NA_ROWS * GRID_W
ATTN_GROUP = 4
GATE_UNROLL = 4
LOG2_E = math.log2(math.e)
Q_SCALE = HEAD_DIM ** -0.5 * LOG2_E

IN_TM = 512
FFN_TM = 512
N_CHUNK = 256


def _compiler_params(n_grid_dims):
    return pltpu.CompilerParams(
        dimension_semantics=("arbitrary",) * n_grid_dims,
        vmem_limit_bytes=VMEM_LIMIT_BYTES,
    )


def _resident():
    return pl.BlockSpec(memory_space=pltpu.VMEM)


def _layer_slab(stacked, layer, n_grid_dims):
    zeros = (0,) * (stacked.ndim - 1)
    index_map = {1: lambda i: (layer,) + zeros, 2: lambda i, j: (layer,) + zeros}[n_grid_dims]
    return pl.BlockSpec((None,) + stacked.shape[1:], index_map, pipeline_mode=pl.Buffered(1))


def _rms_norm_rows(x, g):
    ms = jnp.mean(x * x, axis=-1, keepdims=True)
    return x * lax.rsqrt(ms + EPS) * g


def _head_norm_lane_pairs(y, g):
    n_tiles = y.shape[-1] // LANES
    lane = lax.broadcasted_iota(jnp.int32, (1, LANES), 1)
    first = lane < HEAD_DIM
    outs = []
    for t in range(n_tiles):
        yt = y[:, t * LANES:(t + 1) * LANES]
        sq = yt * yt
        ss_a = jnp.sum(jnp.where(first, sq, 0.0), axis=-1, keepdims=True)
        ss_b = jnp.sum(jnp.where(first, 0.0, sq), axis=-1, keepdims=True)
        ms = jnp.where(first, ss_a, ss_b) * (1.0 / HEAD_DIM)
        outs.append(yt * lax.rsqrt(ms + EPS) * g[:, t * LANES:(t + 1) * LANES])
    return outs[0] if n_tiles == 1 else jnp.concatenate(outs, axis=-1)


def _head_group_ones(width):
    row = lax.broadcasted_iota(jnp.int32, (width, width), 0) // HEAD_DIM
    col = lax.broadcasted_iota(jnp.int32, (width, width), 1) // HEAD_DIM
    return (row == col).astype(BF16)


def _head_norm_mxu(y, group_ones, g):
    m = y.shape[0]
    sq = y * y
    hi = sq.astype(BF16)
    lo = (sq - hi.astype(F32)).astype(BF16)
    sums = jnp.dot(jnp.concatenate([hi, lo], axis=0), group_ones, preferred_element_type=F32)
    ms = (sums[:m] + sums[m:]) * (1.0 / HEAD_DIM)
    return y * lax.rsqrt(ms + EPS) * g


def _in_proj_kernel(x_ref, g_ref, w_ref, o_ref):
    h = _rms_norm_rows(x_ref[...], g_ref[...]).astype(BF16)
    for c in range(IN_WIDTH // N_CHUNK):
        cols = slice(c * N_CHUNK, (c + 1) * N_CHUNK)
        acc = jnp.dot(h, w_ref[:, cols], preferred_element_type=F32)
        if (c + 1) * N_CHUNK <= WIDTH_A:
            acc = acc * Q_SCALE
        o_ref[:, cols] = acc.astype(o_ref.dtype)


def _in_proj(x2d, g, w_stack, layer):
    n = x2d.shape[0]
    return pl.pallas_call(
        _in_proj_kernel,
        grid=(n // IN_TM,),
        in_specs=[
            pl.BlockSpec((IN_TM, D_MODEL), lambda i: (i, 0)),
            _resident(),
            _layer_slab(w_stack, layer, 1),
        ],
        out_specs=pl.BlockSpec((IN_TM, IN_WIDTH), lambda i: (i, 0)),
        out_shape=jax.ShapeDtypeStruct((n, IN_WIDTH), BF16),
        compiler_params=_compiler_params(1),
        name="in_proj",
    )(x2d, g.reshape(1, D_MODEL), w_stack)


def _attn_kernel(q_ref, k_ref, v_ref, bias_ref, g_ref, o_ref, p_ref, l_ref, *, n_rows):
    width = HEADS_PER_STEP * HEAD_DIM
    head_of_lane = lax.broadcasted_iota(jnp.int32, (1, width), 1) // HEAD_DIM
    g = g_ref[...]
    n_groups = n_rows // ATTN_GROUP

    def rows_of(group):
        for j in range(ATTN_GROUP):
            r = group * ATTN_GROUP + j
            rs = jnp.clip(r - NA_ROWS // 2, 0, n_rows - NA_ROWS)
            yield j, r, rs, pl.multiple_of(r * GRID_W, GRID_W), pl.multiple_of(rs * GRID_W, GRID_W)

    def probs(group, slot):
        for j, r, rs, q0, k0 in rows_of(group):
            q = q_ref[pl.ds(q0, GRID_W), :]
            zero = jnp.zeros_like(q)
            qm = jnp.concatenate(
                [jnp.where(head_of_lane == h, q, zero) for h in range(HEADS_PER_STEP)], axis=0)
            s = lax.dot_general(qm, k_ref[pl.ds(k0, KEY_SLAB), :], (((1,), (1,)), ((), ())),
                                preferred_element_type=F32)
            a0 = NA_ROWS - 1 - (r - rs)
            s = s + jnp.concatenate(
                [jnp.concatenate([bias_ref[h, a0 + 2 * i] for i in range(NA_ROWS // 2)], axis=-1)
                 for h in range(HEADS_PER_STEP)], axis=0)
            m = jnp.max(s, axis=-1, keepdims=True)
            p = jnp.exp2(s - m)
            l = jnp.sum(p, axis=-1, keepdims=True)
            l_ref[slot, j] = jnp.broadcast_to(l, (l.shape[0], LANES))
            p_ref[slot, j] = p.astype(BF16)

    def outputs(group, slot):
        for j, _, _, q0, k0 in rows_of(group):
            o4 = jnp.dot(p_ref[slot, j], v_ref[pl.ds(k0, KEY_SLAB), :], preferred_element_type=F32)
            inv_l = 1.0 / l_ref[slot, j]
            o4 = o4 * jnp.concatenate([inv_l] * (width // LANES), axis=-1)
            o = o4[:GRID_W]
            for h in range(1, HEADS_PER_STEP):
                o = jnp.where(head_of_lane == h, o4[h * GRID_W:(h + 1) * GRID_W], o)
            o_ref[pl.ds(q0, GRID_W), :] = _head_norm_lane_pairs(o, g).astype(o_ref.dtype)

    stages = (probs, outputs)

    def step(t, parity, first_stage=0, last_stage=len(stages) - 1):
        for d in range(last_stage, first_stage - 1, -1):
            stages[d](t - d, (parity + d) % 2)

    n_fill = len(stages) - 1
    for t in range(n_fill):
        step(t, t % 2, last_stage=t)

    def two_steps(i, carry):
        t = n_fill + 2 * i
        step(t, n_fill % 2)
        step(t + 1, (n_fill + 1) % 2)
        return carry

    n_steady = n_groups - n_fill
    lax.fori_loop(0, n_steady // 2, two_steps, 0)
    for t in range(n_fill + 2 * (n_steady // 2), n_groups):
        step(t, t % 2)
    for t in range(n_groups, n_groups + n_fill):
        step(t, t % 2, first_stage=t - n_groups + 1)


def _attention(proj, bias_tab, g, layer, batch, seq):
    width = HEADS_PER_STEP * HEAD_DIM
    n_groups = WIDTH_A // width
    k_off = WIDTH_A // width
    v_off = 2 * WIDTH_A // width
    return pl.pallas_call(
        functools.partial(_attn_kernel, n_rows=seq // GRID_W),
        grid=(n_groups, batch),
        in_specs=[
            pl.BlockSpec((seq, width), lambda hg, b: (b, hg)),
            pl.BlockSpec((seq, width), lambda hg, b: (b, k_off + hg)),
            pl.BlockSpec((seq, width), lambda hg, b: (b, v_off + hg)),
            pl.BlockSpec((None, HEADS_PER_STEP) + bias_tab.shape[2:],
                         lambda hg, b: (layer, hg, 0, 0, 0)),
            pl.BlockSpec((1, width), lambda hg, b: (0, hg)),
        ],
        out_specs=pl.BlockSpec((seq, width), lambda hg, b: (b, hg)),
        out_shape=jax.ShapeDtypeStruct((batch * seq, WIDTH_A), BF16),
        scratch_shapes=[
            pltpu.VMEM((2, ATTN_GROUP, HEADS_PER_STEP * GRID_W, KEY_SLAB), BF16),
            pltpu.VMEM((2, ATTN_GROUP, HEADS_PER_STEP * GRID_W, LANES), F32),
        ],
        compiler_params=_compiler_params(2),
        name="nbr_attention",
    )(proj, proj, proj, bias_tab, g.reshape(1, D_MODEL)[:, :WIDTH_A])


def _attention_bias_table(rpb):
    qc = np.arange(GRID_W)[:, None]
    kc = np.arange(GRID_W)[None, :]
    win = np.clip(qc - NA_COLS // 2, 0, GRID_W - NA_COLS)
    col_ok = (kc >= win) & (kc < win + NA_COLS)
    pad = GRID_W - NA_COLS
    padded = jnp.pad(rpb.astype(F32), ((0, 0), (0, 0), (0, 0), (pad, pad)))
    tiles = jnp.stack([padded[..., GRID_W - 1 - c:2 * GRID_W - 1 - c] for c in range(GRID_W)], axis=-2)
    tiles = jnp.where(jnp.asarray(col_ok), tiles, NEG_INF) * LOG2_E
    return jnp.concatenate([tiles[:, :, :-1], tiles[:, :, 1:]], axis=-1)


def _gelu_tanh(x):
    c = math.sqrt(2.0 / math.pi)
    return x * (0.5 * (1.0 + jnp.tanh(c * (x + 0.044715 * (x * x * x)))))


def _gate_kernel(z_ref, lng_ref, lnb_ref, wcat_ref, bs_ref, g_ref, o_ref, *, n_chunks):
    lane = lax.broadcasted_iota(jnp.int32, (1, WIDTH_B), 1)
    head_of_lane = lane // HEAD_DIM

    def chunk(c, carry):
        t0 = pl.multiple_of(c * CHUNK, CHUNK)
        z = _gelu_tanh(z_ref[pl.ds(t0, CHUNK), :].astype(F32))
        u = z[:, :WIDTH_B]
        v = z[:, WIDTH_B:]
        mu = jnp.mean(v, axis=-1, keepdims=True)
        d = v - mu
        var = jnp.mean(d * d, axis=-1, keepdims=True)
        vn = (d * lax.rsqrt(var + EPS) * lng_ref[...] + lnb_ref[...]).astype(BF16)
        stacked = jnp.concatenate(
            [jnp.where(head_of_lane == h, vn, jnp.zeros_like(vn)) for h in range(N_HEADS_B)], axis=0)
        mixed = jnp.dot(wcat_ref[...], stacked, preferred_element_type=F32) + bs_ref[...]
        o_ref[pl.ds(t0, CHUNK), :] = _head_norm_lane_pairs(u * mixed, g_ref[...]).astype(o_ref.dtype)
        return carry

    lax.fori_loop(0, n_chunks, chunk, 0, unroll=GATE_UNROLL)


def _spatial_gate(proj, ln_g, ln_b, w_s, b_s, g, batch, seq):
    z_off = 3 * WIDTH_A // (2 * WIDTH_B)
    wcat = jnp.transpose(w_s, (1, 0, 2)).reshape(CHUNK, N_HEADS_B * CHUNK).astype(BF16)
    bs_full = jnp.repeat(b_s.T.astype(F32), HEAD_DIM, axis=1)
    g_b = g.reshape(1, D_MODEL)[:, WIDTH_A:WIDTH_A + WIDTH_B]
    return pl.pallas_call(
        functools.partial(_gate_kernel, n_chunks=seq // CHUNK),
        grid=(batch,),
        in_specs=[
            pl.BlockSpec((seq, 2 * WIDTH_B), lambda b: (b, z_off)),
            _resident(), _resident(), _resident(), _resident(), _resident(),
        ],
        out_specs=pl.BlockSpec((seq, WIDTH_B), lambda b: (b, 0)),
        out_shape=jax.ShapeDtypeStruct((batch * seq, WIDTH_B), BF16),
        compiler_params=_compiler_params(1),
        name="spatial_gate",
    )(proj, ln_g.reshape(1, WIDTH_B), ln_b.reshape(1, WIDTH_B), wcat, bs_full, g_b)


def _dft_tables(seq):
    def cos_sin(n):
        i = lax.broadcasted_iota(jnp.int32, (n, n), 0)
        j = lax.broadcasted_iota(jnp.int32, (n, n), 1)
        ang = ((i * j) % n).astype(F32) * (2.0 * math.pi / n)
        return jnp.cos(ang), jnp.sin(ang)

    def cos_sin_rows(row_step, n_rows):
        i = lax.broadcasted_iota(jnp.int32, (n_rows, seq), 0) * row_step
        j = lax.broadcasted_iota(jnp.int32, (n_rows, seq), 1)
        ang = ((i * j) % seq).astype(F32) * (2.0 * math.pi / seq)
        return jnp.cos(ang), jnp.sin(ang)

    ca, sa = (t[:, None, :] for t in cos_sin_rows(GRID_W, seq // GRID_W))
    cb, sb = (t[None, :, :] for t in cos_sin_rows(1, GRID_W))
    c_seq = (ca * cb - sa * sb).reshape(seq, seq)
    s_seq = (sa * cb + ca * sb).reshape(seq, seq)
    seq_tab = jnp.concatenate([c_seq, -s_seq], axis=1).astype(BF16)
    c_ch, s_ch = cos_sin(HEAD_DIM)
    eye = jnp.eye(WIDTH_C // HEAD_DIM, dtype=F32)
    ch_tab = jnp.concatenate([jnp.kron(eye, c_ch), jnp.kron(eye, s_ch)], axis=1).astype(BF16)
    return seq_tab, ch_tab


def _fourier_kernel(x_ref, seq_tab_ref, ch_tab_ref, g_ref, o_ref, *, seq):
    xcs = jnp.dot(x_ref[...], ch_tab_ref[...], preferred_element_type=F32)
    stacked = jnp.concatenate([xcs[:, :WIDTH_C], xcs[:, WIDTH_C:]], axis=0).astype(BF16)
    scale = 1.0 / math.sqrt(seq * HEAD_DIM)
    for t in range(seq // N_CHUNK):
        rows = slice(t * N_CHUNK, (t + 1) * N_CHUNK)
        y = jnp.dot(seq_tab_ref[rows, :], stacked, preferred_element_type=F32) * scale
        o_ref[rows, :] = _head_norm_lane_pairs(y, g_ref[...]).astype(o_ref.dtype)


def _fourier_mix(proj, seq_tab, ch_tab, g, batch, seq):
    c_off = (3 * WIDTH_A + 2 * WIDTH_B) // WIDTH_C
    g_c = g.reshape(1, D_MODEL)[:, WIDTH_A + WIDTH_B:]
    return pl.pallas_call(
        functools.partial(_fourier_kernel, seq=seq),
        grid=(batch,),
        in_specs=[
            pl.BlockSpec((seq, WIDTH_C), lambda b: (b, c_off)),
            _resident(), _resident(), _resident(),
        ],
        out_specs=pl.BlockSpec((seq, WIDTH_C), lambda b: (b, 0)),
        out_shape=jax.ShapeDtypeStruct((batch * seq, WIDTH_C), BF16),
        compiler_params=_compiler_params(1),
        name="fourier_mix",
    )(proj, seq_tab, ch_tab, g_c)


def _ffn_kernel(ma_ref, mb_ref, mc_ref, x_ref, wo_ref, gf_ref, wg_ref, wu_ref, wd_ref, gfin_ref,
                o_ref, x1_ref, h_ref, act_ref, *, final):
    mix = jnp.concatenate([ma_ref[...], mb_ref[...], mc_ref[...]], axis=-1)
    for c in range(D_MODEL // N_CHUNK):
        cols = slice(c * N_CHUNK, (c + 1) * N_CHUNK)
        x1_ref[:, cols] = x_ref[:, cols] + jnp.dot(mix, wo_ref[:, cols], preferred_element_type=F32)
    h_ref[...] = _rms_norm_rows(x1_ref[...], gf_ref[...]).astype(BF16)
    for c in range(D_FF // N_CHUNK):
        cols = slice(c * N_CHUNK, (c + 1) * N_CHUNK)
        gate = jnp.dot(h_ref[...], wg_ref[:, cols], preferred_element_type=F32)
        up = jnp.dot(h_ref[...], wu_ref[:, cols], preferred_element_type=F32)
        act_ref[:, cols] = (gate * jax.nn.sigmoid(gate) * up).astype(BF16)
    for c in range(D_MODEL // N_CHUNK):
        cols = slice(c * N_CHUNK, (c + 1) * N_CHUNK)
        x1_ref[:, cols] = x1_ref[:, cols] + jnp.dot(act_ref[...], wd_ref[:, cols],
                                                    preferred_element_type=F32)
    if final:
        o_ref[...] = _rms_norm_rows(x1_ref[...], gfin_ref[...])
    else:
        o_ref[...] = x1_ref[...]


def _out_proj_ffn(mix_a, mix_b, mix_c, x2d, w_out, g_ffn, w_gate, w_up, w_down, g_final, layer,
                  final):
    n = x2d.shape[0]
    row = lambda width: pl.BlockSpec((FFN_TM, width), lambda i: (i, 0))
    slab = lambda w: _layer_slab(w, layer, 1)
    return pl.pallas_call(
        functools.partial(_ffn_kernel, final=final),
        grid=(n // FFN_TM,),
        in_specs=[row(WIDTH_A), row(WIDTH_B), row(WIDTH_C), row(D_MODEL),
                  slab(w_out), _resident(), slab(w_gate), slab(w_up), slab(w_down), _resident()],
        out_specs=row(D_MODEL),
        out_shape=jax.ShapeDtypeStruct((n, D_MODEL), F32),
        scratch_shapes=[
            pltpu.VMEM((FFN_TM, D_MODEL), F32),
            pltpu.VMEM((FFN_TM, D_MODEL), BF16),
            pltpu.VMEM((FFN_TM, D_FF), BF16),
        ],
        compiler_params=_compiler_params(1),
        name="out_proj_ffn",
    )(mix_a, mix_b, mix_c, x2d, w_out, g_ffn.reshape(1, D_MODEL), w_gate, w_up, w_down,
      g_final.reshape(1, D_MODEL))


def kernel(x, norm_mix_g, w_in, rpb, gmlp_ln_g, gmlp_ln_b, w_spatial, b_spatial, head_norm_g, w_out,
           norm_ffn_g, w_gate, w_up, w_down, final_norm_g):
    batch, seq, d_model = x.shape
    assert d_model == D_MODEL and seq % CHUNK == 0 and seq // GRID_W >= NA_ROWS
    depth = w_in.shape[0]
    seq_tab, ch_tab = _dft_tables(seq)
    bias_tab = _attention_bias_table(rpb)
    w_in, w_out, w_gate, w_up, w_down = (w.astype(BF16) for w in (w_in, w_out, w_gate, w_up, w_down))
    x2d = x.reshape(batch * seq, D_MODEL)
    for l in range(depth):
        proj = _in_proj(x2d, norm_mix_g[l], w_in, l)
        mix_a = _attention(proj, bias_tab, head_norm_g[l], l, batch, seq)
        mix_b = _spatial_gate(proj, gmlp_ln_g[l], gmlp_ln_b[l], w_spatial[l], b_spatial[l],
                              head_norm_g[l], batch, seq)
        mix_c = _fourier_mix(proj, seq_tab, ch_tab, head_norm_g[l], batch, seq)
        x2d = _out_proj_ffn(mix_a, mix_b, mix_c, x2d, w_out, norm_ffn_g[l], w_gate, w_up, w_down,
                            final_norm_g, l, final=(l == depth - 1))
    return x2d.reshape(batch, seq, D_MODEL)
```

```python
import functools
import math

import numpy as np
import jax
import jax.numpy as jnp
from jax import lax
from jax.experimental import pallas as pl
from jax.experimental.pallas import tpu as pltpu

F32 = jnp.float32
BF16 = jnp.bfloat16

D_MODEL = 1024
GRID_W = 64
HEAD_DIM = 64
WIDTH_A = D_MODEL // 2
WIDTH_B = D_MODEL // 4
WIDTH_C = D_MODEL // 4
N_HEADS_A = WIDTH_A // HEAD_DIM
N_HEADS_B = WIDTH_B // HEAD_DIM
IN_WIDTH = 3 * WIDTH_A + 2 * WIDTH_B + WIDTH_C
NA_ROWS = 8
NA_COLS = 16
CHUNK = 128
D_FF = -(-8 * D_MODEL // (3 * 256)) * 256
EPS = 1e-6
NEG_INF = -1e9

LANES = 128
VMEM_LIMIT_BYTES = 56 * 1024 * 1024

HEADS_PER_STEP = 4
KEY_SLAB = NA_ROWS * GRID_W
ATTN_GROUP = 4
GATE_UNROLL = 4
LOG2_E = math.log2(math.e)
Q_SCALE = HEAD_DIM ** -0.5 * LOG2_E

IN_TM = 1024
FFN_TM = 1024
N_CHUNK = 256


def _compiler_params(n_grid_dims):
    return pltpu.CompilerParams(
        dimension_semantics=("arbitrary",) * n_grid_dims,
        vmem_limit_bytes=VMEM_LIMIT_BYTES,
    )


def _resident():
    return pl.BlockSpec(memory_space=pltpu.VMEM)


def _layer_slab(stacked, layer, n_grid_dims):
    zeros = (0,) * (stacked.ndim - 1)
    index_map = {1: lambda i: (layer,) + zeros, 2: lambda i, j: (layer,) + zeros}[n_grid_dims]
    return pl.BlockSpec((None,) + stacked.shape[1:], index_map, pipeline_mode=pl.Buffered(1))


def _rms_norm_rows(x, g):
    ms = jnp.mean(x * x, axis=-1, keepdims=True)
    return x * lax.rsqrt(ms + EPS) * g


def _head_norm_lane_pairs(y, g):
    n_tiles = y.shape[-1] // LANES
    lane = lax.broadcasted_iota(jnp.int32, (1, LANES), 1)
    first = lane < HEAD_DIM
    outs = []
    for t in range(n_tiles):
        yt = y[:, t * LANES:(t + 1) * LANES]
        sq = yt * yt
        ss_a = jnp.sum(jnp.where(first, sq, 0.0), axis=-1, keepdims=True)
        ss_b = jnp.sum(jnp.where(first, 0.0, sq), axis=-1, keepdims=True)
        ms = jnp.where(first, ss_a, ss_b) * (1.0 / HEAD_DIM)
        outs.append(yt * lax.rsqrt(ms + EPS) * g[:, t * LANES:(t + 1) * LANES])
    return outs[0] if n_tiles == 1 else jnp.concatenate(outs, axis=-1)


def _head_group_ones(width):
    row = lax.broadcasted_iota(jnp.int32, (width, width), 0) // HEAD_DIM
    col = lax.broadcasted_iota(jnp.int32, (width, width), 1) // HEAD_DIM
    return (row == col).astype(BF16)


def _head_norm_mxu(y, group_ones, g):
    m = y.shape[0]
    sq = y * y
    hi = sq.astype(BF16)
    lo = (sq - hi.astype(F32)).astype(BF16)
    sums = jnp.dot(jnp.concatenate([hi, lo], axis=0), group_ones, preferred_element_type=F32)
    ms = (sums[:m] + sums[m:]) * (1.0 / HEAD_DIM)
    return y * lax.rsqrt(ms + EPS) * g


def _in_proj_kernel(x_ref, g_ref, w_ref, o_ref, h_ref):
    h_ref[...] = _rms_norm_rows(x_ref[...], g_ref[...]).astype(BF16)
    for c in range(IN_WIDTH // N_CHUNK):
        cols = slice(c * N_CHUNK, (c + 1) * N_CHUNK)
        acc = jnp.dot(h_ref[...], w_ref[:, cols], preferred_element_type=F32)
        if (c + 1) * N_CHUNK <= WIDTH_A:
            acc = acc * Q_SCALE
        o_ref[:, cols] = acc.astype(o_ref.dtype)


def _in_proj(x2d, g, w_stack, layer):
    n = x2d.shape[0]
    return pl.pallas_call(
        _in_proj_kernel,
        grid=(n // IN_TM,),
        in_specs=[
            pl.BlockSpec((IN_TM, D_MODEL), lambda i: (i, 0)),
            _resident(),
            _layer_slab(w_stack, layer, 1),
        ],
        out_specs=pl.BlockSpec((IN_TM, IN_WIDTH), lambda i: (i, 0)),
        out_shape=jax.ShapeDtypeStruct((n, IN_WIDTH), BF16),
        scratch_shapes=[pltpu.VMEM((IN_TM, D_MODEL), BF16)],
        compiler_params=_compiler_params(1),
        name="in_proj",
    )(x2d, g.reshape(1, D_MODEL), w_stack)


def _attn_kernel(q_ref, k_ref, v_ref, bias_ref, g_ref, o_ref, p_ref, l_ref, *, n_rows):
    width = HEADS_PER_STEP * HEAD_DIM
    head_of_lane = lax.broadcasted_iota(jnp.int32, (1, width), 1) // HEAD_DIM
    g = g_ref[...]
    n_groups = n_rows // ATTN_GROUP

    def rows_of(group):
        for j in range(ATTN_GROUP):
            r = group * ATTN_GROUP + j
            rs = jnp.clip(r - NA_ROWS // 2, 0, n_rows - NA_ROWS)
            yield j, r, rs, pl.multiple_of(r * GRID_W, GRID_W), pl.multiple_of(rs * GRID_W, GRID_W)

    def probs(group, slot):
        for j, r, rs, q0, k0 in rows_of(group):
            q = q_ref[pl.ds(q0, GRID_W), :]
            zero = jnp.zeros_like(q)
            qm = jnp.concatenate(
                [jnp.where(head_of_lane == h, q, zero) for h in range(HEADS_PER_STEP)], axis=0)
            s = lax.dot_general(qm, k_ref[pl.ds(k0, KEY_SLAB), :], (((1,), (1,)), ((), ())),
                                preferred_element_type=F32)
            a0 = NA_ROWS - 1 - (r - rs)
            s = s + jnp.concatenate(
                [jnp.concatenate([bias_ref[h, a0 + 2 * i] for i in range(NA_ROWS // 2)], axis=-1)
                 for h in range(HEADS_PER_STEP)], axis=0)
            m = jnp.max(s, axis=-1, keepdims=True)
            p = jnp.exp2(s - m)
            l = jnp.sum(p, axis=-1, keepdims=True)
            l_ref[slot, j] = jnp.broadcast_to(l, (l.shape[0], LANES))
            p_ref[slot, j] = p.astype(BF16)

    def outputs(group, slot):
        for j, _, _, q0, k0 in rows_of(group):
            o4 = jnp.dot(p_ref[slot, j], v_ref[pl.ds(k0, KEY_SLAB), :], preferred_element_type=F32)
            inv_l = 1.0 / l_ref[slot, j]
            o4 = o4 * jnp.concatenate([inv_l] * (width // LANES), axis=-1)
            o = o4[:GRID_W]
            for h in range(1, HEADS_PER_STEP):
                o = jnp.where(head_of_lane == h, o4[h * GRID_W:(h + 1) * GRID_W], o)
            o_ref[pl.ds(q0, GRID_W), :] = _head_norm_lane_pairs(o, g).astype(o_ref.dtype)

    stages = (probs, outputs)

    def step(t, parity, first_stage=0, last_stage=len(stages) - 1):
        for d in range(last_stage, first_stage - 1, -1):
            stages[d](t - d, (parity + d) % 2)

    n_fill = len(stages) - 1
    for t in range(n_fill):
        step(t, t % 2, last_stage=t)

    def two_steps(i, carry):
        t = n_fill + 2 * i
        step(t, n_fill % 2)
        step(t + 1, (n_fill + 1) % 2)
        return carry

    n_steady = n_groups - n_fill
    lax.fori_loop(0, n_steady // 2, two_steps, 0)
    for t in range(n_fill + 2 * (n_steady // 2), n_groups):
        step(t, t % 2)
    for t in range(n_groups, n_groups + n_fill):
        step(t, t % 2, first_stage=t - n_groups + 1)


def _attention(proj, bias_tab, g, layer, batch, seq):
    width = HEADS_PER_STEP * HEAD_DIM
    n_groups = WIDTH_A // width
    k_off = WIDTH_A // width
    v_off = 2 * WIDTH_A // width
    return pl.pallas_call(
        functools.partial(_attn_kernel, n_rows=seq // GRID_W),
        grid=(n_groups, batch),
        in_specs=[
            pl.BlockSpec((seq, width), lambda hg, b: (b, hg)),
            pl.BlockSpec((seq, width), lambda hg, b: (b, k_off + hg)),
            pl.BlockSpec((seq, width), lambda hg, b: (b, v_off + hg)),
            pl.BlockSpec((None, HEADS_PER_STEP) + bias_tab.shape[2:],
                         lambda hg, b: (layer, hg, 0, 0, 0)),
            pl.BlockSpec((1, width), lambda hg, b: (0, hg)),
        ],
        out_specs=pl.BlockSpec((seq, width), lambda hg, b: (b, hg)),
        out_shape=jax.ShapeDtypeStruct((batch * seq, WIDTH_A), BF16),
        scratch_shapes=[
            pltpu.VMEM((2, ATTN_GROUP, HEADS_PER_STEP * GRID_W, KEY_SLAB), BF16),
            pltpu.VMEM((2, ATTN_GROUP, HEADS_PER_STEP * GRID_W, LANES), F32),
        ],
        compiler_params=_compiler_params(2),
        name="nbr_attention",
    )(proj, proj, proj, bias_tab, g.reshape(1, D_MODEL)[:, :WIDTH_A])


def _attention_bias_table(rpb):
    qc = np.arange(GRID_W)[:, None]
    kc = np.arange(GRID_W)[None, :]
    win = np.clip(qc - NA_COLS // 2, 0, GRID_W - NA_COLS)
    col_ok = (kc >= win) & (kc < win + NA_COLS)
    pad = GRID_W - NA_COLS
    padded = jnp.pad(rpb.astype(F32), ((0, 0), (0, 0), (0, 0), (pad, pad)))
    tiles = jnp.stack([padded[..., GRID_W - 1 - c:2 * GRID_W - 1 - c] for c in range(GRID_W)], axis=-2)
    tiles = jnp.where(jnp.asarray(col_ok), tiles, NEG_INF) * LOG2_E
    return jnp.concatenate([tiles[:, :, :-1], tiles[:, :, 1:]], axis=-1)


def _gelu_tanh(x):
    c = math.sqrt(2.0 / math.pi)
    return x * (0.5 * (1.0 + jnp.tanh(c * (x + 0.044715 * (x * x * x)))))


def _gate_kernel(z_ref, lng_ref, lnb_ref, wcat_ref, bs_ref, g_ref, o_ref, *, n_chunks):
    lane = lax.broadcasted_iota(jnp.int32, (1, WIDTH_B), 1)
    head_of_lane = lane // HEAD_DIM

    def chunk(c, carry):
        t0 = pl.multiple_of(c * CHUNK, CHUNK)
        z = _gelu_tanh(z_ref[pl.ds(t0, CHUNK), :].astype(F32))
        u = z[:, :WIDTH_B]
        v = z[:, WIDTH_B:]
        mu = jnp.mean(v, axis=-1, keepdims=True)
        d = v - mu
        var = jnp.mean(d * d, axis=-1, keepdims=True)
        vn = (d * lax.rsqrt(var + EPS) * lng_ref[...] + lnb_ref[...]).astype(BF16)
        stacked = jnp.concatenate(
            [jnp.where(head_of_lane == h, vn, jnp.zeros_like(vn)) for h in range(N_HEADS_B)], axis=0)
        mixed = jnp.dot(wcat_ref[...], stacked, preferred_element_type=F32) + bs_ref[...]
        o_ref[pl.ds(t0, CHUNK), :] = _head_norm_lane_pairs(u * mixed, g_ref[...]).astype(o_ref.dtype)
        return carry

    lax.fori_loop(0, n_chunks, chunk, 0, unroll=GATE_UNROLL)


def _spatial_gate(proj, ln_g, ln_b, w_s, b_s, g, batch, seq):
    z_off = 3 * WIDTH_A // (2 * WIDTH_B)
    wcat = jnp.transpose(w_s, (1, 0, 2)).reshape(CHUNK, N_HEADS_B * CHUNK).astype(BF16)
    bs_full = jnp.repeat(b_s.T.astype(F32), HEAD_DIM, axis=1)
    g_b = g.reshape(1, D_MODEL)[:, WIDTH_A:WIDTH_A + WIDTH_B]
    return pl.pallas_call(
        functools.partial(_gate_kernel, n_chunks=seq // CHUNK),
        grid=(batch,),
        in_specs=[
            pl.BlockSpec((seq, 2 * WIDTH_B), lambda b: (b, z_off)),
            _resident(), _resident(), _resident(), _resident(), _resident(),
        ],
        out_specs=pl.BlockSpec((seq, WIDTH_B), lambda b: (b, 0)),
        out_shape=jax.ShapeDtypeStruct((batch * seq, WIDTH_B), BF16),
        compiler_params=_compiler_params(1),
        name="spatial_gate",
    )(proj, ln_g.reshape(1, WIDTH_B), ln_b.reshape(1, WIDTH_B), wcat, bs_full, g_b)


DFT_TAIL = 16


def _dft_tables(seq):
    def cos_sin(n):
        i = lax.broadcasted_iota(jnp.int32, (n, n), 0)
        j = lax.broadcasted_iota(jnp.int32, (n, n), 1)
        ang = ((i * j) % n).astype(F32) * (2.0 * math.pi / n)
        return jnp.cos(ang), jnp.sin(ang)

    def cos_sin_rows(row_step, n_rows):
        i = lax.broadcasted_iota(jnp.int32, (n_rows, seq), 0) * row_step
        j = lax.broadcasted_iota(jnp.int32, (n_rows, seq), 1)
        ang = ((i * j) % seq).astype(F32) * (2.0 * math.pi / seq)
        return jnp.cos(ang), jnp.sin(ang)

    n_rows = seq // 2 + DFT_TAIL
    n_outer = -(-n_rows // GRID_W)
    ca, sa = (t[:, None, :] for t in cos_sin_rows(GRID_W, n_outer))
    cb, sb = (t[None, :, :] for t in cos_sin_rows(1, GRID_W))
    c_seq = (ca * cb - sa * sb).reshape(n_outer * GRID_W, seq)[:n_rows].astype(BF16)
    s_seq = (sa * cb + ca * sb).reshape(n_outer * GRID_W, seq)[:n_rows].astype(BF16)
    c_ch, s_ch = cos_sin(HEAD_DIM)
    eye = jnp.eye(WIDTH_C // HEAD_DIM, dtype=F32)
    ch_tab = jnp.concatenate([jnp.kron(eye, c_ch), jnp.kron(eye, s_ch)], axis=1).astype(BF16)
    i = lax.broadcasted_iota(jnp.int32, (N_CHUNK, N_CHUNK + DFT_TAIL), 0)
    k = lax.broadcasted_iota(jnp.int32, (N_CHUNK, N_CHUNK + DFT_TAIL), 1)
    reversal = (k == N_CHUNK - i).astype(BF16)
    return c_seq, s_seq, ch_tab, reversal


def _fourier_kernel(x_ref, cos_ref, sin_ref, ch_tab_ref, rev_ref, g_ref, o_ref, hi_ref, *, seq):
    xcs = jnp.dot(x_ref[...], ch_tab_ref[...], preferred_element_type=F32)
    xc = xcs[:, :WIDTH_C].astype(BF16)
    xs = xcs[:, WIDTH_C:].astype(BF16)
    scale = 1.0 / math.sqrt(seq * HEAD_DIM)
    g = g_ref[...]
    n_chunks = seq // 2 // N_CHUNK
    for t in range(n_chunks):
        r0 = t * N_CHUNK
        m = N_CHUNK + (DFT_TAIL if t == n_chunks - 1 else 0)
        a = jnp.dot(cos_ref[r0:r0 + m, :], xc, preferred_element_type=F32)
        b = jnp.dot(sin_ref[r0:r0 + m, :], xs, preferred_element_type=F32)
        lo = (a[:N_CHUNK] - b[:N_CHUNK]) * scale
        o_ref[r0:r0 + N_CHUNK, :] = _head_norm_lane_pairs(lo, g).astype(o_ref.dtype)
        hi_ref[r0:r0 + m, :] = _head_norm_lane_pairs((a + b) * scale, g).astype(hi_ref.dtype)
    for u in range(n_chunks):
        src = hi_ref[u * N_CHUNK:(u + 1) * N_CHUNK + DFT_TAIL, :]
        blk = jnp.dot(rev_ref[...], src, preferred_element_type=F32)
        o_ref[seq - (u + 1) * N_CHUNK:seq - u * N_CHUNK, :] = blk.astype(o_ref.dtype)


def _fourier_mix(proj, tables, g, batch, seq):
    c_off = (3 * WIDTH_A + 2 * WIDTH_B) // WIDTH_C
    g_c = g.reshape(1, D_MODEL)[:, WIDTH_A + WIDTH_B:]
    return pl.pallas_call(
        functools.partial(_fourier_kernel, seq=seq),
        grid=(batch,),
        in_specs=[
            pl.BlockSpec((seq, WIDTH_C), lambda b: (b, c_off)),
            _resident(), _resident(), _resident(), _resident(), _resident(),
        ],
        out_specs=pl.BlockSpec((seq, WIDTH_C), lambda b: (b, 0)),
        out_shape=jax.ShapeDtypeStruct((batch * seq, WIDTH_C), BF16),
        scratch_shapes=[pltpu.VMEM((seq // 2 + DFT_TAIL, WIDTH_C), BF16)],
        compiler_params=_compiler_params(1),
        name="fourier_mix",
    )(proj, *tables, g_c)


def _ffn_kernel(ma_ref, mb_ref, mc_ref, x_ref, wo_ref, gf_ref, wg_ref, wu_ref, wd_ref, gfin_ref,
                o_ref, h_ref, act_ref, *, final):
    x1_ref = o_ref
    mix = jnp.concatenate([ma_ref[...], mb_ref[...], mc_ref[...]], axis=-1)
    for c in range(D_MODEL // N_CHUNK):
        cols = slice(c * N_CHUNK, (c + 1) * N_CHUNK)
        x1_ref[:, cols] = x_ref[:, cols] + jnp.dot(mix, wo_ref[:, cols], preferred_element_type=F32)
    h_ref[...] = _rms_norm_rows(x1_ref[...], gf_ref[...]).astype(BF16)
    for c in range(D_FF // N_CHUNK):
        cols = slice(c * N_CHUNK, (c + 1) * N_CHUNK)
        gate = jnp.dot(h_ref[...], wg_ref[:, cols], preferred_element_type=F32)
        up = jnp.dot(h_ref[...], wu_ref[:, cols], preferred_element_type=F32)
        act_ref[:, cols] = (gate * jax.nn.sigmoid(gate) * up).astype(BF16)
    for c in range(D_MODEL // N_CHUNK):
        cols = slice(c * N_CHUNK, (c + 1) * N_CHUNK)
        x1_ref[:, cols] = x1_ref[:, cols] + jnp.dot(act_ref[...], wd_ref[:, cols],
                                                    preferred_element_type=F32)
    if final:
        o_ref[...] = _rms_norm_rows(x1_ref[...], gfin_ref[...])


def _out_proj_ffn(mix_a, mix_b, mix_c, x2d, w_out, g_ffn, w_gate, w_up, w_down, g_final, layer,
                  final):
    n = x2d.shape[0]
    row = lambda width: pl.BlockSpec((FFN_TM, width), lambda i: (i, 0))
    slab = lambda w: _layer_slab(w, layer, 1)
    return pl.pallas_call(
        functools.partial(_ffn_kernel, final=final),
        grid=(n // FFN_TM,),
        in_specs=[row(WIDTH_A), row(WIDTH_B), row(WIDTH_C), row(D_MODEL),
                  slab(w_out), _resident(), slab(w_gate), slab(w_up), slab(w_down), _resident()],
        out_specs=row(D_MODEL),
        out_shape=jax.ShapeDtypeStruct((n, D_MODEL), F32),
        scratch_shapes=[
            pltpu.VMEM((FFN_TM, D_MODEL), BF16),
            pltpu.VMEM((FFN_TM, D_FF), BF16),
        ],
        compiler_params=_compiler_params(1),
        name="out_proj_ffn",
    )(mix_a, mix_b, mix_c, x2d, w_out, g_ffn.reshape(1, D_MODEL), w_gate, w_up, w_down,
      g_final.reshape(1, D_MODEL))


def kernel(x, norm_mix_g, w_in, rpb, gmlp_ln_g, gmlp_ln_b, w_spatial, b_spatial, head_norm_g, w_out,
           norm_ffn_g, w_gate, w_up, w_down, final_norm_g):
    batch, seq, d_model = x.shape
    assert d_model == D_MODEL and seq % CHUNK == 0 and seq // GRID_W >= NA_ROWS
    depth = w_in.shape[0]
    dft_tables = _dft_tables(seq)
    bias_tab = _attention_bias_table(rpb)
    w_in, w_out, w_gate, w_up, w_down = (w.astype(BF16) for w in (w_in, w_out, w_gate, w_up, w_down))
    x2d = x.reshape(batch * seq, D_MODEL)
    for l in range(depth):
        proj = _in_proj(x2d, norm_mix_g[l], w_in, l)
        mix_a = _attention(proj, bias_tab, head_norm_g[l], l, batch, seq)
        mix_b = _spatial_gate(proj, gmlp_ln_g[l], gmlp_ln_b[l], w_spatial[l], b_spatial[l],
                              head_norm_g[l], batch, seq)
        mix_c = _fourier_mix(proj, dft_tables, head_norm_g[l], batch, seq)
        x2d = _out_proj_ffn(mix_a, mix_b, mix_c, x2d, w_out, norm_ffn_g[l], w_gate, w_up, w_down,
                            final_norm_g, l, final=(l == depth - 1))
    return x2d.reshape(batch, seq, D_MODEL)
```

```python
import functools
import math

import numpy as np
import jax
import jax.numpy as jnp
from jax import lax
from jax.experimental import pallas as pl
from jax.experimental.pallas import tpu as pltpu

F32 = jnp.float32
BF16 = jnp.bfloat16

D_MODEL = 1024
GRID_W = 64
HEAD_DIM = 64
WIDTH_A = D_MODEL // 2
WIDTH_B = D_MODEL // 4
WIDTH_C = D_MODEL // 4
N_HEADS_A = WIDTH_A // HEAD_DIM
N_HEADS_B = WIDTH_B // HEAD_DIM
IN_WIDTH = 3 * WIDTH_A + 2 * WIDTH_B + WIDTH_C
NA_ROWS = 8
NA_COLS = 16
CHUNK = 128
D_FF = -(-8 * D_MODEL // (3 * 256)) * 256
EPS = 1e-6
NEG_INF = -1e9

LANES = 128
VMEM_LIMIT_BYTES = 56 * 1024 * 1024

HEADS_PER_STEP = 4
KEY_SLAB = NA_ROWS * GRID_W
ATTN_GROUP = 4
LOG2_E = math.log2(math.e)
Q_SCALE = HEAD_DIM ** -0.5 * LOG2_E

IN_TM = 1024
FFN_TM = 1024
N_CHUNK = 256
DFT_TAIL = 16


def _compiler_params(n_grid_dims):
    return pltpu.CompilerParams(
        dimension_semantics=("arbitrary",) * n_grid_dims,
        vmem_limit_bytes=VMEM_LIMIT_BYTES,
    )


def _resident():
    return pl.BlockSpec(memory_space=pltpu.VMEM)


def _layer_slab(stacked, layer, n_grid_dims):
    zeros = (0,) * (stacked.ndim - 1)
    index_map = {1: lambda i: (layer,) + zeros, 2: lambda i, j: (layer,) + zeros}[n_grid_dims]
    return pl.BlockSpec((None,) + stacked.shape[1:], index_map, pipeline_mode=pl.Buffered(1))


def _rms_norm_rows(x, g):
    ms = jnp.mean(x * x, axis=-1, keepdims=True)
    return x * lax.rsqrt(ms + EPS) * g


def _head_norm_lane_pairs(y, g):
    n_tiles = y.shape[-1] // LANES
    lane = lax.broadcasted_iota(jnp.int32, (1, LANES), 1)
    first = lane < HEAD_DIM
    outs = []
    for t in range(n_tiles):
        yt = y[:, t * LANES:(t + 1) * LANES]
        sq = yt * yt
        ss_a = jnp.sum(jnp.where(first, sq, 0.0), axis=-1, keepdims=True)
        ss_b = jnp.sum(jnp.where(first, 0.0, sq), axis=-1, keepdims=True)
        ms = jnp.where(first, ss_a, ss_b) * (1.0 / HEAD_DIM)
        outs.append(yt * lax.rsqrt(ms + EPS) * g[:, t * LANES:(t + 1) * LANES])
    return outs[0] if n_tiles == 1 else jnp.concatenate(outs, axis=-1)


def _in_proj_kernel(x_ref, g_ref, w_ref, o_ref, h_ref):
    h_ref[...] = _rms_norm_rows(x_ref[...], g_ref[...]).astype(BF16)
    for c in range(IN_WIDTH // N_CHUNK):
        cols = slice(c * N_CHUNK, (c + 1) * N_CHUNK)
        acc = jnp.dot(h_ref[...], w_ref[:, cols], preferred_element_type=F32)
        if (c + 1) * N_CHUNK <= WIDTH_A:
            acc = acc * Q_SCALE
        o_ref[:, cols] = acc.astype(o_ref.dtype)


def _in_proj(x2d, g, w_stack, layer):
    n = x2d.shape[0]
    return pl.pallas_call(
        _in_proj_kernel,
        grid=(n // IN_TM,),
        in_specs=[
            pl.BlockSpec((IN_TM, D_MODEL), lambda i: (i, 0)),
            _resident(),
            _layer_slab(w_stack, layer, 1),
        ],
        out_specs=pl.BlockSpec((IN_TM, IN_WIDTH), lambda i: (i, 0)),
        out_shape=jax.ShapeDtypeStruct((n, IN_WIDTH), BF16),
        scratch_shapes=[pltpu.VMEM((IN_TM, D_MODEL), BF16)],
        compiler_params=_compiler_params(1),
        name="in_proj",
    )(x2d, g.reshape(1, D_MODEL), w_stack)


def _attn_kernel(q_ref, k_ref, v_ref, bias_ref, g_ref, o_ref, p_ref, l_ref, *, n_rows):
    width = HEADS_PER_STEP * HEAD_DIM
    head_of_lane = lax.broadcasted_iota(jnp.int32, (1, width), 1) // HEAD_DIM
    g = g_ref[...]
    n_groups = n_rows // ATTN_GROUP

    def rows_of(group):
        for j in range(ATTN_GROUP):
            r = group * ATTN_GROUP + j
            rs = jnp.clip(r - NA_ROWS // 2, 0, n_rows - NA_ROWS)
            yield j, r, rs, pl.multiple_of(r * GRID_W, GRID_W), pl.multiple_of(rs * GRID_W, GRID_W)

    def probs(group, slot):
        for j, r, rs, q0, k0 in rows_of(group):
            q = q_ref[pl.ds(q0, GRID_W), :]
            zero = jnp.zeros_like(q)
            qm = jnp.concatenate(
                [jnp.where(head_of_lane == h, q, zero) for h in range(HEADS_PER_STEP)], axis=0)
            s = lax.dot_general(qm, k_ref[pl.ds(k0, KEY_SLAB), :], (((1,), (1,)), ((), ())),
                                preferred_element_type=F32)
            a0 = NA_ROWS - 1 - (r - rs)
            s = s + jnp.concatenate(
                [jnp.concatenate([bias_ref[h, a0 + 2 * i] for i in range(NA_ROWS // 2)], axis=-1)
                 for h in range(HEADS_PER_STEP)], axis=0)
            m = jnp.max(s, axis=-1, keepdims=True)
            p = jnp.exp2(s - m)
            l = jnp.sum(p, axis=-1, keepdims=True)
            l_ref[slot, j] = jnp.broadcast_to(l, (l.shape[0], LANES))
            p_ref[slot, j] = p.astype(BF16)

    def outputs(group, slot):
        for j, _, _, q0, k0 in rows_of(group):
            o4 = jnp.dot(p_ref[slot, j], v_ref[pl.ds(k0, KEY_SLAB), :], preferred_element_type=F32)
            inv_l = 1.0 / l_ref[slot, j]
            o4 = o4 * jnp.concatenate([inv_l] * (width // LANES), axis=-1)
            o = o4[:GRID_W]
            for h in range(1, HEADS_PER_STEP):
                o = jnp.where(head_of_lane == h, o4[h * GRID_W:(h + 1) * GRID_W], o)
            o_ref[pl.ds(q0, GRID_W), :] = _head_norm_lane_pairs(o, g).astype(o_ref.dtype)

    stages = (probs, outputs)

    def step(t, parity, first_stage=0, last_stage=len(stages) - 1):
        for d in range(last_stage, first_stage - 1, -1):
            stages[d](t - d, (parity + d) % 2)

    n_fill = len(stages) - 1
    for t in range(n_fill):
        step(t, t % 2, last_stage=t)

    def two_steps(i, carry):
        t = n_fill + 2 * i
        step(t, n_fill % 2)
        step(t + 1, (n_fill + 1) % 2)
        return carry

    n_steady = n_groups - n_fill
    lax.fori_loop(0, n_steady // 2, two_steps, 0)
    for t in range(n_fill + 2 * (n_steady // 2), n_groups):
        step(t, t % 2)
    for t in range(n_groups, n_groups + n_fill):
        step(t, t % 2, first_stage=t - n_groups + 1)


def _attention(proj, bias_tab, g, layer, batch, seq):
    width = HEADS_PER_STEP * HEAD_DIM
    n_groups = WIDTH_A // width
    k_off = WIDTH_A // width
    v_off = 2 * WIDTH_A // width
    return pl.pallas_call(
        functools.partial(_attn_kernel, n_rows=seq // GRID_W),
        grid=(n_groups, batch),
        in_specs=[
            pl.BlockSpec((seq, width), lambda hg, b: (b, hg)),
            pl.BlockSpec((seq, width), lambda hg, b: (b, k_off + hg)),
            pl.BlockSpec((seq, width), lambda hg, b: (b, v_off + hg)),
            pl.BlockSpec((None, HEADS_PER_STEP) + bias_tab.shape[2:],
                         lambda hg, b: (layer, hg, 0, 0, 0)),
            pl.BlockSpec((1, width), lambda hg, b: (0, hg)),
        ],
        out_specs=pl.BlockSpec((seq, width), lambda hg, b: (b, hg)),
        out_shape=jax.ShapeDtypeStruct((batch * seq, WIDTH_A), BF16),
        scratch_shapes=[
            pltpu.VMEM((2, ATTN_GROUP, HEADS_PER_STEP * GRID_W, KEY_SLAB), BF16),
            pltpu.VMEM((2, ATTN_GROUP, HEADS_PER_STEP * GRID_W, LANES), F32),
        ],
        compiler_params=_compiler_params(2),
        name="nbr_attention",
    )(proj, proj, proj, bias_tab, g.reshape(1, D_MODEL)[:, :WIDTH_A])


def _attention_bias_table(rpb):
    qc = np.arange(GRID_W)[:, None]
    kc = np.arange(GRID_W)[None, :]
    win = np.clip(qc - NA_COLS // 2, 0, GRID_W - NA_COLS)
    col_ok = (kc >= win) & (kc < win + NA_COLS)
    period = 2 * GRID_W - 1
    left = GRID_W - NA_COLS
    padded = jnp.pad(rpb.astype(F32), ((0, 0), (0, 0), (0, 0), (left, period + 1 - left - rpb.shape[-1])))
    flat = jnp.broadcast_to(padded[..., None, :], padded.shape[:-1] + (GRID_W, period + 1))
    flat = flat.reshape(padded.shape[:-1] + (GRID_W * (period + 1),))
    tiles = flat[..., GRID_W - 1:GRID_W - 1 + GRID_W * period]
    tiles = tiles.reshape(padded.shape[:-1] + (GRID_W, period))[..., :GRID_W]
    tiles = jnp.where(jnp.asarray(col_ok), tiles, NEG_INF) * LOG2_E
    return jnp.concatenate([tiles[:, :, :-1], tiles[:, :, 1:]], axis=-1)


def _gelu_tanh(x):
    c = math.sqrt(2.0 / math.pi)
    return x * (0.5 * (1.0 + jnp.tanh(c * (x + 0.044715 * (x * x * x)))))


def _gate_chunk(z_ref, lng_ref, lnb_ref, wcat_ref, bs_ref, g_ref, o_ref, t0):
    head_of_lane = lax.broadcasted_iota(jnp.int32, (1, WIDTH_B), 1) // HEAD_DIM
    z = _gelu_tanh(z_ref[pl.ds(t0, CHUNK), :].astype(F32))
    u = z[:, :WIDTH_B]
    v = z[:, WIDTH_B:]
    mu = jnp.mean(v, axis=-1, keepdims=True)
    d = v - mu
    var = jnp.mean(d * d, axis=-1, keepdims=True)
    vn = (d * lax.rsqrt(var + EPS) * lng_ref[...] + lnb_ref[...]).astype(BF16)
    stacked = jnp.concatenate(
        [jnp.where(head_of_lane == h, vn, jnp.zeros_like(vn)) for h in range(N_HEADS_B)], axis=0)
    mixed = jnp.dot(wcat_ref[...], stacked, preferred_element_type=F32) + bs_ref[...]
    o_ref[pl.ds(t0, CHUNK), :] = _head_norm_lane_pairs(u * mixed, g_ref[...]).astype(o_ref.dtype)


def _dft_tables(seq):
    def cos_sin(n):
        i = lax.broadcasted_iota(jnp.int32, (n, n), 0)
        j = lax.broadcasted_iota(jnp.int32, (n, n), 1)
        ang = ((i * j) % n).astype(F32) * (2.0 * math.pi / n)
        return jnp.cos(ang), jnp.sin(ang)

    def cos_sin_rows(row_step, n_rows):
        i = lax.broadcasted_iota(jnp.int32, (n_rows, seq), 0) * row_step
        j = lax.broadcasted_iota(jnp.int32, (n_rows, seq), 1)
        ang = ((i * j) % seq).astype(F32) * (2.0 * math.pi / seq)
        return jnp.cos(ang), jnp.sin(ang)

    n_rows = seq // 2 + DFT_TAIL
    n_outer = -(-n_rows // GRID_W)
    ca, sa = (t[:, None, :] for t in cos_sin_rows(GRID_W, n_outer))
    cb, sb = (t[None, :, :] for t in cos_sin_rows(1, GRID_W))
    c_seq = (ca * cb - sa * sb).reshape(n_outer * GRID_W, seq)[:n_rows].astype(BF16)
    s_seq = (sa * cb + ca * sb).reshape(n_outer * GRID_W, seq)[:n_rows].astype(BF16)
    c_ch, s_ch = cos_sin(HEAD_DIM)
    eye = jnp.eye(WIDTH_C // HEAD_DIM, dtype=F32)
    ch_tab = jnp.concatenate([jnp.kron(eye, c_ch), jnp.kron(eye, s_ch)], axis=1).astype(BF16)
    i = lax.broadcasted_iota(jnp.int32, (N_CHUNK, N_CHUNK + DFT_TAIL), 0)
    k = lax.broadcasted_iota(jnp.int32, (N_CHUNK, N_CHUNK + DFT_TAIL), 1)
    reversal = (k == N_CHUNK - i).astype(BF16)
    return c_seq, s_seq, ch_tab, reversal


def _gate_fourier_kernel(z_ref, x_ref, lng_ref, lnb_ref, wcat_ref, bs_ref, gb_ref,
                         cos_ref, sin_ref, ch_tab_ref, rev_ref, gc_ref,
                         ob_ref, oc_ref, hi_ref, *, seq):
    gate_args = (z_ref, lng_ref, lnb_ref, wcat_ref, bs_ref, gb_ref, ob_ref)
    n_gate = seq // CHUNK
    n_chunks = seq // 2 // N_CHUNK
    gate_done = 0

    def gate_until(n):
        nonlocal gate_done
        for c in range(gate_done, min(n, n_gate)):
            _gate_chunk(*gate_args, c * CHUNK)
        gate_done = max(gate_done, min(n, n_gate))

    xcs = jnp.dot(x_ref[...], ch_tab_ref[...], preferred_element_type=F32)
    xc = xcs[:, :WIDTH_C].astype(BF16)
    xs = xcs[:, WIDTH_C:].astype(BF16)
    scale = 1.0 / math.sqrt(seq * HEAD_DIM)
    g = gc_ref[...]
    per_step = -(-n_gate // (2 * n_chunks))
    for t in range(n_chunks):
        r0 = t * N_CHUNK
        m = N_CHUNK + (DFT_TAIL if t == n_chunks - 1 else 0)
        a = jnp.dot(cos_ref[r0:r0 + m, :], xc, preferred_element_type=F32)
        b = jnp.dot(sin_ref[r0:r0 + m, :], xs, preferred_element_type=F32)
        lo = (a[:N_CHUNK] - b[:N_CHUNK]) * scale
        oc_ref[r0:r0 + N_CHUNK, :] = _head_norm_lane_pairs(lo, g).astype(oc_ref.dtype)
        hi_ref[r0:r0 + m, :] = _head_norm_lane_pairs((a + b) * scale, g).astype(hi_ref.dtype)
        gate_until((t + 1) * per_step)
    for u in range(n_chunks):
        src = hi_ref[u * N_CHUNK:(u + 1) * N_CHUNK + DFT_TAIL, :]
        blk = jnp.dot(rev_ref[...], src, preferred_element_type=F32)
        oc_ref[seq - (u + 1) * N_CHUNK:seq - u * N_CHUNK, :] = blk.astype(oc_ref.dtype)
        gate_until((n_chunks + u + 1) * per_step)
    gate_until(n_gate)


def _gate_fourier(proj, ln_g, ln_b, w_s, b_s, tables, g, batch, seq):
    z_off = 3 * WIDTH_A // (2 * WIDTH_B)
    c_off = (3 * WIDTH_A + 2 * WIDTH_B) // WIDTH_C
    wcat = jnp.transpose(w_s, (1, 0, 2)).reshape(CHUNK, N_HEADS_B * CHUNK).astype(BF16)
    bs_full = jnp.repeat(b_s.T.astype(F32), HEAD_DIM, axis=1)
    g_row = g.reshape(1, D_MODEL)
    g_b = g_row[:, WIDTH_A:WIDTH_A + WIDTH_B]
    g_c = g_row[:, WIDTH_A + WIDTH_B:]
    n_resident = 5 + len(tables) + 1
    return pl.pallas_call(
        functools.partial(_gate_fourier_kernel, seq=seq),
        grid=(batch,),
        in_specs=[
            pl.BlockSpec((seq, 2 * WIDTH_B), lambda b: (b, z_off)),
            pl.BlockSpec((seq, WIDTH_C), lambda b: (b, c_off)),
        ] + [_resident()] * n_resident,
        out_specs=[
            pl.BlockSpec((seq, WIDTH_B), lambda b: (b, 0)),
            pl.BlockSpec((seq, WIDTH_C), lambda b: (b, 0)),
        ],
        out_shape=[
            jax.ShapeDtypeStruct((batch * seq, WIDTH_B), BF16),
            jax.ShapeDtypeStruct((batch * seq, WIDTH_C), BF16),
        ],
        scratch_shapes=[pltpu.VMEM((seq // 2 + DFT_TAIL, WIDTH_C), BF16)],
        compiler_params=_compiler_params(1),
        name="gate_fourier",
    )(proj, proj, ln_g.reshape(1, WIDTH_B), ln_b.reshape(1, WIDTH_B), wcat, bs_full, g_b,
      *tables, g_c)


def _ffn_kernel(ma_ref, mb_ref, mc_ref, x_ref, wo_ref, gf_ref, wg_ref, wu_ref, wd_ref, gfin_ref,
                o_ref, h_ref, act_ref, *, final):
    x1_ref = o_ref
    mix = jnp.concatenate([ma_ref[...], mb_ref[...], mc_ref[...]], axis=-1)
    for c in range(D_MODEL // N_CHUNK):
        cols = slice(c * N_CHUNK, (c + 1) * N_CHUNK)
        x1_ref[:, cols] = x_ref[:, cols] + jnp.dot(mix, wo_ref[:, cols], preferred_element_type=F32)
    h_ref[...] = _rms_norm_rows(x1_ref[...], gf_ref[...]).astype(BF16)
    for c in range(D_FF // N_CHUNK):
        cols = slice(c * N_CHUNK, (c + 1) * N_CHUNK)
        gate = jnp.dot(h_ref[...], wg_ref[:, cols], preferred_element_type=F32)
        up = jnp.dot(h_ref[...], wu_ref[:, cols], preferred_element_type=F32)
        act_ref[:, cols] = (gate * jax.nn.sigmoid(gate) * up).astype(BF16)
    for c in range(D_MODEL // N_CHUNK):
        cols = slice(c * N_CHUNK, (c + 1) * N_CHUNK)
        x1_ref[:, cols] = x1_ref[:, cols] + jnp.dot(act_ref[...], wd_ref[:, cols],
                                                    preferred_element_type=F32)
    if final:
        o_ref[...] = _rms_norm_rows(x1_ref[...], gfin_ref[...])


def _out_proj_ffn(mix_a, mix_b, mix_c, x2d, w_out, g_ffn, w_gate, w_up, w_down, g_final, layer,
                  final):
    n = x2d.shape[0]
    row = lambda width: pl.BlockSpec((FFN_TM, width), lambda i: (i, 0))
    slab = lambda w: _layer_slab(w, layer, 1)
    return pl.pallas_call(
        functools.partial(_ffn_kernel, final=final),
        grid=(n // FFN_TM,),
        in_specs=[row(WIDTH_A), row(WIDTH_B), row(WIDTH_C), row(D_MODEL),
                  slab(w_out), _resident(), slab(w_gate), slab(w_up), slab(w_down), _resident()],
        out_specs=row(D_MODEL),
        out_shape=jax.ShapeDtypeStruct((n, D_MODEL), F32),
        scratch_shapes=[
            pltpu.VMEM((FFN_TM, D_MODEL), BF16),
            pltpu.VMEM((FFN_TM, D_FF), BF16),
        ],
        compiler_params=_compiler_params(1),
        name="out_proj_ffn",
    )(mix_a, mix_b, mix_c, x2d, w_out, g_ffn.reshape(1, D_MODEL), w_gate, w_up, w_down,
      g_final.reshape(1, D_MODEL))


def kernel(x, norm_mix_g, w_in, rpb, gmlp_ln_g, gmlp_ln_b, w_spatial, b_spatial, head_norm_g, w_out,
           norm_ffn_g, w_gate, w_up, w_down, final_norm_g):
    batch, seq, d_model = x.shape
    assert d_model == D_MODEL and seq % (2 * N_CHUNK) == 0 and seq // GRID_W >= NA_ROWS
    depth = w_in.shape[0]
    dft_tables = _dft_tables(seq)
    bias_tab = _attention_bias_table(rpb)
    w_in, w_out, w_gate, w_up, w_down = (w.astype(BF16) for w in (w_in, w_out, w_gate, w_up, w_down))
    x2d = x.reshape(batch * seq, D_MODEL)
    for l in range(depth):
        proj = _in_proj(x2d, norm_mix_g[l], w_in, l)
        mix_a = _attention(proj, bias_tab, head_norm_g[l], l, batch, seq)
        mix_b, mix_c = _gate_fourier(proj, gmlp_ln_g[l], gmlp_ln_b[l], w_spatial[l], b_spatial[l],
                                     dft_tables, head_norm_g[l], batch, seq)
        x2d = _out_proj_ffn(mix_a, mix_b, mix_c, x2d, w_out, norm_ffn_g[l], w_gate, w_up, w_down,
                            final_norm_g, l, final=(l == depth - 1))
    return x2d.reshape(batch, seq, D_MODEL)
```

```python
import functools
import math

import numpy as np
import jax
import jax.numpy as jnp
from jax import lax
from jax.experimental import pallas as pl
from jax.experimental.pallas import tpu as pltpu

F32 = jnp.float32
BF16 = jnp.bfloat16

D_MODEL = 1024
GRID_W = 64
HEAD_DIM = 64
WIDTH_A = D_MODEL // 2
WIDTH_B = D_MODEL // 4
WIDTH_C = D_MODEL // 4
N_HEADS_A = WIDTH_A // HEAD_DIM
N_HEADS_B = WIDTH_B // HEAD_DIM
IN_WIDTH = 3 * WIDTH_A + 2 * WIDTH_B + WIDTH_C
NA_ROWS = 8
NA_COLS = 16
CHUNK = 128
D_FF = -(-8 * D_MODEL // (3 * 256)) * 256
EPS = 1e-6
NEG_INF = -1e9

LANES = 128
VMEM_LIMIT_BYTES = 56 * 1024 * 1024

HEADS_PER_STEP = 4
KEY_SLAB = NA_ROWS * GRID_W
ATTN_GROUP = 4
LOG2_E = math.log2(math.e)
Q_SCALE = HEAD_DIM ** -0.5 * LOG2_E

IN_TM = 1024
FFN_TM = 1024
N_CHUNK = 256
DFT_TAIL = 16


def _compiler_params(n_grid_dims):
    return pltpu.CompilerParams(
        dimension_semantics=("arbitrary",) * n_grid_dims,
        vmem_limit_bytes=VMEM_LIMIT_BYTES,
    )


def _resident():
    return pl.BlockSpec(memory_space=pltpu.VMEM)


def _layer_slab(stacked, layer, n_grid_dims):
    zeros = (0,) * (stacked.ndim - 1)
    index_map = {1: lambda i: (layer,) + zeros, 2: lambda i, j: (layer,) + zeros}[n_grid_dims]
    return pl.BlockSpec((None,) + stacked.shape[1:], index_map, pipeline_mode=pl.Buffered(1))


def _rms_norm_rows(x, g):
    ms = jnp.mean(x * x, axis=-1, keepdims=True)
    return x * lax.rsqrt(ms + EPS) * g


def _head_norm_lane_pairs(y, g):
    n_tiles = y.shape[-1] // LANES
    lane = lax.broadcasted_iota(jnp.int32, (1, LANES), 1)
    first = lane < HEAD_DIM
    outs = []
    for t in range(n_tiles):
        yt = y[:, t * LANES:(t + 1) * LANES]
        sq = yt * yt
        ss_a = jnp.sum(jnp.where(first, sq, 0.0), axis=-1, keepdims=True)
        ss_b = jnp.sum(jnp.where(first, 0.0, sq), axis=-1, keepdims=True)
        ss = jnp.where(first, ss_a, ss_b)
        gain = g[:, t * LANES:(t + 1) * LANES] * math.sqrt(HEAD_DIM)
        outs.append(yt * lax.rsqrt(ss + HEAD_DIM * EPS) * gain)
    return outs[0] if n_tiles == 1 else jnp.concatenate(outs, axis=-1)


def _in_proj_kernel(x_ref, g_ref, w_ref, o_ref, h_ref):
    h_ref[...] = _rms_norm_rows(x_ref[...], g_ref[...]).astype(BF16)
    for c in range(IN_WIDTH // N_CHUNK):
        cols = slice(c * N_CHUNK, (c + 1) * N_CHUNK)
        acc = jnp.dot(h_ref[...], w_ref[:, cols], preferred_element_type=F32)
        if (c + 1) * N_CHUNK <= WIDTH_A:
            acc = acc * Q_SCALE
        o_ref[:, cols] = acc.astype(o_ref.dtype)


def _in_proj(x2d, g, w_stack, layer):
    n = x2d.shape[0]
    return pl.pallas_call(
        _in_proj_kernel,
        grid=(n // IN_TM,),
        in_specs=[
            pl.BlockSpec((IN_TM, D_MODEL), lambda i: (i, 0)),
            _resident(),
            _layer_slab(w_stack, layer, 1),
        ],
        out_specs=pl.BlockSpec((IN_TM, IN_WIDTH), lambda i: (i, 0)),
        out_shape=jax.ShapeDtypeStruct((n, IN_WIDTH), BF16),
        scratch_shapes=[pltpu.VMEM((IN_TM, D_MODEL), BF16)],
        compiler_params=_compiler_params(1),
        name="in_proj",
    )(x2d, g.reshape(1, D_MODEL), w_stack)


def _attn_kernel(q_ref, k_ref, v_ref, bias_ref, g_ref, o_ref, p_ref, l_ref, *, n_rows):
    width = HEADS_PER_STEP * HEAD_DIM
    head_of_lane = lax.broadcasted_iota(jnp.int32, (1, width), 1) // HEAD_DIM
    g = g_ref[...]
    n_groups = n_rows // ATTN_GROUP

    def rows_of(group):
        for j in range(ATTN_GROUP):
            r = group * ATTN_GROUP + j
            rs = jnp.clip(r - NA_ROWS // 2, 0, n_rows - NA_ROWS)
            yield j, r, rs, pl.multiple_of(r * GRID_W, GRID_W), pl.multiple_of(rs * GRID_W, GRID_W)

    def probs(group, slot):
        for j, r, rs, q0, k0 in rows_of(group):
            q = q_ref[pl.ds(q0, GRID_W), :]
            zero = jnp.zeros_like(q)
            qm = jnp.concatenate(
                [jnp.where(head_of_lane == h, q, zero) for h in range(HEADS_PER_STEP)], axis=0)
            s = lax.dot_general(qm, k_ref[pl.ds(k0, KEY_SLAB), :], (((1,), (1,)), ((), ())),
                                preferred_element_type=F32)
            a0 = NA_ROWS - 1 - (r - rs)
            s = s + jnp.concatenate(
                [jnp.concatenate([bias_ref[h, a0 + 2 * i] for i in range(NA_ROWS // 2)], axis=-1)
                 for h in range(HEADS_PER_STEP)], axis=0)
            m = jnp.max(s, axis=-1, keepdims=True)
            p = jnp.exp2(s - m)
            l = jnp.sum(p, axis=-1, keepdims=True)
            l_ref[slot, j] = jnp.broadcast_to(l, (l.shape[0], LANES))
            p_ref[slot, j] = p.astype(BF16)

    def outputs(group, slot):
        for j, _, _, q0, k0 in rows_of(group):
            o4 = jnp.dot(p_ref[slot, j], v_ref[pl.ds(k0, KEY_SLAB), :], preferred_element_type=F32)
            inv_l = 1.0 / l_ref[slot, j]
            o4 = o4 * jnp.concatenate([inv_l] * (width // LANES), axis=-1)
            o = o4[:GRID_W]
            for h in range(1, HEADS_PER_STEP):
                o = jnp.where(head_of_lane == h, o4[h * GRID_W:(h + 1) * GRID_W], o)
            o_ref[pl.ds(q0, GRID_W), :] = _head_norm_lane_pairs(o, g).astype(o_ref.dtype)

    stages = (probs, outputs)

    def step(t, parity, first_stage=0, last_stage=len(stages) - 1):
        for d in range(last_stage, first_stage - 1, -1):
            stages[d](t - d, (parity + d) % 2)

    n_fill = len(stages) - 1
    for t in range(n_fill):
        step(t, t % 2, last_stage=t)

    def two_steps(i, carry):
        t = n_fill + 2 * i
        step(t, n_fill % 2)
        step(t + 1, (n_fill + 1) % 2)
        return carry

    n_steady = n_groups - n_fill
    lax.fori_loop(0, n_steady // 2, two_steps, 0)
    for t in range(n_fill + 2 * (n_steady // 2), n_groups):
        step(t, t % 2)
    for t in range(n_groups, n_groups + n_fill):
        step(t, t % 2, first_stage=t - n_groups + 1)


def _attention(proj, bias_tab, g, layer, batch, seq):
    width = HEADS_PER_STEP * HEAD_DIM
    n_groups = WIDTH_A // width
    k_off = WIDTH_A // width
    v_off = 2 * WIDTH_A // width
    return pl.pallas_call(
        functools.partial(_attn_kernel, n_rows=seq // GRID_W),
        grid=(n_groups, batch),
        in_specs=[
            pl.BlockSpec((seq, width), lambda hg, b: (b, hg)),
            pl.BlockSpec((seq, width), lambda hg, b: (b, k_off + hg)),
            pl.BlockSpec((seq, width), lambda hg, b: (b, v_off + hg)),
            pl.BlockSpec((None, HEADS_PER_STEP) + bias_tab.shape[2:],
                         lambda hg, b: (layer, hg, 0, 0, 0)),
            pl.BlockSpec((1, width), lambda hg, b: (0, hg)),
        ],
        out_specs=pl.BlockSpec((seq, width), lambda hg, b: (b, hg)),
        out_shape=jax.ShapeDtypeStruct((batch * seq, WIDTH_A), BF16),
        scratch_shapes=[
            pltpu.VMEM((2, ATTN_GROUP, HEADS_PER_STEP * GRID_W, KEY_SLAB), BF16),
            pltpu.VMEM((2, ATTN_GROUP, HEADS_PER_STEP * GRID_W, LANES), F32),
        ],
        compiler_params=_compiler_params(2),
        name="nbr_attention",
    )(proj, proj, proj, bias_tab, g.reshape(1, D_MODEL)[:, :WIDTH_A])


def _attention_bias_table(rpb):
    qc = np.arange(GRID_W)[:, None]
    kc = np.arange(GRID_W)[None, :]
    win = np.clip(qc - NA_COLS // 2, 0, GRID_W - NA_COLS)
    col_ok = (kc >= win) & (kc < win + NA_COLS)
    n_rel = rpb.shape[-1]
    onehot = (kc - qc + NA_COLS - 1)[None] == np.arange(n_rel)[:, None, None]
    tiles = jnp.einsum("lhab,bqk->lhaqk", rpb.astype(F32), jnp.asarray(onehot, F32),
                       precision=lax.Precision.HIGHEST)
    tiles = jnp.where(jnp.asarray(col_ok), tiles, NEG_INF) * LOG2_E
    return jnp.concatenate([tiles[:, :, :-1], tiles[:, :, 1:]], axis=-1)


def _gelu_tanh(x):
    c2 = -2.0 * math.sqrt(2.0 / math.pi) * LOG2_E
    return x / (1.0 + jnp.exp2(x * (c2 + (c2 * 0.044715) * (x * x))))


def _gate_chunk(z_ref, lng_ref, lnb_ref, wcat_ref, bs_ref, g_ref, o_ref, t0):
    head_of_lane = lax.broadcasted_iota(jnp.int32, (1, WIDTH_B), 1) // HEAD_DIM
    z = _gelu_tanh(z_ref[pl.ds(t0, CHUNK), :].astype(F32))
    u = z[:, :WIDTH_B]
    v = z[:, WIDTH_B:]
    mu = jnp.mean(v, axis=-1, keepdims=True)
    d = v - mu
    var = jnp.mean(d * d, axis=-1, keepdims=True)
    vn = (d * lax.rsqrt(var + EPS) * lng_ref[...] + lnb_ref[...]).astype(BF16)
    stacked = jnp.concatenate(
        [jnp.where(head_of_lane == h, vn, jnp.zeros_like(vn)) for h in range(N_HEADS_B)], axis=0)
    mixed = jnp.dot(wcat_ref[...], stacked, preferred_element_type=F32) + bs_ref[...]
    o_ref[pl.ds(t0, CHUNK), :] = _head_norm_lane_pairs(u * mixed, g_ref[...]).astype(o_ref.dtype)


def _dft_tables(seq):
    def cos_sin(n):
        i = lax.broadcasted_iota(jnp.int32, (n, n), 0)
        j = lax.broadcasted_iota(jnp.int32, (n, n), 1)
        ang = ((i * j) % n).astype(F32) * (2.0 * math.pi / n)
        return jnp.cos(ang), jnp.sin(ang)

    def cos_sin_rows(row_step, n_rows):
        i = lax.broadcasted_iota(jnp.int32, (n_rows, seq), 0) * row_step
        j = lax.broadcasted_iota(jnp.int32, (n_rows, seq), 1)
        ang = ((i * j) % seq).astype(F32) * (2.0 * math.pi / seq)
        return jnp.cos(ang), jnp.sin(ang)

    n_rows = seq // 2 + DFT_TAIL
    n_outer = -(-n_rows // GRID_W)
    ca, sa = (t[:, None, :] for t in cos_sin_rows(GRID_W, n_outer))
    cb, sb = (t[None, :, :] for t in cos_sin_rows(1, GRID_W))
    c_seq = (ca * cb - sa * sb).reshape(n_outer * GRID_W, seq)[:n_rows].astype(BF16)
    s_seq = (sa * cb + ca * sb).reshape(n_outer * GRID_W, seq)[:n_rows].astype(BF16)
    c_ch, s_ch = cos_sin(HEAD_DIM)
    eye = jnp.eye(WIDTH_C // HEAD_DIM, dtype=F32)
    ch_tab = jnp.concatenate([jnp.kron(eye, c_ch), jnp.kron(eye, s_ch)], axis=1).astype(BF16)
    i = lax.broadcasted_iota(jnp.int32, (N_CHUNK, N_CHUNK + DFT_TAIL), 0)
    k = lax.broadcasted_iota(jnp.int32, (N_CHUNK, N_CHUNK + DFT_TAIL), 1)
    reversal = (k == N_CHUNK - i).astype(BF16)
    return c_seq, s_seq, ch_tab, reversal


def _gate_fourier_kernel(z_ref, x_ref, lng_ref, lnb_ref, wcat_ref, bs_ref, gb_ref,
                         cos_ref, sin_ref, ch_tab_ref, rev_ref, gc_ref,
                         ob_ref, oc_ref, hi_ref, *, seq):
    gate_args = (z_ref, lng_ref, lnb_ref, wcat_ref, bs_ref, gb_ref, ob_ref)
    n_gate = seq // CHUNK
    n_chunks = seq // 2 // N_CHUNK
    gate_done = 0

    def gate_until(n):
        nonlocal gate_done
        for c in range(gate_done, min(n, n_gate)):
            _gate_chunk(*gate_args, c * CHUNK)
        gate_done = max(gate_done, min(n, n_gate))

    xcs = jnp.dot(x_ref[...], ch_tab_ref[...], preferred_element_type=F32)
    xc = xcs[:, :WIDTH_C].astype(BF16)
    xs = xcs[:, WIDTH_C:].astype(BF16)
    scale = 1.0 / math.sqrt(seq * HEAD_DIM)
    g = gc_ref[...]
    per_step = -(-n_gate // (2 * n_chunks))
    for t in range(n_chunks):
        r0 = t * N_CHUNK
        m = N_CHUNK + (DFT_TAIL if t == n_chunks - 1 else 0)
        a = jnp.dot(cos_ref[r0:r0 + m, :], xc, preferred_element_type=F32)
        b = jnp.dot(sin_ref[r0:r0 + m, :], xs, preferred_element_type=F32)
        lo = (a[:N_CHUNK] - b[:N_CHUNK]) * scale
        oc_ref[r0:r0 + N_CHUNK, :] = _head_norm_lane_pairs(lo, g).astype(oc_ref.dtype)
        hi_ref[r0:r0 + m, :] = _head_norm_lane_pairs((a + b) * scale, g).astype(hi_ref.dtype)
        gate_until((t + 1) * per_step)
    for u in range(n_chunks):
        src = hi_ref[u * N_CHUNK:(u + 1) * N_CHUNK + DFT_TAIL, :]
        blk = jnp.dot(rev_ref[...], src, preferred_element_type=F32)
        oc_ref[seq - (u + 1) * N_CHUNK:seq - u * N_CHUNK, :] = blk.astype(oc_ref.dtype)
        gate_until((n_chunks + u + 1) * per_step)
    gate_until(n_gate)


def _gate_fourier(proj, ln_g, ln_b, w_s, b_s, tables, g, batch, seq):
    z_off = 3 * WIDTH_A // (2 * WIDTH_B)
    c_off = (3 * WIDTH_A + 2 * WIDTH_B) // WIDTH_C
    wcat = jnp.transpose(w_s, (1, 0, 2)).reshape(CHUNK, N_HEADS_B * CHUNK).astype(BF16)
    bs_full = jnp.repeat(b_s.T.astype(F32), HEAD_DIM, axis=1)
    g_row = g.reshape(1, D_MODEL)
    g_b = g_row[:, WIDTH_A:WIDTH_A + WIDTH_B]
    g_c = g_row[:, WIDTH_A + WIDTH_B:]
    n_resident = 5 + len(tables) + 1
    return pl.pallas_call(
        functools.partial(_gate_fourier_kernel, seq=seq),
        grid=(batch,),
        in_specs=[
            pl.BlockSpec((seq, 2 * WIDTH_B), lambda b: (b, z_off)),
            pl.BlockSpec((seq, WIDTH_C), lambda b: (b, c_off)),
        ] + [_resident()] * n_resident,
        out_specs=[
            pl.BlockSpec((seq, WIDTH_B), lambda b: (b, 0)),
            pl.BlockSpec((seq, WIDTH_C), lambda b: (b, 0)),
        ],
        out_shape=[
            jax.ShapeDtypeStruct((batch * seq, WIDTH_B), BF16),
            jax.ShapeDtypeStruct((batch * seq, WIDTH_C), BF16),
        ],
        scratch_shapes=[pltpu.VMEM((seq // 2 + DFT_TAIL, WIDTH_C), BF16)],
        compiler_params=_compiler_params(1),
        name="gate_fourier",
    )(proj, proj, ln_g.reshape(1, WIDTH_B), ln_b.reshape(1, WIDTH_B), wcat, bs_full, g_b,
      *tables, g_c)


def _ffn_kernel(ma_ref, mb_ref, mc_ref, x_ref, wo_ref, gf_ref, wg_ref, wu_ref, wd_ref, gfin_ref,
                o_ref, h_ref, act_ref, *, final):
    x1_ref = o_ref
    mix = jnp.concatenate([ma_ref[...], mb_ref[...], mc_ref[...]], axis=-1)
    for c in range(D_MODEL // N_CHUNK):
        cols = slice(c * N_CHUNK, (c + 1) * N_CHUNK)
        x1_ref[:, cols] = x_ref[:, cols] + jnp.dot(mix, wo_ref[:, cols], preferred_element_type=F32)
    h_ref[...] = _rms_norm_rows(x1_ref[...], gf_ref[...]).astype(BF16)
    for c in range(D_FF // N_CHUNK):
        cols = slice(c * N_CHUNK, (c + 1) * N_CHUNK)
        gate = jnp.dot(h_ref[...], wg_ref[:, cols], preferred_element_type=F32)
        up = jnp.dot(h_ref[...], wu_ref[:, cols], preferred_element_type=F32)
        act_ref[:, cols] = (gate * jax.nn.sigmoid(gate) * up).astype(BF16)
    for c in range(D_MODEL // N_CHUNK):
        cols = slice(c * N_CHUNK, (c + 1) * N_CHUNK)
        x1_ref[:, cols] = x1_ref[:, cols] + jnp.dot(act_ref[...], wd_ref[:, cols],
                                                    preferred_element_type=F32)
    if final:
        o_ref[...] = _rms_norm_rows(x1_ref[...], gfin_ref[...])


def _out_proj_ffn(mix_a, mix_b, mix_c, x2d, w_out, g_ffn, w_gate, w_up, w_down, g_final, layer,
                  final):
    n = x2d.shape[0]
    row = lambda width: pl.BlockSpec((FFN_TM, width), lambda i: (i, 0))
    slab = lambda w: _layer_slab(w, layer, 1)
    return pl.pallas_call(
        functools.partial(_ffn_kernel, final=final),
        grid=(n // FFN_TM,),
        in_specs=[row(WIDTH_A), row(WIDTH_B), row(WIDTH_C), row(D_MODEL),
                  slab(w_out), _resident(), slab(w_gate), slab(w_up), slab(w_down), _resident()],
        out_specs=row(D_MODEL),
        out_shape=jax.ShapeDtypeStruct((n, D_MODEL), F32),
        scratch_shapes=[
            pltpu.VMEM((FFN_TM, D_MODEL), BF16),
            pltpu.VMEM((FFN_TM, D_FF), BF16),
        ],
        compiler_params=_compiler_params(1),
        name="out_proj_ffn",
    )(mix_a, mix_b, mix_c, x2d, w_out, g_ffn.reshape(1, D_MODEL), w_gate, w_up, w_down,
      g_final.reshape(1, D_MODEL))


def kernel(x, norm_mix_g, w_in, rpb, gmlp_ln_g, gmlp_ln_b, w_spatial, b_spatial, head_norm_g, w_out,
           norm_ffn_g, w_gate, w_up, w_down, final_norm_g):
    batch, seq, d_model = x.shape
    assert d_model == D_MODEL and seq % (2 * N_CHUNK) == 0 and seq // GRID_W >= NA_ROWS
    depth = w_in.shape[0]
    dft_tables = _dft_tables(seq)
    bias_tab = _attention_bias_table(rpb)
    w_in, w_out, w_gate, w_up, w_down = (w.astype(BF16) for w in (w_in, w_out, w_gate, w_up, w_down))
    x2d = x.reshape(batch * seq, D_MODEL)
    for l in range(depth):
        proj = _in_proj(x2d, norm_mix_g[l], w_in, l)
        mix_a = _attention(proj, bias_tab, head_norm_g[l], l, batch, seq)
        mix_b, mix_c = _gate_fourier(proj, gmlp_ln_g[l], gmlp_ln_b[l], w_spatial[l], b_spatial[l],
                                     dft_tables, head_norm_g[l], batch, seq)
        x2d = _out_proj_ffn(mix_a, mix_b, mix_c, x2d, w_out, norm_ffn_g[l], w_gate, w_up, w_down,
                            final_norm_g, l, final=(l == depth - 1))
    return x2d.reshape(batch, seq, D_MODEL)
```

```python
import functools
import math

import numpy as np
import jax
import jax.numpy as jnp
from jax import lax
from jax.experimental import pallas as pl
from jax.experimental.pallas import tpu as pltpu

F32 = jnp.float32
BF16 = jnp.bfloat16

D_MODEL = 1024
GRID_W = 64
HEAD_DIM = 64
WIDTH_A = D_MODEL // 2
WIDTH_B = D_MODEL // 4
WIDTH_C = D_MODEL // 4
N_HEADS_A = WIDTH_A // HEAD_DIM
N_HEADS_B = WIDTH_B // HEAD_DIM
IN_WIDTH = 3 * WIDTH_A + 2 * WIDTH_B + WIDTH_C
NA_ROWS = 8
NA_COLS = 16
CHUNK = 128
D_FF = -(-8 * D_MODEL // (3 * 256)) * 256
EPS = 1e-6
NEG_INF = -1e9

LANES = 128
VMEM_LIMIT_BYTES = 56 * 1024 * 1024

HEADS_PER_STEP = 4
KEY_SLAB = NA_ROWS * GRID_W
ATTN_GROUP = 4
LOG2_E = math.log2(math.e)
Q_SCALE = HEAD_DIM ** -0.5 * LOG2_E

IN_TM = 1024
IN_RB = 256
FFN_TM = 1024
N_CHUNK = 256
DFT_TAIL = 16


def _compiler_params(n_grid_dims):
    return pltpu.CompilerParams(
        dimension_semantics=("arbitrary",) * n_grid_dims,
        vmem_limit_bytes=VMEM_LIMIT_BYTES,
    )


def _resident():
    return pl.BlockSpec(memory_space=pltpu.VMEM)


def _layer_slab(stacked, layer, n_grid_dims):
    zeros = (0,) * (stacked.ndim - 1)
    index_map = {1: lambda i: (layer,) + zeros, 2: lambda i, j: (layer,) + zeros}[n_grid_dims]
    return pl.BlockSpec((None,) + stacked.shape[1:], index_map, pipeline_mode=pl.Buffered(1))


def _rms_norm_rows(x, g):
    ms = jnp.mean(x * x, axis=-1, keepdims=True)
    return x * lax.rsqrt(ms + EPS) * g


def _head_norm_lane_pairs(y, g):
    n_tiles = y.shape[-1] // LANES
    lane = lax.broadcasted_iota(jnp.int32, (1, LANES), 1)
    first = lane < HEAD_DIM
    outs = []
    for t in range(n_tiles):
        yt = y[:, t * LANES:(t + 1) * LANES]
        sq = yt * yt
        ss_a = jnp.sum(jnp.where(first, sq, 0.0), axis=-1, keepdims=True)
        ss_b = jnp.sum(jnp.where(first, 0.0, sq), axis=-1, keepdims=True)
        ss = jnp.where(first, ss_a, ss_b)
        gain = g[:, t * LANES:(t + 1) * LANES] * math.sqrt(HEAD_DIM)
        outs.append(yt * lax.rsqrt(ss + HEAD_DIM * EPS) * gain)
    return outs[0] if n_tiles == 1 else jnp.concatenate(outs, axis=-1)


def _in_proj_kernel(x_ref, g_ref, w_ref, lng_ref, lnb_ref, o_ref, h_ref):
    assert N_CHUNK == WIDTH_B
    u_chunk = 3 * WIDTH_A // N_CHUNK
    v_chunk = u_chunk + 1
    others = [c for c in range(IN_WIDTH // N_CHUNK) if c not in (u_chunk, v_chunk)]
    for rb in range(IN_TM // IN_RB):
        rows = slice(rb * IN_RB, (rb + 1) * IN_RB)
        h_ref[rows, :] = _rms_norm_rows(x_ref[rows, :], g_ref[...]).astype(BF16)
        for c in [v_chunk, u_chunk] + others:
            cols = slice(c * N_CHUNK, (c + 1) * N_CHUNK)
            acc = jnp.dot(h_ref[rows, :], w_ref[:, cols], preferred_element_type=F32)
            if (c + 1) * N_CHUNK <= WIDTH_A:
                acc = acc * Q_SCALE
            elif c in (u_chunk, v_chunk):
                acc = _gelu_tanh(acc)
                if c == v_chunk:
                    mu = jnp.mean(acc, axis=-1, keepdims=True)
                    d = acc - mu
                    var = jnp.mean(d * d, axis=-1, keepdims=True)
                    acc = d * lax.rsqrt(var + EPS) * lng_ref[...] + lnb_ref[...]
            o_ref[rows, cols] = acc.astype(o_ref.dtype)


def _in_proj(x2d, g, w_stack, ln_g, ln_b, layer):
    n = x2d.shape[0]
    return pl.pallas_call(
        _in_proj_kernel,
        grid=(n // IN_TM,),
        in_specs=[
            pl.BlockSpec((IN_TM, D_MODEL), lambda i: (i, 0)),
            _resident(),
            _layer_slab(w_stack, layer, 1),
            _resident(),
            _resident(),
        ],
        out_specs=pl.BlockSpec((IN_TM, IN_WIDTH), lambda i: (i, 0)),
        out_shape=jax.ShapeDtypeStruct((n, IN_WIDTH), BF16),
        scratch_shapes=[pltpu.VMEM((IN_TM, D_MODEL), BF16)],
        compiler_params=_compiler_params(1),
        name="in_proj",
    )(x2d, g.reshape(1, D_MODEL), w_stack, ln_g.reshape(1, WIDTH_B), ln_b.reshape(1, WIDTH_B))


def _attn_kernel(q_ref, k_ref, v_ref, bias_ref, g_ref, o_ref, p_ref, l_ref, *, n_rows):
    width = HEADS_PER_STEP * HEAD_DIM
    head_of_lane = lax.broadcasted_iota(jnp.int32, (1, width), 1) // HEAD_DIM
    g = g_ref[...]
    n_groups = n_rows // ATTN_GROUP

    def rows_of(group):
        for j in range(ATTN_GROUP):
            r = group * ATTN_GROUP + j
            rs = jnp.clip(r - NA_ROWS // 2, 0, n_rows - NA_ROWS)
            yield j, r, rs, pl.multiple_of(r * GRID_W, GRID_W), pl.multiple_of(rs * GRID_W, GRID_W)

    def probs(group, slot):
        for j, r, rs, q0, k0 in rows_of(group):
            q = q_ref[pl.ds(q0, GRID_W), :]
            zero = jnp.zeros_like(q)
            qm = jnp.concatenate(
                [jnp.where(head_of_lane == h, q, zero) for h in range(HEADS_PER_STEP)], axis=0)
            s = lax.dot_general(qm, k_ref[pl.ds(k0, KEY_SLAB), :], (((1,), (1,)), ((), ())),
                                preferred_element_type=F32)
            a0 = NA_ROWS - 1 - (r - rs)
            s = s + jnp.concatenate(
                [jnp.concatenate([bias_ref[h, a0 + 2 * i] for i in range(NA_ROWS // 2)], axis=-1)
                 for h in range(HEADS_PER_STEP)], axis=0)
            m = jnp.max(s, axis=-1, keepdims=True)
            p = jnp.exp2(s - m)
            l = jnp.sum(p, axis=-1, keepdims=True)
            l_ref[slot, j] = jnp.broadcast_to(l, (l.shape[0], LANES))
            p_ref[slot, j] = p.astype(BF16)

    def outputs(group, slot):
        for j, _, _, q0, k0 in rows_of(group):
            o4 = jnp.dot(p_ref[slot, j], v_ref[pl.ds(k0, KEY_SLAB), :], preferred_element_type=F32)
            inv_l = 1.0 / l_ref[slot, j]
            o4 = o4 * jnp.concatenate([inv_l] * (width // LANES), axis=-1)
            o = o4[:GRID_W]
            for h in range(1, HEADS_PER_STEP):
                o = jnp.where(head_of_lane == h, o4[h * GRID_W:(h + 1) * GRID_W], o)
            o_ref[pl.ds(q0, GRID_W), :] = _head_norm_lane_pairs(o, g).astype(o_ref.dtype)

    stages = (probs, outputs)

    def step(t, parity, first_stage=0, last_stage=len(stages) - 1):
        for d in range(last_stage, first_stage - 1, -1):
            stages[d](t - d, (parity + d) % 2)

    n_fill = len(stages) - 1
    for t in range(n_fill):
        step(t, t % 2, last_stage=t)

    def two_steps(i, carry):
        t = n_fill + 2 * i
        step(t, n_fill % 2)
        step(t + 1, (n_fill + 1) % 2)
        return carry

    n_steady = n_groups - n_fill
    lax.fori_loop(0, n_steady // 2, two_steps, 0)
    for t in range(n_fill + 2 * (n_steady // 2), n_groups):
        step(t, t % 2)
    for t in range(n_groups, n_groups + n_fill):
        step(t, t % 2, first_stage=t - n_groups + 1)


def _attention(proj, bias_tab, g, layer, batch, seq):
    width = HEADS_PER_STEP * HEAD_DIM
    n_groups = WIDTH_A // width
    k_off = WIDTH_A // width
    v_off = 2 * WIDTH_A // width
    return pl.pallas_call(
        functools.partial(_attn_kernel, n_rows=seq // GRID_W),
        grid=(n_groups, batch),
        in_specs=[
            pl.BlockSpec((seq, width), lambda hg, b: (b, hg)),
            pl.BlockSpec((seq, width), lambda hg, b: (b, k_off + hg)),
            pl.BlockSpec((seq, width), lambda hg, b: (b, v_off + hg)),
            pl.BlockSpec((None, HEADS_PER_STEP) + bias_tab.shape[2:],
                         lambda hg, b: (layer, hg, 0, 0, 0)),
            pl.BlockSpec((1, width), lambda hg, b: (0, hg)),
        ],
        out_specs=pl.BlockSpec((seq, width), lambda hg, b: (b, hg)),
        out_shape=jax.ShapeDtypeStruct((batch * seq, WIDTH_A), BF16),
        scratch_shapes=[
            pltpu.VMEM((2, ATTN_GROUP, HEADS_PER_STEP * GRID_W, KEY_SLAB), BF16),
            pltpu.VMEM((2, ATTN_GROUP, HEADS_PER_STEP * GRID_W, LANES), F32),
        ],
        compiler_params=_compiler_params(2),
        name="nbr_attention",
    )(proj, proj, proj, bias_tab, g.reshape(1, D_MODEL)[:, :WIDTH_A])


def _attention_bias_table(rpb):
    qc = np.arange(GRID_W)[:, None]
    kc = np.arange(GRID_W)[None, :]
    win = np.clip(qc - NA_COLS // 2, 0, GRID_W - NA_COLS)
    col_ok = (kc >= win) & (kc < win + NA_COLS)
    n_rel = rpb.shape[-1]
    onehot = (kc - qc + NA_COLS - 1)[None] == np.arange(n_rel)[:, None, None]
    tiles = jnp.einsum("lhab,bqk->lhaqk", rpb.astype(F32), jnp.asarray(onehot, F32),
                       precision=lax.Precision.HIGHEST)
    tiles = jnp.where(jnp.asarray(col_ok), tiles, NEG_INF) * LOG2_E
    return jnp.concatenate([tiles[:, :, :-1], tiles[:, :, 1:]], axis=-1)


def _gelu_tanh(x):
    c2 = -2.0 * math.sqrt(2.0 / math.pi) * LOG2_E
    return x / (1.0 + jnp.exp2(x * (c2 + (c2 * 0.044715) * (x * x))))


def _gate_chunk(z_ref, wcat_ref, bs_ref, g_ref, o_ref, t0):
    head_of_lane = lax.broadcasted_iota(jnp.int32, (1, WIDTH_B), 1) // HEAD_DIM
    u = z_ref[pl.ds(t0, CHUNK), :WIDTH_B].astype(F32)
    vn = z_ref[pl.ds(t0, CHUNK), WIDTH_B:]
    stacked = jnp.concatenate(
        [jnp.where(head_of_lane == h, vn, jnp.zeros_like(vn)) for h in range(N_HEADS_B)], axis=0)
    mixed = jnp.dot(wcat_ref[...], stacked, preferred_element_type=F32) + bs_ref[...]
    o_ref[pl.ds(t0, CHUNK), :] = _head_norm_lane_pairs(u * mixed, g_ref[...]).astype(o_ref.dtype)


def _dft_tables(seq):
    def cos_sin(n):
        i = lax.broadcasted_iota(jnp.int32, (n, n), 0)
        j = lax.broadcasted_iota(jnp.int32, (n, n), 1)
        ang = ((i * j) % n).astype(F32) * (2.0 * math.pi / n)
        return jnp.cos(ang), jnp.sin(ang)

    def cos_sin_rows(row_step, n_rows):
        i = lax.broadcasted_iota(jnp.int32, (n_rows, seq), 0) * row_step
        j = lax.broadcasted_iota(jnp.int32, (n_rows, seq), 1)
        ang = ((i * j) % seq).astype(F32) * (2.0 * math.pi / seq)
        return jnp.cos(ang), jnp.sin(ang)

    n_rows = seq // 2 + DFT_TAIL
    n_outer = -(-n_rows // GRID_W)
    ca, sa = (t[:, None, :] for t in cos_sin_rows(GRID_W, n_outer))
    cb, sb = (t[None, :, :] for t in cos_sin_rows(1, GRID_W))
    c_seq = (ca * cb - sa * sb).reshape(n_outer * GRID_W, seq)[:n_rows].astype(BF16)
    s_seq = (sa * cb + ca * sb).reshape(n_outer * GRID_W, seq)[:n_rows].astype(BF16)
    c_ch, s_ch = cos_sin(HEAD_DIM)
    eye = jnp.eye(WIDTH_C // HEAD_DIM, dtype=F32)
    ch_tab = jnp.concatenate([jnp.kron(eye, c_ch), jnp.kron(eye, s_ch)], axis=1).astype(BF16)
    i = lax.broadcasted_iota(jnp.int32, (N_CHUNK, N_CHUNK + DFT_TAIL), 0)
    k = lax.broadcasted_iota(jnp.int32, (N_CHUNK, N_CHUNK + DFT_TAIL), 1)
    reversal = (k == N_CHUNK - i).astype(BF16)
    return c_seq, s_seq, ch_tab, reversal


def _gate_fourier_kernel(z_ref, x_ref, wcat_ref, bs_ref, gb_ref,
                         cos_ref, sin_ref, ch_tab_ref, rev_ref, gc_ref,
                         ob_ref, oc_ref, hi_ref, *, seq):
    gate_args = (z_ref, wcat_ref, bs_ref, gb_ref, ob_ref)
    n_gate = seq // CHUNK
    n_chunks = seq // 2 // N_CHUNK
    gate_done = 0

    def gate_until(n):
        nonlocal gate_done
        for c in range(gate_done, min(n, n_gate)):
            _gate_chunk(*gate_args, c * CHUNK)
        gate_done = max(gate_done, min(n, n_gate))

    xcs = jnp.dot(x_ref[...], ch_tab_ref[...], preferred_element_type=F32)
    xc = xcs[:, :WIDTH_C].astype(BF16)
    xs = xcs[:, WIDTH_C:].astype(BF16)
    scale = 1.0 / math.sqrt(seq * HEAD_DIM)
    g = gc_ref[...]
    per_step = -(-n_gate // (2 * n_chunks))
    for t in range(n_chunks):
        r0 = t * N_CHUNK
        m = N_CHUNK + (DFT_TAIL if t == n_chunks - 1 else 0)
        a = jnp.dot(cos_ref[r0:r0 + m, :], xc, preferred_element_type=F32)
        b = jnp.dot(sin_ref[r0:r0 + m, :], xs, preferred_element_type=F32)
        lo = (a[:N_CHUNK] - b[:N_CHUNK]) * scale
        oc_ref[r0:r0 + N_CHUNK, :] = _head_norm_lane_pairs(lo, g).astype(oc_ref.dtype)
        hi_ref[r0:r0 + m, :] = _head_norm_lane_pairs((a + b) * scale, g).astype(hi_ref.dtype)
        gate_until((t + 1) * per_step)
    for u in range(n_chunks):
        src = hi_ref[u * N_CHUNK:(u + 1) * N_CHUNK + DFT_TAIL, :]
        blk = jnp.dot(rev_ref[...], src, preferred_element_type=F32)
        oc_ref[seq - (u + 1) * N_CHUNK:seq - u * N_CHUNK, :] = blk.astype(oc_ref.dtype)
        gate_until((n_chunks + u + 1) * per_step)
    gate_until(n_gate)


def _gate_fourier(proj, w_s, b_s, tables, g, batch, seq):
    z_off = 3 * WIDTH_A // (2 * WIDTH_B)
    c_off = (3 * WIDTH_A + 2 * WIDTH_B) // WIDTH_C
    wcat = jnp.transpose(w_s, (1, 0, 2)).reshape(CHUNK, N_HEADS_B * CHUNK).astype(BF16)
    bs_full = jnp.repeat(b_s.T.astype(F32), HEAD_DIM, axis=1)
    g_row = g.reshape(1, D_MODEL)
    g_b = g_row[:, WIDTH_A:WIDTH_A + WIDTH_B]
    g_c = g_row[:, WIDTH_A + WIDTH_B:]
    n_resident = 3 + len(tables) + 1
    return pl.pallas_call(
        functools.partial(_gate_fourier_kernel, seq=seq),
        grid=(batch,),
        in_specs=[
            pl.BlockSpec((seq, 2 * WIDTH_B), lambda b: (b, z_off)),
            pl.BlockSpec((seq, WIDTH_C), lambda b: (b, c_off)),
        ] + [_resident()] * n_resident,
        out_specs=[
            pl.BlockSpec((seq, WIDTH_B), lambda b: (b, 0)),
            pl.BlockSpec((seq, WIDTH_C), lambda b: (b, 0)),
        ],
        out_shape=[
            jax.ShapeDtypeStruct((batch * seq, WIDTH_B), BF16),
            jax.ShapeDtypeStruct((batch * seq, WIDTH_C), BF16),
        ],
        scratch_shapes=[pltpu.VMEM((seq // 2 + DFT_TAIL, WIDTH_C), BF16)],
        compiler_params=_compiler_params(1),
        name="gate_fourier",
    )(proj, proj, wcat, bs_full, g_b, *tables, g_c)


def _ffn_kernel(ma_ref, mb_ref, mc_ref, x_ref, wo_ref, gf_ref, wg_ref, wu_ref, wd_ref, gfin_ref,
                o_ref, h_ref, act_ref, *, final):
    x1_ref = o_ref
    mix = jnp.concatenate([ma_ref[...], mb_ref[...], mc_ref[...]], axis=-1)
    for c in range(D_MODEL // N_CHUNK):
        cols = slice(c * N_CHUNK, (c + 1) * N_CHUNK)
        x1_ref[:, cols] = x_ref[:, cols] + jnp.dot(mix, wo_ref[:, cols], preferred_element_type=F32)
    h_ref[...] = _rms_norm_rows(x1_ref[...], gf_ref[...]).astype(BF16)
    for c in range(D_FF // N_CHUNK):
        cols = slice(c * N_CHUNK, (c + 1) * N_CHUNK)
        gate = jnp.dot(h_ref[...], wg_ref[:, cols], preferred_element_type=F32)
        up = jnp.dot(h_ref[...], wu_ref[:, cols], preferred_element_type=F32)
        act_ref[:, cols] = (gate * jax.nn.sigmoid(gate) * up).astype(BF16)
    for c in range(D_MODEL // N_CHUNK):
        cols = slice(c * N_CHUNK, (c + 1) * N_CHUNK)
        x1_ref[:, cols] = x1_ref[:, cols] + jnp.dot(act_ref[...], wd_ref[:, cols],
                                                    preferred_element_type=F32)
    if final:
        o_ref[...] = _rms_norm_rows(x1_ref[...], gfin_ref[...])


def _out_proj_ffn(mix_a, mix_b, mix_c, x2d, w_out, g_ffn, w_gate, w_up, w_down, g_final, layer,
                  final):
    n = x2d.shape[0]
    row = lambda width: pl.BlockSpec((FFN_TM, width), lambda i: (i, 0))
    slab = lambda w: _layer_slab(w, layer, 1)
    return pl.pallas_call(
        functools.partial(_ffn_kernel, final=final),
        grid=(n // FFN_TM,),
        in_specs=[row(WIDTH_A), row(WIDTH_B), row(WIDTH_C), row(D_MODEL),
                  slab(w_out), _resident(), slab(w_gate), slab(w_up), slab(w_down), _resident()],
        out_specs=row(D_MODEL),
        out_shape=jax.ShapeDtypeStruct((n, D_MODEL), F32),
        scratch_shapes=[
            pltpu.VMEM((FFN_TM, D_MODEL), BF16),
            pltpu.VMEM((FFN_TM, D_FF), BF16),
        ],
        compiler_params=_compiler_params(1),
        name="out_proj_ffn",
    )(mix_a, mix_b, mix_c, x2d, w_out, g_ffn.reshape(1, D_MODEL), w_gate, w_up, w_down,
      g_final.reshape(1, D_MODEL))


def kernel(x, norm_mix_g, w_in, rpb, gmlp_ln_g, gmlp_ln_b, w_spatial, b_spatial, head_norm_g, w_out,
           norm_ffn_g, w_gate, w_up, w_down, final_norm_g):
    batch, seq, d_model = x.shape
    assert d_model == D_MODEL and seq % (2 * N_CHUNK) == 0 and seq // GRID_W >= NA_ROWS
    depth = w_in.shape[0]
    dft_tables = _dft_tables(seq)
    bias_tab = _attention_bias_table(rpb)
    w_in, w_out, w_gate, w_up, w_down = (w.astype(BF16) for w in (w_in, w_out, w_gate, w_up, w_down))
    x2d = x.reshape(batch * seq, D_MODEL)
    for l in range(depth):
        proj = _in_proj(x2d, norm_mix_g[l], w_in, gmlp_ln_g[l], gmlp_ln_b[l], l)
        mix_a = _attention(proj, bias_tab, head_norm_g[l], l, batch, seq)
        mix_b, mix_c = _gate_fourier(proj, w_spatial[l], b_spatial[l], dft_tables, head_norm_g[l],
                                     batch, seq)
        x2d = _out_proj_ffn(mix_a, mix_b, mix_c, x2d, w_out, norm_ffn_g[l], w_gate, w_up, w_down,
                            final_norm_g, l, final=(l == depth - 1))
    return x2d.reshape(batch, seq, D_MODEL)
```

```python
import functools
import math

import numpy as np
import jax
import jax.numpy as jnp
from jax import lax
from jax.experimental import pallas as pl
from jax.experimental.pallas import tpu as pltpu

F32 = jnp.float32
BF16 = jnp.bfloat16

D_MODEL = 1024
GRID_W = 64
HEAD_DIM = 64
WIDTH_A = D_MODEL // 2
WIDTH_B = D_MODEL // 4
WIDTH_C = D_MODEL // 4
N_HEADS_A = WIDTH_A // HEAD_DIM
N_HEADS_B = WIDTH_B // HEAD_DIM
IN_WIDTH = 3 * WIDTH_A + 2 * WIDTH_B + WIDTH_C
NA_ROWS = 8
NA_COLS = 16
CHUNK = 128
D_FF = -(-8 * D_MODEL // (3 * 256)) * 256
EPS = 1e-6
NEG_INF = -1e9

LANES = 128
VMEM_LIMIT_BYTES = 56 * 1024 * 1024

HEADS_PER_STEP = 4
KEY_SLAB = NA_ROWS * GRID_W
ATTN_GROUP = 4
LOG2_E = math.log2(math.e)
Q_SCALE = HEAD_DIM ** -0.5 * LOG2_E

IN_TM = 1024
IN_RB = 512
FFN_TM = 1024
N_CHUNK = 256
DFT_TAIL = 16


def _compiler_params(n_grid_dims):
    return pltpu.CompilerParams(
        dimension_semantics=("arbitrary",) * n_grid_dims,
        vmem_limit_bytes=VMEM_LIMIT_BYTES,
    )


def _resident():
    return pl.BlockSpec(memory_space=pltpu.VMEM)


def _layer_slab(stacked, layer, n_grid_dims):
    zeros = (0,) * (stacked.ndim - 1)
    index_map = {1: lambda i: (layer,) + zeros, 2: lambda i, j: (layer,) + zeros}[n_grid_dims]
    return pl.BlockSpec((None,) + stacked.shape[1:], index_map, pipeline_mode=pl.Buffered(1))


def _rms_norm_rows(x, g):
    ms = jnp.mean(x * x, axis=-1, keepdims=True)
    return x * lax.rsqrt(ms + EPS) * g


def _head_norm_lane_pairs(y, g):
    n_tiles = y.shape[-1] // LANES
    lane = lax.broadcasted_iota(jnp.int32, (1, LANES), 1)
    first = lane < HEAD_DIM
    outs = []
    for t in range(n_tiles):
        yt = y[:, t * LANES:(t + 1) * LANES]
        sq = yt * yt
        ss_a = jnp.sum(jnp.where(first, sq, 0.0), axis=-1, keepdims=True)
        ss_b = jnp.sum(jnp.where(first, 0.0, sq), axis=-1, keepdims=True)
        ss = jnp.where(first, ss_a, ss_b)
        gain = g[:, t * LANES:(t + 1) * LANES] * math.sqrt(HEAD_DIM)
        outs.append(yt * lax.rsqrt(ss + HEAD_DIM * EPS) * gain)
    return outs[0] if n_tiles == 1 else jnp.concatenate(outs, axis=-1)


def _in_proj_kernel(x_ref, g_ref, w_ref, lng_ref, lnb_ref, o_ref, h_ref):
    assert N_CHUNK == WIDTH_B
    u_chunk = 3 * WIDTH_A // N_CHUNK
    v_chunk = u_chunk + 1
    others = [c for c in range(IN_WIDTH // N_CHUNK) if c not in (u_chunk, v_chunk)]
    for rb in range(IN_TM // IN_RB):
        rows = slice(rb * IN_RB, (rb + 1) * IN_RB)
        h_ref[rows, :] = _rms_norm_rows(x_ref[rows, :], g_ref[...]).astype(BF16)
        for c in [v_chunk, u_chunk] + others:
            cols = slice(c * N_CHUNK, (c + 1) * N_CHUNK)
            acc = jnp.dot(h_ref[rows, :], w_ref[:, cols], preferred_element_type=F32)
            if (c + 1) * N_CHUNK <= WIDTH_A:
                acc = acc * Q_SCALE
            elif c in (u_chunk, v_chunk):
                acc = _gelu_tanh(acc)
                if c == v_chunk:
                    mu = jnp.mean(acc, axis=-1, keepdims=True)
                    d = acc - mu
                    var = jnp.mean(d * d, axis=-1, keepdims=True)
                    acc = d * lax.rsqrt(var + EPS) * lng_ref[...] + lnb_ref[...]
            o_ref[rows, cols] = acc.astype(o_ref.dtype)


def _in_proj(x2d, g, w_stack, ln_g, ln_b, layer):
    n = x2d.shape[0]
    return pl.pallas_call(
        _in_proj_kernel,
        grid=(n // IN_TM,),
        in_specs=[
            pl.BlockSpec((IN_TM, D_MODEL), lambda i: (i, 0)),
            _resident(),
            _layer_slab(w_stack, layer, 1),
            _resident(),
            _resident(),
        ],
        out_specs=pl.BlockSpec((IN_TM, IN_WIDTH), lambda i: (i, 0)),
        out_shape=jax.ShapeDtypeStruct((n, IN_WIDTH), BF16),
        scratch_shapes=[pltpu.VMEM((IN_TM, D_MODEL), BF16)],
        compiler_params=_compiler_params(1),
        name="in_proj",
    )(x2d, g.reshape(1, D_MODEL), w_stack, ln_g.reshape(1, WIDTH_B), ln_b.reshape(1, WIDTH_B))


def _attn_kernel(q_ref, k_ref, v_ref, bias_ref, g_ref, o_ref, p_ref, l_ref, *, n_rows):
    width = HEADS_PER_STEP * HEAD_DIM
    head_of_lane = lax.broadcasted_iota(jnp.int32, (1, width), 1) // HEAD_DIM
    g = g_ref[...]
    n_groups = n_rows // ATTN_GROUP

    def rows_of(group):
        for j in range(ATTN_GROUP):
            r = group * ATTN_GROUP + j
            rs = jnp.clip(r - NA_ROWS // 2, 0, n_rows - NA_ROWS)
            yield j, r, rs, pl.multiple_of(r * GRID_W, GRID_W), pl.multiple_of(rs * GRID_W, GRID_W)

    def probs(group, slot):
        for j, r, rs, q0, k0 in rows_of(group):
            q = q_ref[pl.ds(q0, GRID_W), :]
            zero = jnp.zeros_like(q)
            qm = jnp.concatenate(
                [jnp.where(head_of_lane == h, q, zero) for h in range(HEADS_PER_STEP)], axis=0)
            s = lax.dot_general(qm, k_ref[pl.ds(k0, KEY_SLAB), :], (((1,), (1,)), ((), ())),
                                preferred_element_type=F32)
            a0 = NA_ROWS - 1 - (r - rs)
            s = s + jnp.concatenate(
                [jnp.concatenate([bias_ref[h, a0 + 2 * i] for i in range(NA_ROWS // 2)], axis=-1)
                 for h in range(HEADS_PER_STEP)], axis=0)
            m = jnp.max(s, axis=-1, keepdims=True)
            p = jnp.exp2(s - m)
            l = jnp.sum(p, axis=-1, keepdims=True)
            l_ref[slot, j] = jnp.broadcast_to(l, (l.shape[0], LANES))
            p_ref[slot, j] = p.astype(BF16)

    def outputs(group, slot):
        for j, _, _, q0, k0 in rows_of(group):
            o4 = jnp.dot(p_ref[slot, j], v_ref[pl.ds(k0, KEY_SLAB), :], preferred_element_type=F32)
            inv_l = 1.0 / l_ref[slot, j]
            o4 = o4 * jnp.concatenate([inv_l] * (width // LANES), axis=-1)
            o = o4[:GRID_W]
            for h in range(1, HEADS_PER_STEP):
                o = jnp.where(head_of_lane == h, o4[h * GRID_W:(h + 1) * GRID_W], o)
            o_ref[pl.ds(q0, GRID_W), :] = _head_norm_lane_pairs(o, g).astype(o_ref.dtype)

    stages = (probs, outputs)

    def step(t, parity, first_stage=0, last_stage=len(stages) - 1):
        for d in range(last_stage, first_stage - 1, -1):
            stages[d](t - d, (parity + d) % 2)

    n_fill = len(stages) - 1
    for t in range(n_fill):
        step(t, t % 2, last_stage=t)

    def two_steps(i, carry):
        t = n_fill + 2 * i
        step(t, n_fill % 2)
        step(t + 1, (n_fill + 1) % 2)
        return carry

    n_steady = n_groups - n_fill
    lax.fori_loop(0, n_steady // 2, two_steps, 0)
    for t in range(n_fill + 2 * (n_steady // 2), n_groups):
        step(t, t % 2)
    for t in range(n_groups, n_groups + n_fill):
        step(t, t % 2, first_stage=t - n_groups + 1)


def _attention(proj, bias_tab, g, layer, batch, seq):
    width = HEADS_PER_STEP * HEAD_DIM
    n_groups = WIDTH_A // width
    k_off = WIDTH_A // width
    v_off = 2 * WIDTH_A // width
    return pl.pallas_call(
        functools.partial(_attn_kernel, n_rows=seq // GRID_W),
        grid=(n_groups, batch),
        in_specs=[
            pl.BlockSpec((seq, width), lambda hg, b: (b, hg)),
            pl.BlockSpec((seq, width), lambda hg, b: (b, k_off + hg)),
            pl.BlockSpec((seq, width), lambda hg, b: (b, v_off + hg)),
            pl.BlockSpec((None, HEADS_PER_STEP) + bias_tab.shape[2:],
                         lambda hg, b: (layer, hg, 0, 0, 0)),
            pl.BlockSpec((1, width), lambda hg, b: (0, hg)),
        ],
        out_specs=pl.BlockSpec((seq, width), lambda hg, b: (b, hg)),
        out_shape=jax.ShapeDtypeStruct((batch * seq, WIDTH_A), BF16),
        scratch_shapes=[
            pltpu.VMEM((2, ATTN_GROUP, HEADS_PER_STEP * GRID_W, KEY_SLAB), BF16),
            pltpu.VMEM((2, ATTN_GROUP, HEADS_PER_STEP * GRID_W, LANES), F32),
        ],
        compiler_params=_compiler_params(2),
        name="nbr_attention",
    )(proj, proj, proj, bias_tab, g.reshape(1, D_MODEL)[:, :WIDTH_A])


def _attention_bias_table(rpb):
    qc = np.arange(GRID_W)[:, None]
    kc = np.arange(GRID_W)[None, :]
    win = np.clip(qc - NA_COLS // 2, 0, GRID_W - NA_COLS)
    col_ok = (kc >= win) & (kc < win + NA_COLS)
    n_rel = rpb.shape[-1]
    onehot = (kc - qc + NA_COLS - 1)[None] == np.arange(n_rel)[:, None, None]
    zeros = np.zeros_like(onehot)
    onehot2 = np.concatenate([np.concatenate([onehot, zeros], axis=-1),
                              np.concatenate([zeros, onehot], axis=-1)], axis=0)
    rows2 = jnp.concatenate([rpb[:, :, :-1], rpb[:, :, 1:]], axis=-1).astype(F32)
    tiles = jnp.einsum("lhab,bqk->lhaqk", rows2, jnp.asarray(onehot2, F32),
                       precision=lax.Precision.HIGHEST)
    return jnp.where(jnp.asarray(np.tile(col_ok, (1, 2))), tiles, NEG_INF) * LOG2_E


def _gelu_tanh(x):
    c2 = -2.0 * math.sqrt(2.0 / math.pi) * LOG2_E
    return x / (1.0 + jnp.exp2(x * (c2 + (c2 * 0.044715) * (x * x))))


def _gate_chunk(z_ref, wcat_ref, bs_ref, g_ref, o_ref, t0):
    head_of_lane = lax.broadcasted_iota(jnp.int32, (1, WIDTH_B), 1) // HEAD_DIM
    u = z_ref[pl.ds(t0, CHUNK), :WIDTH_B].astype(F32)
    vn = z_ref[pl.ds(t0, CHUNK), WIDTH_B:]
    stacked = jnp.concatenate(
        [jnp.where(head_of_lane == h, vn, jnp.zeros_like(vn)) for h in range(N_HEADS_B)], axis=0)
    mixed = jnp.dot(wcat_ref[...], stacked, preferred_element_type=F32) + bs_ref[...]
    o_ref[pl.ds(t0, CHUNK), :] = _head_norm_lane_pairs(u * mixed, g_ref[...]).astype(o_ref.dtype)


def _dft_tables(seq):
    def cos_sin(n):
        i = lax.broadcasted_iota(jnp.int32, (n, n), 0)
        j = lax.broadcasted_iota(jnp.int32, (n, n), 1)
        ang = ((i * j) % n).astype(F32) * (2.0 * math.pi / n)
        return jnp.cos(ang), jnp.sin(ang)

    def cos_sin_rows(row_step, n_rows):
        i = lax.broadcasted_iota(jnp.int32, (n_rows, seq), 0) * row_step
        j = lax.broadcasted_iota(jnp.int32, (n_rows, seq), 1)
        ang = ((i * j) % seq).astype(F32) * (2.0 * math.pi / seq)
        return jnp.cos(ang), jnp.sin(ang)

    n_rows = seq // 2 + DFT_TAIL
    n_outer = -(-n_rows // GRID_W)
    ca, sa = (t[:, None, :] for t in cos_sin_rows(GRID_W, n_outer))
    cb, sb = (t[None, :, :] for t in cos_sin_rows(1, GRID_W))
    c_seq = (ca * cb - sa * sb).reshape(n_outer * GRID_W, seq)[:n_rows].astype(BF16)
    s_seq = (sa * cb + ca * sb).reshape(n_outer * GRID_W, seq)[:n_rows].astype(BF16)
    c_ch, s_ch = cos_sin(HEAD_DIM)
    eye = jnp.eye(WIDTH_C // HEAD_DIM, dtype=F32)
    ch_tab = jnp.concatenate([jnp.kron(eye, c_ch), jnp.kron(eye, s_ch)], axis=1).astype(BF16)
    i = lax.broadcasted_iota(jnp.int32, (N_CHUNK, N_CHUNK + DFT_TAIL), 0)
    k = lax.broadcasted_iota(jnp.int32, (N_CHUNK, N_CHUNK + DFT_TAIL), 1)
    reversal = (k == N_CHUNK - i).astype(BF16)
    return c_seq, s_seq, ch_tab, reversal


def _gate_fourier_kernel(z_ref, x_ref, wcat_ref, bs_ref, gb_ref,
                         cos_ref, sin_ref, ch_tab_ref, rev_ref, gc_ref,
                         ob_ref, oc_ref, hi_ref, *, seq):
    gate_args = (z_ref, wcat_ref, bs_ref, gb_ref, ob_ref)
    n_gate = seq // CHUNK
    n_chunks = seq // 2 // N_CHUNK
    gate_done = 0

    def gate_until(n):
        nonlocal gate_done
        for c in range(gate_done, min(n, n_gate)):
            _gate_chunk(*gate_args, c * CHUNK)
        gate_done = max(gate_done, min(n, n_gate))

    xcs = jnp.dot(x_ref[...], ch_tab_ref[...], preferred_element_type=F32)
    xc = xcs[:, :WIDTH_C].astype(BF16)
    xs = xcs[:, WIDTH_C:].astype(BF16)
    scale = 1.0 / math.sqrt(seq * HEAD_DIM)
    g = gc_ref[...]
    per_step = -(-n_gate // (2 * n_chunks))
    for t in range(n_chunks):
        r0 = t * N_CHUNK
        m = N_CHUNK + (DFT_TAIL if t == n_chunks - 1 else 0)
        a = jnp.dot(cos_ref[r0:r0 + m, :], xc, preferred_element_type=F32)
        b = jnp.dot(sin_ref[r0:r0 + m, :], xs, preferred_element_type=F32)
        lo = (a[:N_CHUNK] - b[:N_CHUNK]) * scale
        oc_ref[r0:r0 + N_CHUNK, :] = _head_norm_lane_pairs(lo, g).astype(oc_ref.dtype)
        hi_ref[r0:r0 + m, :] = _head_norm_lane_pairs((a + b) * scale, g).astype(hi_ref.dtype)
        gate_until((t + 1) * per_step)
    for u in range(n_chunks):
        src = hi_ref[u * N_CHUNK:(u + 1) * N_CHUNK + DFT_TAIL, :]
        blk = jnp.dot(rev_ref[...], src, preferred_element_type=F32)
        oc_ref[seq - (u + 1) * N_CHUNK:seq - u * N_CHUNK, :] = blk.astype(oc_ref.dtype)
        gate_until((n_chunks + u + 1) * per_step)
    gate_until(n_gate)


def _gate_fourier(proj, w_s, b_s, tables, g, batch, seq):
    z_off = 3 * WIDTH_A // (2 * WIDTH_B)
    c_off = (3 * WIDTH_A + 2 * WIDTH_B) // WIDTH_C
    wcat = jnp.transpose(w_s, (1, 0, 2)).reshape(CHUNK, N_HEADS_B * CHUNK).astype(BF16)
    bs_full = jnp.repeat(b_s.T.astype(F32), HEAD_DIM, axis=1)
    g_row = g.reshape(1, D_MODEL)
    g_b = g_row[:, WIDTH_A:WIDTH_A + WIDTH_B]
    g_c = g_row[:, WIDTH_A + WIDTH_B:]
    n_resident = 3 + len(tables) + 1
    return pl.pallas_call(
        functools.partial(_gate_fourier_kernel, seq=seq),
        grid=(batch,),
        in_specs=[
            pl.BlockSpec((seq, 2 * WIDTH_B), lambda b: (b, z_off)),
            pl.BlockSpec((seq, WIDTH_C), lambda b: (b, c_off)),
        ] + [_resident()] * n_resident,
        out_specs=[
            pl.BlockSpec((seq, WIDTH_B), lambda b: (b, 0)),
            pl.BlockSpec((seq, WIDTH_C), lambda b: (b, 0)),
        ],
        out_shape=[
            jax.ShapeDtypeStruct((batch * seq, WIDTH_B), BF16),
            jax.ShapeDtypeStruct((batch * seq, WIDTH_C), BF16),
        ],
        scratch_shapes=[pltpu.VMEM((seq // 2 + DFT_TAIL, WIDTH_C), BF16)],
        compiler_params=_compiler_params(1),
        name="gate_fourier",
    )(proj, proj, wcat, bs_full, g_b, *tables, g_c)


def _ffn_kernel(ma_ref, mb_ref, mc_ref, x_ref, wo_ref, gf_ref, wg_ref, wu_ref, wd_ref, gfin_ref,
                o_ref, h_ref, act_ref, *, final):
    x1_ref = o_ref
    mix = jnp.concatenate([ma_ref[...], mb_ref[...], mc_ref[...]], axis=-1)
    for c in range(D_MODEL // N_CHUNK):
        cols = slice(c * N_CHUNK, (c + 1) * N_CHUNK)
        x1_ref[:, cols] = x_ref[:, cols] + jnp.dot(mix, wo_ref[:, cols], preferred_element_type=F32)
    h_ref[...] = _rms_norm_rows(x1_ref[...], gf_ref[...]).astype(BF16)
    for c in range(D_FF // N_CHUNK):
        cols = slice(c * N_CHUNK, (c + 1) * N_CHUNK)
        gate = jnp.dot(h_ref[...], wg_ref[:, cols], preferred_element_type=F32)
        up = jnp.dot(h_ref[...], wu_ref[:, cols], preferred_element_type=F32)
        act_ref[:, cols] = (gate * jax.nn.sigmoid(gate) * up).astype(BF16)
    for c in range(D_MODEL // N_CHUNK):
        cols = slice(c * N_CHUNK, (c + 1) * N_CHUNK)
        x1_ref[:, cols] = x1_ref[:, cols] + jnp.dot(act_ref[...], wd_ref[:, cols],
                                                    preferred_element_type=F32)
    if final:
        o_ref[...] = _rms_norm_rows(x1_ref[...], gfin_ref[...])


def _out_proj_ffn(mix_a, mix_b, mix_c, x2d, w_out, g_ffn, w_gate, w_up, w_down, g_final, layer,
                  final):
    n = x2d.shape[0]
    row = lambda width: pl.BlockSpec((FFN_TM, width), lambda i: (i, 0))
    slab = lambda w: _layer_slab(w, layer, 1)
    return pl.pallas_call(
        functools.partial(_ffn_kernel, final=final),
        grid=(n // FFN_TM,),
        in_specs=[row(WIDTH_A), row(WIDTH_B), row(WIDTH_C), row(D_MODEL),
                  slab(w_out), _resident(), slab(w_gate), slab(w_up), slab(w_down), _resident()],
        out_specs=row(D_MODEL),
        out_shape=jax.ShapeDtypeStruct((n, D_MODEL), F32),
        scratch_shapes=[
            pltpu.VMEM((FFN_TM, D_MODEL), BF16),
            pltpu.VMEM((FFN_TM, D_FF), BF16),
        ],
        compiler_params=_compiler_params(1),
        name="out_proj_ffn",
    )(mix_a, mix_b, mix_c, x2d, w_out, g_ffn.reshape(1, D_MODEL), w_gate, w_up, w_down,
      g_final.reshape(1, D_MODEL))


def kernel(x, norm_mix_g, w_in, rpb, gmlp_ln_g, gmlp_ln_b, w_spatial, b_spatial, head_norm_g, w_out,
           norm_ffn_g, w_gate, w_up, w_down, final_norm_g):
    batch, seq, d_model = x.shape
    assert d_model == D_MODEL and seq % (2 * N_CHUNK) == 0 and seq // GRID_W >= NA_ROWS
    depth = w_in.shape[0]
    dft_tables = _dft_tables(seq)
    bias_tab = _attention_bias_table(rpb)
    w_in, w_out, w_gate, w_up, w_down = (w.astype(BF16) for w in (w_in, w_out, w_gate, w_up, w_down))
    x2d = x.reshape(batch * seq, D_MODEL)
    for l in range(depth):
        proj = _in_proj(x2d, norm_mix_g[l], w_in, gmlp_ln_g[l], gmlp_ln_b[l], l)
        mix_a = _attention(proj, bias_tab, head_norm_g[l], l, batch, seq)
        mix_b, mix_c = _gate_fourier(proj, w_spatial[l], b_spatial[l], dft_tables, head_norm_g[l],
                                     batch, seq)
        x2d = _out_proj_ffn(mix_a, mix_b, mix_c, x2d, w_out, norm_ffn_g[l], w_gate, w_up, w_down,
                            final_norm_g, l, final=(l == depth - 1))
    return x2d.reshape(batch, seq, D_MODEL)
```

```python
import functools
import math

import numpy as np
import jax
import jax.numpy as jnp
from jax import lax
from jax.experimental import pallas as pl
from jax.experimental.pallas import tpu as pltpu

F32 = jnp.float32
BF16 = jnp.bfloat16

D_MODEL = 1024
GRID_W = 64
HEAD_DIM = 64
WIDTH_A = D_MODEL // 2
WIDTH_B = D_MODEL // 4
WIDTH_C = D_MODEL // 4
N_HEADS_A = WIDTH_A // HEAD_DIM
N_HEADS_B = WIDTH_B // HEAD_DIM
IN_WIDTH = 3 * WIDTH_A + 2 * WIDTH_B + WIDTH_C
NA_ROWS = 8
NA_COLS = 16
CHUNK = 128
D_FF = -(-8 * D_MODEL // (3 * 256)) * 256
EPS = 1e-6
NEG_INF = -1e9

LANES = 128
VMEM_LIMIT_BYTES = 56 * 1024 * 1024

HEADS_PER_STEP = 4
KEY_SLAB = NA_ROWS * GRID_W
ATTN_GROUP = 4
LOG2_E = math.log2(math.e)
Q_SCALE = HEAD_DIM ** -0.5 * LOG2_E

IN_TM = 2048
IN_RB = 512
FFN_TM = 1024
N_CHUNK = 256
DFT_TAIL = 16


def _compiler_params(n_grid_dims):
    return pltpu.CompilerParams(
        dimension_semantics=("arbitrary",) * n_grid_dims,
        vmem_limit_bytes=VMEM_LIMIT_BYTES,
    )


def _resident():
    return pl.BlockSpec(memory_space=pltpu.VMEM)


def _layer_slab(stacked, layer, n_grid_dims):
    zeros = (0,) * (stacked.ndim - 1)
    index_map = {1: lambda i: (layer,) + zeros, 2: lambda i, j: (layer,) + zeros}[n_grid_dims]
    return pl.BlockSpec((None,) + stacked.shape[1:], index_map, pipeline_mode=pl.Buffered(1))


def _rms_norm_rows(x, g):
    ms = jnp.mean(x * x, axis=-1, keepdims=True)
    return x * lax.rsqrt(ms + EPS) * g


def _head_norm_lane_pairs(y, g):
    n_tiles = y.shape[-1] // LANES
    lane = lax.broadcasted_iota(jnp.int32, (1, LANES), 1)
    first = lane < HEAD_DIM
    outs = []
    for t in range(n_tiles):
        yt = y[:, t * LANES:(t + 1) * LANES]
        sq = yt * yt
        ss_a = jnp.sum(jnp.where(first, sq, 0.0), axis=-1, keepdims=True)
        ss_b = jnp.sum(jnp.where(first, 0.0, sq), axis=-1, keepdims=True)
        ss = jnp.where(first, ss_a, ss_b)
        gain = g[:, t * LANES:(t + 1) * LANES] * math.sqrt(HEAD_DIM)
        outs.append(yt * lax.rsqrt(ss + HEAD_DIM * EPS) * gain)
    return outs[0] if n_tiles == 1 else jnp.concatenate(outs, axis=-1)


def _in_proj_kernel(x_ref, g_ref, w_ref, lng_ref, lnb_ref, o_ref, h_ref):
    assert N_CHUNK == WIDTH_B
    u_chunk = 3 * WIDTH_A // N_CHUNK
    v_chunk = u_chunk + 1
    others = [c for c in range(IN_WIDTH // N_CHUNK) if c not in (u_chunk, v_chunk)]
    for rb in range(IN_TM // IN_RB):
        rows = slice(rb * IN_RB, (rb + 1) * IN_RB)
        h_ref[rows, :] = _rms_norm_rows(x_ref[rows, :], g_ref[...]).astype(BF16)
        for c in [v_chunk, u_chunk] + others:
            cols = slice(c * N_CHUNK, (c + 1) * N_CHUNK)
            acc = jnp.dot(h_ref[rows, :], w_ref[:, cols], preferred_element_type=F32)
            if (c + 1) * N_CHUNK <= WIDTH_A:
                acc = acc * Q_SCALE
            elif c in (u_chunk, v_chunk):
                acc = _gelu_tanh(acc)
                if c == v_chunk:
                    mu = jnp.mean(acc, axis=-1, keepdims=True)
                    d = acc - mu
                    var = jnp.mean(d * d, axis=-1, keepdims=True)
                    acc = d * lax.rsqrt(var + EPS) * lng_ref[...] + lnb_ref[...]
            o_ref[rows, cols] = acc.astype(o_ref.dtype)


def _in_proj(x2d, g, w_stack, ln_g, ln_b, layer):
    n = x2d.shape[0]
    return pl.pallas_call(
        _in_proj_kernel,
        grid=(n // IN_TM,),
        in_specs=[
            pl.BlockSpec((IN_TM, D_MODEL), lambda i: (i, 0)),
            _resident(),
            _layer_slab(w_stack, layer, 1),
            _resident(),
            _resident(),
        ],
        out_specs=pl.BlockSpec((IN_TM, IN_WIDTH), lambda i: (i, 0)),
        out_shape=jax.ShapeDtypeStruct((n, IN_WIDTH), BF16),
        scratch_shapes=[pltpu.VMEM((IN_TM, D_MODEL), BF16)],
        compiler_params=_compiler_params(1),
        name="in_proj",
    )(x2d, g.reshape(1, D_MODEL), w_stack, ln_g.reshape(1, WIDTH_B), ln_b.reshape(1, WIDTH_B))


def _attn_kernel(q_ref, k_ref, v_ref, bias_ref, g_ref, o_ref, p_ref, l_ref, *, n_rows):
    width = HEADS_PER_STEP * HEAD_DIM
    head_of_lane = lax.broadcasted_iota(jnp.int32, (1, width), 1) // HEAD_DIM
    g = g_ref[...]
    n_groups = n_rows // ATTN_GROUP

    def rows_of(group):
        for j in range(ATTN_GROUP):
            r = group * ATTN_GROUP + j
            rs = jnp.clip(r - NA_ROWS // 2, 0, n_rows - NA_ROWS)
            yield j, r, rs, pl.multiple_of(r * GRID_W, GRID_W), pl.multiple_of(rs * GRID_W, GRID_W)

    def probs(group, slot):
        for j, r, rs, q0, k0 in rows_of(group):
            q = q_ref[pl.ds(q0, GRID_W), :]
            zero = jnp.zeros_like(q)
            qm = jnp.concatenate(
                [jnp.where(head_of_lane == h, q, zero) for h in range(HEADS_PER_STEP)], axis=0)
            s = lax.dot_general(qm, k_ref[pl.ds(k0, KEY_SLAB), :], (((1,), (1,)), ((), ())),
                                preferred_element_type=F32)
            a0 = NA_ROWS - 1 - (r - rs)
            s = s + jnp.concatenate(
                [jnp.concatenate([bias_ref[h, a0 + 2 * i] for i in range(NA_ROWS // 2)], axis=-1)
                 for h in range(HEADS_PER_STEP)], axis=0)
            m = jnp.max(s, axis=-1, keepdims=True)
            p = jnp.exp2(s - m)
            l = jnp.sum(p, axis=-1, keepdims=True)
            l_ref[slot, j] = jnp.broadcast_to(l, (l.shape[0], LANES))
            p_ref[slot, j] = p.astype(BF16)

    def outputs(group, slot):
        for j, _, _, q0, k0 in rows_of(group):
            o4 = jnp.dot(p_ref[slot, j], v_ref[pl.ds(k0, KEY_SLAB), :], preferred_element_type=F32)
            inv_l = 1.0 / l_ref[slot, j]
            o4 = o4 * jnp.concatenate([inv_l] * (width // LANES), axis=-1)
            o = o4[:GRID_W]
            for h in range(1, HEADS_PER_STEP):
                o = jnp.where(head_of_lane == h, o4[h * GRID_W:(h + 1) * GRID_W], o)
            o_ref[pl.ds(q0, GRID_W), :] = _head_norm_lane_pairs(o, g).astype(o_ref.dtype)

    stages = (probs, outputs)

    def step(t, parity, first_stage=0, last_stage=len(stages) - 1):
        for d in range(last_stage, first_stage - 1, -1):
            stages[d](t - d, (parity + d) % 2)

    n_fill = len(stages) - 1
    for t in range(n_fill):
        step(t, t % 2, last_stage=t)

    def two_steps(i, carry):
        t = n_fill + 2 * i
        step(t, n_fill % 2)
        step(t + 1, (n_fill + 1) % 2)
        return carry

    n_steady = n_groups - n_fill
    lax.fori_loop(0, n_steady // 2, two_steps, 0)
    for t in range(n_fill + 2 * (n_steady // 2), n_groups):
        step(t, t % 2)
    for t in range(n_groups, n_groups + n_fill):
        step(t, t % 2, first_stage=t - n_groups + 1)


def _attention(proj, bias_tab, g, layer, batch, seq):
    width = HEADS_PER_STEP * HEAD_DIM
    n_groups = WIDTH_A // width
    k_off = WIDTH_A // width
    v_off = 2 * WIDTH_A // width
    return pl.pallas_call(
        functools.partial(_attn_kernel, n_rows=seq // GRID_W),
        grid=(n_groups, batch),
        in_specs=[
            pl.BlockSpec((seq, width), lambda hg, b: (b, hg)),
            pl.BlockSpec((seq, width), lambda hg, b: (b, k_off + hg)),
            pl.BlockSpec((seq, width), lambda hg, b: (b, v_off + hg)),
            pl.BlockSpec((None, HEADS_PER_STEP) + bias_tab.shape[2:],
                         lambda hg, b: (layer, hg, 0, 0, 0)),
            pl.BlockSpec((1, width), lambda hg, b: (0, hg)),
        ],
        out_specs=pl.BlockSpec((seq, width), lambda hg, b: (b, hg)),
        out_shape=jax.ShapeDtypeStruct((batch * seq, WIDTH_A), BF16),
        scratch_shapes=[
            pltpu.VMEM((2, ATTN_GROUP, HEADS_PER_STEP * GRID_W, KEY_SLAB), BF16),
            pltpu.VMEM((2, ATTN_GROUP, HEADS_PER_STEP * GRID_W, LANES), F32),
        ],
        compiler_params=_compiler_params(2),
        name="nbr_attention",
    )(proj, proj, proj, bias_tab, g.reshape(1, D_MODEL)[:, :WIDTH_A])


def _attention_bias_table(rpb):
    qc = np.arange(GRID_W)[:, None]
    kc = np.arange(GRID_W)[None, :]
    win = np.clip(qc - NA_COLS // 2, 0, GRID_W - NA_COLS)
    col_ok = (kc >= win) & (kc < win + NA_COLS)
    n_rel = rpb.shape[-1]
    onehot = (kc - qc + NA_COLS - 1)[None] == np.arange(n_rel)[:, None, None]
    zeros = np.zeros_like(onehot)
    onehot2 = np.concatenate([np.concatenate([onehot, zeros], axis=-1),
                              np.concatenate([zeros, onehot], axis=-1)], axis=0)
    rows2 = jnp.concatenate([rpb[:, :, :-1], rpb[:, :, 1:]], axis=-1).astype(F32)
    tiles = jnp.einsum("lhab,bqk->lhaqk", rows2, jnp.asarray(onehot2, F32),
                       precision=lax.Precision.HIGHEST)
    return jnp.where(jnp.asarray(np.tile(col_ok, (1, 2))), tiles, NEG_INF) * LOG2_E


def _gelu_tanh(x):
    c2 = -2.0 * math.sqrt(2.0 / math.pi) * LOG2_E
    return x / (1.0 + jnp.exp2(x * (c2 + (c2 * 0.044715) * (x * x))))


def _gate_chunk(z_ref, wcat_ref, bs_ref, g_ref, o_ref, t0):
    head_of_lane = lax.broadcasted_iota(jnp.int32, (1, WIDTH_B), 1) // HEAD_DIM
    u = z_ref[pl.ds(t0, CHUNK), :WIDTH_B].astype(F32)
    vn = z_ref[pl.ds(t0, CHUNK), WIDTH_B:]
    stacked = jnp.concatenate(
        [jnp.where(head_of_lane == h, vn, jnp.zeros_like(vn)) for h in range(N_HEADS_B)], axis=0)
    mixed = jnp.dot(wcat_ref[...], stacked, preferred_element_type=F32) + bs_ref[...]
    o_ref[pl.ds(t0, CHUNK), :] = _head_norm_lane_pairs(u * mixed, g_ref[...]).astype(o_ref.dtype)


def _dft_tables(seq):
    def cos_sin(n):
        i = lax.broadcasted_iota(jnp.int32, (n, n), 0)
        j = lax.broadcasted_iota(jnp.int32, (n, n), 1)
        ang = ((i * j) % n).astype(F32) * (2.0 * math.pi / n)
        return jnp.cos(ang), jnp.sin(ang)

    def cos_sin_rows(row_step, n_rows):
        i = lax.broadcasted_iota(jnp.int32, (n_rows, seq), 0) * row_step
        j = lax.broadcasted_iota(jnp.int32, (n_rows, seq), 1)
        ang = ((i * j) % seq).astype(F32) * (2.0 * math.pi / seq)
        return jnp.cos(ang), jnp.sin(ang)

    n_rows = seq // 2 + DFT_TAIL
    n_outer = -(-n_rows // GRID_W)
    ca, sa = (t[:, None, :] for t in cos_sin_rows(GRID_W, n_outer))
    cb, sb = (t[None, :, :] for t in cos_sin_rows(1, GRID_W))
    c_seq = (ca * cb - sa * sb).reshape(n_outer * GRID_W, seq)[:n_rows].astype(BF16)
    s_seq = (sa * cb + ca * sb).reshape(n_outer * GRID_W, seq)[:n_rows].astype(BF16)
    c_ch, s_ch = cos_sin(HEAD_DIM)
    eye = jnp.eye(WIDTH_C // HEAD_DIM, dtype=F32)
    ch_tab = jnp.concatenate([jnp.kron(eye, c_ch), jnp.kron(eye, s_ch)], axis=1).astype(BF16)
    i = lax.broadcasted_iota(jnp.int32, (N_CHUNK, N_CHUNK + DFT_TAIL), 0)
    k = lax.broadcasted_iota(jnp.int32, (N_CHUNK, N_CHUNK + DFT_TAIL), 1)
    reversal = (k == N_CHUNK - i).astype(BF16)
    return c_seq, s_seq, ch_tab, reversal


def _gate_fourier_kernel(z_ref, x_ref, wcat_ref, bs_ref, gb_ref,
                         cos_ref, sin_ref, ch_tab_ref, rev_ref, gc_ref,
                         ob_ref, oc_ref, hi_ref, *, seq):
    gate_args = (z_ref, wcat_ref, bs_ref, gb_ref, ob_ref)
    n_gate = seq // CHUNK
    n_chunks = seq // 2 // N_CHUNK
    gate_done = 0

    def gate_until(n):
        nonlocal gate_done
        for c in range(gate_done, min(n, n_gate)):
            _gate_chunk(*gate_args, c * CHUNK)
        gate_done = max(gate_done, min(n, n_gate))

    xcs = jnp.dot(x_ref[...], ch_tab_ref[...], preferred_element_type=F32)
    xc = xcs[:, :WIDTH_C].astype(BF16)
    xs = xcs[:, WIDTH_C:].astype(BF16)
    scale = 1.0 / math.sqrt(seq * HEAD_DIM)
    g = gc_ref[...]
    per_step = -(-n_gate // (2 * n_chunks))
    for t in range(n_chunks):
        r0 = t * N_CHUNK
        m = N_CHUNK + (DFT_TAIL if t == n_chunks - 1 else 0)
        a = jnp.dot(cos_ref[r0:r0 + m, :], xc, preferred_element_type=F32)
        b = jnp.dot(sin_ref[r0:r0 + m, :], xs, preferred_element_type=F32)
        lo = (a[:N_CHUNK] - b[:N_CHUNK]) * scale
        oc_ref[r0:r0 + N_CHUNK, :] = _head_norm_lane_pairs(lo, g).astype(oc_ref.dtype)
        hi_ref[r0:r0 + m, :] = _head_norm_lane_pairs((a + b) * scale, g).astype(hi_ref.dtype)
        gate_until((t + 1) * per_step)
    for u in range(n_chunks):
        src = hi_ref[u * N_CHUNK:(u + 1) * N_CHUNK + DFT_TAIL, :]
        blk = jnp.dot(rev_ref[...], src, preferred_element_type=F32)
        oc_ref[seq - (u + 1) * N_CHUNK:seq - u * N_CHUNK, :] = blk.astype(oc_ref.dtype)
        gate_until((n_chunks + u + 1) * per_step)
    gate_until(n_gate)


def _gate_fourier(proj, w_s, b_s, tables, g, batch, seq):
    z_off = 3 * WIDTH_A // (2 * WIDTH_B)
    c_off = (3 * WIDTH_A + 2 * WIDTH_B) // WIDTH_C
    wcat = jnp.transpose(w_s, (1, 0, 2)).reshape(CHUNK, N_HEADS_B * CHUNK).astype(BF16)
    bs_full = jnp.repeat(b_s.T.astype(F32), HEAD_DIM, axis=1)
    g_row = g.reshape(1, D_MODEL)
    g_b = g_row[:, WIDTH_A:WIDTH_A + WIDTH_B]
    g_c = g_row[:, WIDTH_A + WIDTH_B:]
    n_resident = 3 + len(tables) + 1
    return pl.pallas_call(
        functools.partial(_gate_fourier_kernel, seq=seq),
        grid=(batch,),
        in_specs=[
            pl.BlockSpec((seq, 2 * WIDTH_B), lambda b: (b, z_off)),
            pl.BlockSpec((seq, WIDTH_C), lambda b: (b, c_off)),
        ] + [_resident()] * n_resident,
        out_specs=[
            pl.BlockSpec((seq, WIDTH_B), lambda b: (b, 0)),
            pl.BlockSpec((seq, WIDTH_C), lambda b: (b, 0)),
        ],
        out_shape=[
            jax.ShapeDtypeStruct((batch * seq, WIDTH_B), BF16),
            jax.ShapeDtypeStruct((batch * seq, WIDTH_C), BF16),
        ],
        scratch_shapes=[pltpu.VMEM((seq // 2 + DFT_TAIL, WIDTH_C), BF16)],
        compiler_params=_compiler_params(1),
        name="gate_fourier",
    )(proj, proj, wcat, bs_full, g_b, *tables, g_c)


def _ffn_kernel(ma_ref, mb_ref, mc_ref, x_ref, wo_ref, gf_ref, wg_ref, wu_ref, wd_ref, gfin_ref,
                o_ref, h_ref, act_ref, *, final):
    x1_ref = o_ref
    mix = jnp.concatenate([ma_ref[...], mb_ref[...], mc_ref[...]], axis=-1)
    for c in range(D_MODEL // N_CHUNK):
        cols = slice(c * N_CHUNK, (c + 1) * N_CHUNK)
        x1_ref[:, cols] = x_ref[:, cols] + jnp.dot(mix, wo_ref[:, cols], preferred_element_type=F32)
    h_ref[...] = _rms_norm_rows(x1_ref[...], gf_ref[...]).astype(BF16)
    for c in range(D_FF // N_CHUNK):
        cols = slice(c * N_CHUNK, (c + 1) * N_CHUNK)
        gate = jnp.dot(h_ref[...], wg_ref[:, cols], preferred_element_type=F32)
        up = jnp.dot(h_ref[...], wu_ref[:, cols], preferred_element_type=F32)
        act_ref[:, cols] = (gate * jax.nn.sigmoid(gate) * up).astype(BF16)
    for c in range(D_MODEL // N_CHUNK):
        cols = slice(c * N_CHUNK, (c + 1) * N_CHUNK)
        x1_ref[:, cols] = x1_ref[:, cols] + jnp.dot(act_ref[...], wd_ref[:, cols],
                                                    preferred_element_type=F32)
    if final:
        o_ref[...] = _rms_norm_rows(x1_ref[...], gfin_ref[...])


def _out_proj_ffn(mix_a, mix_b, mix_c, x2d, w_out, g_ffn, w_gate, w_up, w_down, g_final, layer,
                  final):
    n = x2d.shape[0]
    row = lambda width: pl.BlockSpec((FFN_TM, width), lambda i: (i, 0))
    slab = lambda w: _layer_slab(w, layer, 1)
    return pl.pallas_call(
        functools.partial(_ffn_kernel, final=final),
        grid=(n // FFN_TM,),
        in_specs=[row(WIDTH_A), row(WIDTH_B), row(WIDTH_C), row(D_MODEL),
                  slab(w_out), _resident(), slab(w_gate), slab(w_up), slab(w_down), _resident()],
        out_specs=row(D_MODEL),
        out_shape=jax.ShapeDtypeStruct((n, D_MODEL), F32),
        scratch_shapes=[
            pltpu.VMEM((FFN_TM, D_MODEL), BF16),
            pltpu.VMEM((FFN_TM, D_FF), BF16),
        ],
        compiler_params=_compiler_params(1),
        name="out_proj_ffn",
    )(mix_a, mix_b, mix_c, x2d, w_out, g_ffn.reshape(1, D_MODEL), w_gate, w_up, w_down,
      g_final.reshape(1, D_MODEL))


def kernel(x, norm_mix_g, w_in, rpb, gmlp_ln_g, gmlp_ln_b, w_spatial, b_spatial, head_norm_g, w_out,
           norm_ffn_g, w_gate, w_up, w_down, final_norm_g):
    batch, seq, d_model = x.shape
    assert d_model == D_MODEL and seq % (2 * N_CHUNK) == 0 and seq // GRID_W >= NA_ROWS
    depth = w_in.shape[0]
    dft_tables = _dft_tables(seq)
    bias_tab = _attention_bias_table(rpb)
    w_in, w_out, w_gate, w_up, w_down = (w.astype(BF16) for w in (w_in, w_out, w_gate, w_up, w_down))
    x2d = x.reshape(batch * seq, D_MODEL)
    for l in range(depth):
        proj = _in_proj(x2d, norm_mix_g[l], w_in, gmlp_ln_g[l], gmlp_ln_b[l], l)
        mix_a = _attention(proj, bias_tab, head_norm_g[l], l, batch, seq)
        mix_b, mix_c = _gate_fourier(proj, w_spatial[l], b_spatial[l], dft_tables, head_norm_g[l],
                                     batch, seq)
        x2d = _out_proj_ffn(mix_a, mix_b, mix_c, x2d, w_out, norm_ffn_g[l], w_gate, w_up, w_down,
                            final_norm_g, l, final=(l == depth - 1))
    return x2d.reshape(batch, seq, D_MODEL)
```

```python
import functools
import math

import numpy as np
import jax
import jax.numpy as jnp
from jax import lax
from jax.experimental import pallas as pl
from jax.experimental.pallas import tpu as pltpu

F32 = jnp.float32
BF16 = jnp.bfloat16

D_MODEL = 1024
GRID_W = 64
HEAD_DIM = 64
WIDTH_A = D_MODEL // 2
WIDTH_B = D_MODEL // 4
WIDTH_C = D_MODEL // 4
N_HEADS_A = WIDTH_A // HEAD_DIM
N_HEADS_B = WIDTH_B // HEAD_DIM
IN_WIDTH = 3 * WIDTH_A + 2 * WIDTH_B + WIDTH_C
NA_ROWS = 8
NA_COLS = 16
CHUNK = 128
D_FF = -(-8 * D_MODEL // (3 * 256)) * 256
EPS = 1e-6
NEG_INF = -1e9

LANES = 128
VMEM_LIMIT_BYTES = 56 * 1024 * 1024

HEADS_PER_STEP = 4
KEY_SLAB = NA_ROWS * GRID_W
ATTN_GROUP = 4
LOG2_E = math.log2(math.e)
Q_SCALE = HEAD_DIM ** -0.5 * LOG2_E

IN_TM = 1024
IN_RB = 512
FFN_TM = 1024
N_CHUNK = 256
DFT_TAIL = 16


def _compiler_params(n_grid_dims):
    return pltpu.CompilerParams(
        dimension_semantics=("arbitrary",) * n_grid_dims,
        vmem_limit_bytes=VMEM_LIMIT_BYTES,
    )


def _resident():
    return pl.BlockSpec(memory_space=pltpu.VMEM)


def _layer_slab(stacked, layer, n_grid_dims):
    zeros = (0,) * (stacked.ndim - 1)
    index_map = {1: lambda i: (layer,) + zeros, 2: lambda i, j: (layer,) + zeros}[n_grid_dims]
    return pl.BlockSpec((None,) + stacked.shape[1:], index_map, pipeline_mode=pl.Buffered(1))


def _rms_norm_rows(x, g):
    ms = jnp.mean(x * x, axis=-1, keepdims=True)
    return x * lax.rsqrt(ms + EPS) * g


def _head_norm_lane_pairs(y, g):
    n_tiles = y.shape[-1] // LANES
    lane = lax.broadcasted_iota(jnp.int32, (1, LANES), 1)
    first = lane < HEAD_DIM
    outs = []
    for t in range(n_tiles):
        yt = y[:, t * LANES:(t + 1) * LANES]
        sq = yt * yt
        ss_a = jnp.sum(jnp.where(first, sq, 0.0), axis=-1, keepdims=True)
        ss_b = jnp.sum(jnp.where(first, 0.0, sq), axis=-1, keepdims=True)
        ss = jnp.where(first, ss_a, ss_b)
        gain = g[:, t * LANES:(t + 1) * LANES] * math.sqrt(HEAD_DIM)
        outs.append(yt * lax.rsqrt(ss + HEAD_DIM * EPS) * gain)
    return outs[0] if n_tiles == 1 else jnp.concatenate(outs, axis=-1)


def _in_proj_kernel(x_ref, g_ref, w_ref, lng_ref, lnb_ref, o_ref, h_ref):
    assert N_CHUNK == WIDTH_B
    u_chunk = 3 * WIDTH_A // N_CHUNK
    v_chunk = u_chunk + 1
    others = [c for c in range(IN_WIDTH // N_CHUNK) if c not in (u_chunk, v_chunk)]
    for rb in range(IN_TM // IN_RB):
        rows = slice(rb * IN_RB, (rb + 1) * IN_RB)
        h_ref[rows, :] = _rms_norm_rows(x_ref[rows, :], g_ref[...]).astype(BF16)
        for c in [v_chunk, u_chunk] + others:
            cols = slice(c * N_CHUNK, (c + 1) * N_CHUNK)
            acc = jnp.dot(h_ref[rows, :], w_ref[:, cols], preferred_element_type=F32)
            if (c + 1) * N_CHUNK <= WIDTH_A:
                acc = acc * Q_SCALE
            elif c in (u_chunk, v_chunk):
                acc = _gelu_tanh(acc)
                if c == v_chunk:
                    mu = jnp.mean(acc, axis=-1, keepdims=True)
                    d = acc - mu
                    var = jnp.mean(d * d, axis=-1, keepdims=True)
                    acc = d * lax.rsqrt(var + EPS) * lng_ref[...] + lnb_ref[...]
            o_ref[rows, cols] = acc.astype(o_ref.dtype)


def _in_proj(x2d, g, w_stack, ln_g, ln_b, layer):
    n = x2d.shape[0]
    return pl.pallas_call(
        _in_proj_kernel,
        grid=(n // IN_TM,),
        in_specs=[
            pl.BlockSpec((IN_TM, D_MODEL), lambda i: (i, 0)),
            _resident(),
            _layer_slab(w_stack, layer, 1),
            _resident(),
            _resident(),
        ],
        out_specs=pl.BlockSpec((IN_TM, IN_WIDTH), lambda i: (i, 0)),
        out_shape=jax.ShapeDtypeStruct((n, IN_WIDTH), BF16),
        scratch_shapes=[pltpu.VMEM((IN_TM, D_MODEL), BF16)],
        compiler_params=_compiler_params(1),
        name="in_proj",
    )(x2d, g.reshape(1, D_MODEL), w_stack, ln_g.reshape(1, WIDTH_B), ln_b.reshape(1, WIDTH_B))


def _attn_kernel(q_ref, k_ref, v_ref, bias_ref, g_ref, o_ref, p_ref, l_ref, *, n_rows):
    width = HEADS_PER_STEP * HEAD_DIM
    head_of_lane = lax.broadcasted_iota(jnp.int32, (1, width), 1) // HEAD_DIM
    g = g_ref[...]
    n_groups = n_rows // ATTN_GROUP

    def rows_of(group):
        for j in range(ATTN_GROUP):
            r = group * ATTN_GROUP + j
            rs = jnp.clip(r - NA_ROWS // 2, 0, n_rows - NA_ROWS)
            yield j, r, rs, pl.multiple_of(r * GRID_W, GRID_W), pl.multiple_of(rs * GRID_W, GRID_W)

    def probs(group, slot):
        for j, r, rs, q0, k0 in rows_of(group):
            q = q_ref[pl.ds(q0, GRID_W), :]
            zero = jnp.zeros_like(q)
            qm = jnp.concatenate(
                [jnp.where(head_of_lane == h, q, zero) for h in range(HEADS_PER_STEP)], axis=0)
            s = lax.dot_general(qm, k_ref[pl.ds(k0, KEY_SLAB), :], (((1,), (1,)), ((), ())),
                                preferred_element_type=F32)
            a0 = NA_ROWS - 1 - (r - rs)
            s = s + jnp.concatenate(
                [jnp.concatenate([bias_ref[h, a0 + 2 * i] for i in range(NA_ROWS // 2)], axis=-1)
                 for h in range(HEADS_PER_STEP)], axis=0)
            m = jnp.max(s, axis=-1, keepdims=True)
            p = jnp.exp2(s - m)
            l = jnp.sum(p, axis=-1, keepdims=True)
            l_ref[slot, j] = jnp.broadcast_to(l, (l.shape[0], LANES))
            p_ref[slot, j] = p.astype(BF16)

    def outputs(group, slot):
        for j, _, _, q0, k0 in rows_of(group):
            o4 = jnp.dot(p_ref[slot, j], v_ref[pl.ds(k0, KEY_SLAB), :], preferred_element_type=F32)
            inv_l = 1.0 / l_ref[slot, j]
            o4 = o4 * jnp.concatenate([inv_l] * (width // LANES), axis=-1)
            o = o4[:GRID_W]
            for h in range(1, HEADS_PER_STEP):
                o = jnp.where(head_of_lane == h, o4[h * GRID_W:(h + 1) * GRID_W], o)
            o_ref[pl.ds(q0, GRID_W), :] = _head_norm_lane_pairs(o, g).astype(o_ref.dtype)

    def two_groups(i, carry):
        for slot in range(2):
            probs(2 * i + slot, slot)
            outputs(2 * i + slot, slot)
        return carry

    lax.fori_loop(0, n_groups // 2, two_groups, 0)


def _attention(proj, bias_tab, g, layer, batch, seq):
    width = HEADS_PER_STEP * HEAD_DIM
    n_groups = WIDTH_A // width
    k_off = WIDTH_A // width
    v_off = 2 * WIDTH_A // width
    return pl.pallas_call(
        functools.partial(_attn_kernel, n_rows=seq // GRID_W),
        grid=(n_groups, batch),
        in_specs=[
            pl.BlockSpec((seq, width), lambda hg, b: (b, hg)),
            pl.BlockSpec((seq, width), lambda hg, b: (b, k_off + hg)),
            pl.BlockSpec((seq, width), lambda hg, b: (b, v_off + hg)),
            pl.BlockSpec((None, HEADS_PER_STEP) + bias_tab.shape[2:],
                         lambda hg, b: (layer, hg, 0, 0, 0)),
            pl.BlockSpec((1, width), lambda hg, b: (0, hg)),
        ],
        out_specs=pl.BlockSpec((seq, width), lambda hg, b: (b, hg)),
        out_shape=jax.ShapeDtypeStruct((batch * seq, WIDTH_A), BF16),
        scratch_shapes=[
            pltpu.VMEM((2, ATTN_GROUP, HEADS_PER_STEP * GRID_W, KEY_SLAB), BF16),
            pltpu.VMEM((2, ATTN_GROUP, HEADS_PER_STEP * GRID_W, LANES), F32),
        ],
        compiler_params=_compiler_params(2),
        name="nbr_attention",
    )(proj, proj, proj, bias_tab, g.reshape(1, D_MODEL)[:, :WIDTH_A])


def _attention_bias_table(rpb):
    qc = np.arange(GRID_W)[:, None]
    kc = np.arange(GRID_W)[None, :]
    win = np.clip(qc - NA_COLS // 2, 0, GRID_W - NA_COLS)
    col_ok = (kc >= win) & (kc < win + NA_COLS)
    n_rel = rpb.shape[-1]
    onehot = (kc - qc + NA_COLS - 1)[None] == np.arange(n_rel)[:, None, None]
    zeros = np.zeros_like(onehot)
    onehot2 = np.concatenate([np.concatenate([onehot, zeros], axis=-1),
                              np.concatenate([zeros, onehot], axis=-1)], axis=0)
    rows2 = jnp.concatenate([rpb[:, :, :-1], rpb[:, :, 1:]], axis=-1).astype(F32)
    tiles = jnp.einsum("lhab,bqk->lhaqk", rows2, jnp.asarray(onehot2, F32),
                       precision=lax.Precision.HIGHEST)
    return jnp.where(jnp.asarray(np.tile(col_ok, (1, 2))), tiles, NEG_INF) * LOG2_E


def _gelu_tanh(x):
    c2 = -2.0 * math.sqrt(2.0 / math.pi) * LOG2_E
    return x / (1.0 + jnp.exp2(x * (c2 + (c2 * 0.044715) * (x * x))))


def _gate_chunk(z_ref, wcat_ref, bs_ref, g_ref, o_ref, t0):
    head_of_lane = lax.broadcasted_iota(jnp.int32, (1, WIDTH_B), 1) // HEAD_DIM
    u = z_ref[pl.ds(t0, CHUNK), :WIDTH_B].astype(F32)
    vn = z_ref[pl.ds(t0, CHUNK), WIDTH_B:]
    stacked = jnp.concatenate(
        [jnp.where(head_of_lane == h, vn, jnp.zeros_like(vn)) for h in range(N_HEADS_B)], axis=0)
    mixed = jnp.dot(wcat_ref[...], stacked, preferred_element_type=F32) + bs_ref[...]
    o_ref[pl.ds(t0, CHUNK), :] = _head_norm_lane_pairs(u * mixed, g_ref[...]).astype(o_ref.dtype)


def _dft_tables(seq):
    def cos_sin(n):
        i = lax.broadcasted_iota(jnp.int32, (n, n), 0)
        j = lax.broadcasted_iota(jnp.int32, (n, n), 1)
        ang = ((i * j) % n).astype(F32) * (2.0 * math.pi / n)
        return jnp.cos(ang), jnp.sin(ang)

    def cos_sin_rows(row_step, n_rows):
        i = lax.broadcasted_iota(jnp.int32, (n_rows, seq), 0) * row_step
        j = lax.broadcasted_iota(jnp.int32, (n_rows, seq), 1)
        ang = ((i * j) % seq).astype(F32) * (2.0 * math.pi / seq)
        return jnp.cos(ang), jnp.sin(ang)

    n_rows = seq // 2 + DFT_TAIL
    n_outer = -(-n_rows // GRID_W)
    ca, sa = (t[:, None, :] for t in cos_sin_rows(GRID_W, n_outer))
    cb, sb = (t[None, :, :] for t in cos_sin_rows(1, GRID_W))
    c_seq = (ca * cb - sa * sb).reshape(n_outer * GRID_W, seq)[:n_rows].astype(BF16)
    s_seq = (sa * cb + ca * sb).reshape(n_outer * GRID_W, seq)[:n_rows].astype(BF16)
    c_ch, s_ch = cos_sin(HEAD_DIM)
    eye = jnp.eye(WIDTH_C // HEAD_DIM, dtype=F32)
    ch_tab = jnp.concatenate([jnp.kron(eye, c_ch), jnp.kron(eye, s_ch)], axis=1).astype(BF16)
    i = lax.broadcasted_iota(jnp.int32, (N_CHUNK, N_CHUNK + DFT_TAIL), 0)
    k = lax.broadcasted_iota(jnp.int32, (N_CHUNK, N_CHUNK + DFT_TAIL), 1)
    reversal = (k == N_CHUNK - i).astype(BF16)
    return c_seq, s_seq, ch_tab, reversal


def _gate_fourier_kernel(z_ref, x_ref, wcat_ref, bs_ref, gb_ref,
                         cos_ref, sin_ref, ch_tab_ref, rev_ref, gc_ref,
                         ob_ref, oc_ref, hi_ref, *, seq):
    gate_args = (z_ref, wcat_ref, bs_ref, gb_ref, ob_ref)
    n_gate = seq // CHUNK
    n_chunks = seq // 2 // N_CHUNK
    gate_done = 0

    def gate_until(n):
        nonlocal gate_done
        for c in range(gate_done, min(n, n_gate)):
            _gate_chunk(*gate_args, c * CHUNK)
        gate_done = max(gate_done, min(n, n_gate))

    xcs = jnp.dot(x_ref[...], ch_tab_ref[...], preferred_element_type=F32)
    xc = xcs[:, :WIDTH_C].astype(BF16)
    xs = xcs[:, WIDTH_C:].astype(BF16)
    scale = 1.0 / math.sqrt(seq * HEAD_DIM)
    g = gc_ref[...]
    per_step = -(-n_gate // (2 * n_chunks))
    for t in range(n_chunks):
        r0 = t * N_CHUNK
        m = N_CHUNK + (DFT_TAIL if t == n_chunks - 1 else 0)
        a = jnp.dot(cos_ref[r0:r0 + m, :], xc, preferred_element_type=F32)
        b = jnp.dot(sin_ref[r0:r0 + m, :], xs, preferred_element_type=F32)
        lo = (a[:N_CHUNK] - b[:N_CHUNK]) * scale
        oc_ref[r0:r0 + N_CHUNK, :] = _head_norm_lane_pairs(lo, g).astype(oc_ref.dtype)
        hi_ref[r0:r0 + m, :] = _head_norm_lane_pairs((a + b) * scale, g).astype(hi_ref.dtype)
        gate_until((t + 1) * per_step)
    for u in range(n_chunks):
        src = hi_ref[u * N_CHUNK:(u + 1) * N_CHUNK + DFT_TAIL, :]
        blk = jnp.dot(rev_ref[...], src, preferred_element_type=F32)
        oc_ref[seq - (u + 1) * N_CHUNK:seq - u * N_CHUNK, :] = blk.astype(oc_ref.dtype)
        gate_until((n_chunks + u + 1) * per_step)
    gate_until(n_gate)


def _gate_fourier(proj, w_s, b_s, tables, g, batch, seq):
    z_off = 3 * WIDTH_A // (2 * WIDTH_B)
    c_off = (3 * WIDTH_A + 2 * WIDTH_B) // WIDTH_C
    wcat = jnp.transpose(w_s, (1, 0, 2)).reshape(CHUNK, N_HEADS_B * CHUNK).astype(BF16)
    bs_full = jnp.repeat(b_s.T.astype(F32), HEAD_DIM, axis=1)
    g_row = g.reshape(1, D_MODEL)
    g_b = g_row[:, WIDTH_A:WIDTH_A + WIDTH_B]
    g_c = g_row[:, WIDTH_A + WIDTH_B:]
    n_resident = 3 + len(tables) + 1
    return pl.pallas_call(
        functools.partial(_gate_fourier_kernel, seq=seq),
        grid=(batch,),
        in_specs=[
            pl.BlockSpec((seq, 2 * WIDTH_B), lambda b: (b, z_off)),
            pl.BlockSpec((seq, WIDTH_C), lambda b: (b, c_off)),
        ] + [_resident()] * n_resident,
        out_specs=[
            pl.BlockSpec((seq, WIDTH_B), lambda b: (b, 0)),
            pl.BlockSpec((seq, WIDTH_C), lambda b: (b, 0)),
        ],
        out_shape=[
            jax.ShapeDtypeStruct((batch * seq, WIDTH_B), BF16),
            jax.ShapeDtypeStruct((batch * seq, WIDTH_C), BF16),
        ],
        scratch_shapes=[pltpu.VMEM((seq // 2 + DFT_TAIL, WIDTH_C), BF16)],
        compiler_params=_compiler_params(1),
        name="gate_fourier",
    )(proj, proj, wcat, bs_full, g_b, *tables, g_c)


def _ffn_kernel(ma_ref, mb_ref, mc_ref, x_ref, wo_ref, gf_ref, wg_ref, wu_ref, wd_ref, gfin_ref,
                o_ref, h_ref, act_ref, *, final):
    x1_ref = o_ref
    mix = jnp.concatenate([ma_ref[...], mb_ref[...], mc_ref[...]], axis=-1)
    for c in range(D_MODEL // N_CHUNK):
        cols = slice(c * N_CHUNK, (c + 1) * N_CHUNK)
        x1_ref[:, cols] = x_ref[:, cols] + jnp.dot(mix, wo_ref[:, cols], preferred_element_type=F32)
    h_ref[...] = _rms_norm_rows(x1_ref[...], gf_ref[...]).astype(BF16)
    for c in range(D_FF // N_CHUNK):
        cols = slice(c * N_CHUNK, (c + 1) * N_CHUNK)
        gate = jnp.dot(h_ref[...], wg_ref[:, cols], preferred_element_type=F32)
        up = jnp.dot(h_ref[...], wu_ref[:, cols], preferred_element_type=F32)
        act_ref[:, cols] = (gate * jax.nn.sigmoid(gate) * up).astype(BF16)
    for c in range(D_MODEL // N_CHUNK):
        cols = slice(c * N_CHUNK, (c + 1) * N_CHUNK)
        x1_ref[:, cols] = x1_ref[:, cols] + jnp.dot(act_ref[...], wd_ref[:, cols],
                                                    preferred_element_type=F32)
    if final:
        o_ref[...] = _rms_norm_rows(x1_ref[...], gfin_ref[...])


def _out_proj_ffn(mix_a, mix_b, mix_c, x2d, w_out, g_ffn, w_gate, w_up, w_down, g_final, layer,
                  final):
    n = x2d.shape[0]
    row = lambda width: pl.BlockSpec((FFN_TM, width), lambda i: (i, 0))
    slab = lambda w: _layer_slab(w, layer, 1)
    return pl.pallas_call(
        functools.partial(_ffn_kernel, final=final),
        grid=(n // FFN_TM,),
        in_specs=[row(WIDTH_A), row(WIDTH_B), row(WIDTH_C), row(D_MODEL),
                  slab(w_out), _resident(), slab(w_gate), slab(w_up), slab(w_down), _resident()],
        out_specs=row(D_MODEL),
        out_shape=jax.ShapeDtypeStruct((n, D_MODEL), F32),
        scratch_shapes=[
            pltpu.VMEM((FFN_TM, D_MODEL), BF16),
            pltpu.VMEM((FFN_TM, D_FF), BF16),
        ],
        compiler_params=_compiler_params(1),
        name="out_proj_ffn",
    )(mix_a, mix_b, mix_c, x2d, w_out, g_ffn.reshape(1, D_MODEL), w_gate, w_up, w_down,
      g_final.reshape(1, D_MODEL))


def kernel(x, norm_mix_g, w_in, rpb, gmlp_ln_g, gmlp_ln_b, w_spatial, b_spatial, head_norm_g, w_out,
           norm_ffn_g, w_gate, w_up, w_down, final_norm_g):
    batch, seq, d_model = x.shape
    assert d_model == D_MODEL and seq % (2 * N_CHUNK) == 0 and seq // GRID_W >= NA_ROWS
    depth = w_in.shape[0]
    dft_tables = _dft_tables(seq)
    bias_tab = _attention_bias_table(rpb)
    w_in, w_out, w_gate, w_up, w_down = (w.astype(BF16) for w in (w_in, w_out, w_gate, w_up, w_down))
    x2d = x.reshape(batch * seq, D_MODEL)
    for l in range(depth):
        proj = _in_proj(x2d, norm_mix_g[l], w_in, gmlp_ln_g[l], gmlp_ln_b[l], l)
        mix_a = _attention(proj, bias_tab, head_norm_g[l], l, batch, seq)
        mix_b, mix_c = _gate_fourier(proj, w_spatial[l], b_spatial[l], dft_tables, head_norm_g[l],
                                     batch, seq)
        x2d = _out_proj_ffn(mix_a, mix_b, mix_c, x2d, w_out, norm_ffn_g[l], w_gate, w_up, w_down,
                            final_norm_g, l, final=(l == depth - 1))
    return x2d.reshape(batch, seq, D_MODEL)
```

```python
import functools
import math

import numpy as np
import jax
import jax.numpy as jnp
from jax import lax
from jax.experimental import pallas as pl
from jax.experimental.pallas import tpu as pltpu

F32 = jnp.float32
BF16 = jnp.bfloat16

D_MODEL = 1024
GRID_W = 64
HEAD_DIM = 64
WIDTH_A = D_MODEL // 2
WIDTH_B = D_MODEL // 4
WIDTH_C = D_MODEL // 4
N_HEADS_A = WIDTH_A // HEAD_DIM
N_HEADS_B = WIDTH_B // HEAD_DIM
IN_WIDTH = 3 * WIDTH_A + 2 * WIDTH_B + WIDTH_C
NA_ROWS = 8
NA_COLS = 16
CHUNK = 128
D_FF = -(-8 * D_MODEL // (3 * 256)) * 256
EPS = 1e-6
NEG_INF = -1e9

LANES = 128
VMEM_LIMIT_BYTES = 56 * 1024 * 1024

HEADS_PER_STEP = 4
KEY_SLAB = NA_ROWS * GRID_W
ATTN_GROUP = 8
LOG2_E = math.log2(math.e)
Q_SCALE = HEAD_DIM ** -0.5 * LOG2_E

IN_TM = 1024
IN_RB = 512
FFN_TM = 1024
N_CHUNK = 256
DFT_TAIL = 16


def _compiler_params(n_grid_dims):
    return pltpu.CompilerParams(
        dimension_semantics=("arbitrary",) * n_grid_dims,
        vmem_limit_bytes=VMEM_LIMIT_BYTES,
    )


def _resident():
    return pl.BlockSpec(memory_space=pltpu.VMEM)


def _layer_slab(stacked, layer, n_grid_dims):
    zeros = (0,) * (stacked.ndim - 1)
    index_map = {1: lambda i: (layer,) + zeros, 2: lambda i, j: (layer,) + zeros}[n_grid_dims]
    return pl.BlockSpec((None,) + stacked.shape[1:], index_map, pipeline_mode=pl.Buffered(1))


def _rms_norm_rows(x, g):
    ms = jnp.mean(x * x, axis=-1, keepdims=True)
    return x * lax.rsqrt(ms + EPS) * g


def _head_norm_lane_pairs(y, g):
    n_tiles = y.shape[-1] // LANES
    lane = lax.broadcasted_iota(jnp.int32, (1, LANES), 1)
    first = lane < HEAD_DIM
    outs = []
    for t in range(n_tiles):
        yt = y[:, t * LANES:(t + 1) * LANES]
        sq = yt * yt
        ss_a = jnp.sum(jnp.where(first, sq, 0.0), axis=-1, keepdims=True)
        ss_b = jnp.sum(jnp.where(first, 0.0, sq), axis=-1, keepdims=True)
        ss = jnp.where(first, ss_a, ss_b)
        gain = g[:, t * LANES:(t + 1) * LANES] * math.sqrt(HEAD_DIM)
        outs.append(yt * lax.rsqrt(ss + HEAD_DIM * EPS) * gain)
    return outs[0] if n_tiles == 1 else jnp.concatenate(outs, axis=-1)


def _in_proj_kernel(x_ref, g_ref, w_ref, lng_ref, lnb_ref, o_ref, h_ref):
    assert N_CHUNK == WIDTH_B
    u_chunk = 3 * WIDTH_A // N_CHUNK
    v_chunk = u_chunk + 1
    others = [c for c in range(IN_WIDTH // N_CHUNK) if c not in (u_chunk, v_chunk)]
    for rb in range(IN_TM // IN_RB):
        rows = slice(rb * IN_RB, (rb + 1) * IN_RB)
        h_ref[rows, :] = _rms_norm_rows(x_ref[rows, :], g_ref[...]).astype(BF16)
        for c in [v_chunk, u_chunk] + others:
            cols = slice(c * N_CHUNK, (c + 1) * N_CHUNK)
            acc = jnp.dot(h_ref[rows, :], w_ref[:, cols], preferred_element_type=F32)
            if (c + 1) * N_CHUNK <= WIDTH_A:
                acc = acc * Q_SCALE
            elif c in (u_chunk, v_chunk):
                acc = _gelu_tanh(acc)
                if c == v_chunk:
                    mu = jnp.mean(acc, axis=-1, keepdims=True)
                    d = acc - mu
                    var = jnp.mean(d * d, axis=-1, keepdims=True)
                    acc = d * lax.rsqrt(var + EPS) * lng_ref[...] + lnb_ref[...]
            o_ref[rows, cols] = acc.astype(o_ref.dtype)


def _in_proj(x2d, g, w_stack, ln_g, ln_b, layer):
    n = x2d.shape[0]
    return pl.pallas_call(
        _in_proj_kernel,
        grid=(n // IN_TM,),
        in_specs=[
            pl.BlockSpec((IN_TM, D_MODEL), lambda i: (i, 0)),
            _resident(),
            _layer_slab(w_stack, layer, 1),
            _resident(),
            _resident(),
        ],
        out_specs=pl.BlockSpec((IN_TM, IN_WIDTH), lambda i: (i, 0)),
        out_shape=jax.ShapeDtypeStruct((n, IN_WIDTH), BF16),
        scratch_shapes=[pltpu.VMEM((IN_TM, D_MODEL), BF16)],
        compiler_params=_compiler_params(1),
        name="in_proj",
    )(x2d, g.reshape(1, D_MODEL), w_stack, ln_g.reshape(1, WIDTH_B), ln_b.reshape(1, WIDTH_B))


def _attn_kernel(q_ref, k_ref, v_ref, bias_ref, g_ref, o_ref, p_ref, l_ref, *, n_rows):
    width = HEADS_PER_STEP * HEAD_DIM
    head_of_lane = lax.broadcasted_iota(jnp.int32, (1, width), 1) // HEAD_DIM
    g = g_ref[...]
    n_groups = n_rows // ATTN_GROUP

    def rows_of(group):
        for j in range(ATTN_GROUP):
            r = group * ATTN_GROUP + j
            rs = jnp.clip(r - NA_ROWS // 2, 0, n_rows - NA_ROWS)
            yield j, r, rs, pl.multiple_of(r * GRID_W, GRID_W), pl.multiple_of(rs * GRID_W, GRID_W)

    def probs(group, slot):
        for j, r, rs, q0, k0 in rows_of(group):
            q = q_ref[pl.ds(q0, GRID_W), :]
            zero = jnp.zeros_like(q)
            qm = jnp.concatenate(
                [jnp.where(head_of_lane == h, q, zero) for h in range(HEADS_PER_STEP)], axis=0)
            s = lax.dot_general(qm, k_ref[pl.ds(k0, KEY_SLAB), :], (((1,), (1,)), ((), ())),
                                preferred_element_type=F32)
            a0 = NA_ROWS - 1 - (r - rs)
            s = s + jnp.concatenate(
                [jnp.concatenate([bias_ref[h, a0 + 2 * i] for i in range(NA_ROWS // 2)], axis=-1)
                 for h in range(HEADS_PER_STEP)], axis=0)
            m = jnp.max(s, axis=-1, keepdims=True)
            p = jnp.exp2(s - m)
            l = jnp.sum(p, axis=-1, keepdims=True)
            l_ref[slot, j] = jnp.broadcast_to(l, (l.shape[0], LANES))
            p_ref[slot, j] = p.astype(BF16)

    def outputs(group, slot):
        for j, _, _, q0, k0 in rows_of(group):
            o4 = jnp.dot(p_ref[slot, j], v_ref[pl.ds(k0, KEY_SLAB), :], preferred_element_type=F32)
            inv_l = 1.0 / l_ref[slot, j]
            o4 = o4 * jnp.concatenate([inv_l] * (width // LANES), axis=-1)
            o = o4[:GRID_W]
            for h in range(1, HEADS_PER_STEP):
                o = jnp.where(head_of_lane == h, o4[h * GRID_W:(h + 1) * GRID_W], o)
            o_ref[pl.ds(q0, GRID_W), :] = _head_norm_lane_pairs(o, g).astype(o_ref.dtype)

    def two_groups(i, carry):
        for slot in range(2):
            probs(2 * i + slot, slot)
            outputs(2 * i + slot, slot)
        return carry

    lax.fori_loop(0, n_groups // 2, two_groups, 0)


def _attention(proj, bias_tab, g, layer, batch, seq):
    width = HEADS_PER_STEP * HEAD_DIM
    n_groups = WIDTH_A // width
    k_off = WIDTH_A // width
    v_off = 2 * WIDTH_A // width
    return pl.pallas_call(
        functools.partial(_attn_kernel, n_rows=seq // GRID_W),
        grid=(n_groups, batch),
        in_specs=[
            pl.BlockSpec((seq, width), lambda hg, b: (b, hg)),
            pl.BlockSpec((seq, width), lambda hg, b: (b, k_off + hg)),
            pl.BlockSpec((seq, width), lambda hg, b: (b, v_off + hg)),
            pl.BlockSpec((None, HEADS_PER_STEP) + bias_tab.shape[2:],
                         lambda hg, b: (layer, hg, 0, 0, 0)),
            pl.BlockSpec((1, width), lambda hg, b: (0, hg)),
        ],
        out_specs=pl.BlockSpec((seq, width), lambda hg, b: (b, hg)),
        out_shape=jax.ShapeDtypeStruct((batch * seq, WIDTH_A), BF16),
        scratch_shapes=[
            pltpu.VMEM((2, ATTN_GROUP, HEADS_PER_STEP * GRID_W, KEY_SLAB), BF16),
            pltpu.VMEM((2, ATTN_GROUP, HEADS_PER_STEP * GRID_W, LANES), F32),
        ],
        compiler_params=_compiler_params(2),
        name="nbr_attention",
    )(proj, proj, proj, bias_tab, g.reshape(1, D_MODEL)[:, :WIDTH_A])


def _attention_bias_table(rpb):
    qc = np.arange(GRID_W)[:, None]
    kc = np.arange(GRID_W)[None, :]
    win = np.clip(qc - NA_COLS // 2, 0, GRID_W - NA_COLS)
    col_ok = (kc >= win) & (kc < win + NA_COLS)
    n_rel = rpb.shape[-1]
    onehot = (kc - qc + NA_COLS - 1)[None] == np.arange(n_rel)[:, None, None]
    zeros = np.zeros_like(onehot)
    onehot2 = np.concatenate([np.concatenate([onehot, zeros], axis=-1),
                              np.concatenate([zeros, onehot], axis=-1)], axis=0)
    rows2 = jnp.concatenate([rpb[:, :, :-1], rpb[:, :, 1:]], axis=-1).astype(F32)
    tiles = jnp.einsum("lhab,bqk->lhaqk", rows2, jnp.asarray(onehot2, F32),
                       precision=lax.Precision.HIGHEST)
    return jnp.where(jnp.asarray(np.tile(col_ok, (1, 2))), tiles, NEG_INF) * LOG2_E


def _gelu_tanh(x):
    c2 = -2.0 * math.sqrt(2.0 / math.pi) * LOG2_E
    return x / (1.0 + jnp.exp2(x * (c2 + (c2 * 0.044715) * (x * x))))


def _gate_chunk(z_ref, wcat_ref, bs_ref, g_ref, o_ref, t0):
    head_of_lane = lax.broadcasted_iota(jnp.int32, (1, WIDTH_B), 1) // HEAD_DIM
    u = z_ref[pl.ds(t0, CHUNK), :WIDTH_B].astype(F32)
    vn = z_ref[pl.ds(t0, CHUNK), WIDTH_B:]
    stacked = jnp.concatenate(
        [jnp.where(head_of_lane == h, vn, jnp.zeros_like(vn)) for h in range(N_HEADS_B)], axis=0)
    mixed = jnp.dot(wcat_ref[...], stacked, preferred_element_type=F32) + bs_ref[...]
    o_ref[pl.ds(t0, CHUNK), :] = _head_norm_lane_pairs(u * mixed, g_ref[...]).astype(o_ref.dtype)


def _dft_tables(seq):
    def cos_sin(n):
        i = lax.broadcasted_iota(jnp.int32, (n, n), 0)
        j = lax.broadcasted_iota(jnp.int32, (n, n), 1)
        ang = ((i * j) % n).astype(F32) * (2.0 * math.pi / n)
        return jnp.cos(ang), jnp.sin(ang)

    def cos_sin_rows(row_step, n_rows):
        i = lax.broadcasted_iota(jnp.int32, (n_rows, seq), 0) * row_step
        j = lax.broadcasted_iota(jnp.int32, (n_rows, seq), 1)
        ang = ((i * j) % seq).astype(F32) * (2.0 * math.pi / seq)
        return jnp.cos(ang), jnp.sin(ang)

    n_rows = seq // 2 + DFT_TAIL
    n_outer = -(-n_rows // GRID_W)
    ca, sa = (t[:, None, :] for t in cos_sin_rows(GRID_W, n_outer))
    cb, sb = (t[None, :, :] for t in cos_sin_rows(1, GRID_W))
    c_seq = (ca * cb - sa * sb).reshape(n_outer * GRID_W, seq)[:n_rows].astype(BF16)
    s_seq = (sa * cb + ca * sb).reshape(n_outer * GRID_W, seq)[:n_rows].astype(BF16)
    c_ch, s_ch = cos_sin(HEAD_DIM)
    eye = jnp.eye(WIDTH_C // HEAD_DIM, dtype=F32)
    ch_tab = jnp.concatenate([jnp.kron(eye, c_ch), jnp.kron(eye, s_ch)], axis=1).astype(BF16)
    i = lax.broadcasted_iota(jnp.int32, (N_CHUNK, N_CHUNK + DFT_TAIL), 0)
    k = lax.broadcasted_iota(jnp.int32, (N_CHUNK, N_CHUNK + DFT_TAIL), 1)
    reversal = (k == N_CHUNK - i).astype(BF16)
    return c_seq, s_seq, ch_tab, reversal


def _gate_fourier_kernel(z_ref, x_ref, wcat_ref, bs_ref, gb_ref,
                         cos_ref, sin_ref, ch_tab_ref, rev_ref, gc_ref,
                         ob_ref, oc_ref, hi_ref, *, seq):
    gate_args = (z_ref, wcat_ref, bs_ref, gb_ref, ob_ref)
    n_gate = seq // CHUNK
    n_chunks = seq // 2 // N_CHUNK
    gate_done = 0

    def gate_until(n):
        nonlocal gate_done
        for c in range(gate_done, min(n, n_gate)):
            _gate_chunk(*gate_args, c * CHUNK)
        gate_done = max(gate_done, min(n, n_gate))

    xcs = jnp.dot(x_ref[...], ch_tab_ref[...], preferred_element_type=F32)
    xc = xcs[:, :WIDTH_C].astype(BF16)
    xs = xcs[:, WIDTH_C:].astype(BF16)
    scale = 1.0 / math.sqrt(seq * HEAD_DIM)
    g = gc_ref[...]
    per_step = -(-n_gate // (2 * n_chunks))
    for t in range(n_chunks):
        r0 = t * N_CHUNK
        m = N_CHUNK + (DFT_TAIL if t == n_chunks - 1 else 0)
        a = jnp.dot(cos_ref[r0:r0 + m, :], xc, preferred_element_type=F32)
        b = jnp.dot(sin_ref[r0:r0 + m, :], xs, preferred_element_type=F32)
        lo = (a[:N_CHUNK] - b[:N_CHUNK]) * scale
        oc_ref[r0:r0 + N_CHUNK, :] = _head_norm_lane_pairs(lo, g).astype(oc_ref.dtype)
        hi_ref[r0:r0 + m, :] = _head_norm_lane_pairs((a + b) * scale, g).astype(hi_ref.dtype)
        gate_until((t + 1) * per_step)
    for u in range(n_chunks):
        src = hi_ref[u * N_CHUNK:(u + 1) * N_CHUNK + DFT_TAIL, :]
        blk = jnp.dot(rev_ref[...], src, preferred_element_type=F32)
        oc_ref[seq - (u + 1) * N_CHUNK:seq - u * N_CHUNK, :] = blk.astype(oc_ref.dtype)
        gate_until((n_chunks + u + 1) * per_step)
    gate_until(n_gate)


def _gate_fourier(proj, w_s, b_s, tables, g, batch, seq):
    z_off = 3 * WIDTH_A // (2 * WIDTH_B)
    c_off = (3 * WIDTH_A + 2 * WIDTH_B) // WIDTH_C
    wcat = jnp.transpose(w_s, (1, 0, 2)).reshape(CHUNK, N_HEADS_B * CHUNK).astype(BF16)
    bs_full = jnp.repeat(b_s.T.astype(F32), HEAD_DIM, axis=1)
    g_row = g.reshape(1, D_MODEL)
    g_b = g_row[:, WIDTH_A:WIDTH_A + WIDTH_B]
    g_c = g_row[:, WIDTH_A + WIDTH_B:]
    n_resident = 3 + len(tables) + 1
    return pl.pallas_call(
        functools.partial(_gate_fourier_kernel, seq=seq),
        grid=(batch,),
        in_specs=[
            pl.BlockSpec((seq, 2 * WIDTH_B), lambda b: (b, z_off)),
            pl.BlockSpec((seq, WIDTH_C), lambda b: (b, c_off)),
        ] + [_resident()] * n_resident,
        out_specs=[
            pl.BlockSpec((seq, WIDTH_B), lambda b: (b, 0)),
            pl.BlockSpec((seq, WIDTH_C), lambda b: (b, 0)),
        ],
        out_shape=[
            jax.ShapeDtypeStruct((batch * seq, WIDTH_B), BF16),
            jax.ShapeDtypeStruct((batch * seq, WIDTH_C), BF16),
        ],
        scratch_shapes=[pltpu.VMEM((seq // 2 + DFT_TAIL, WIDTH_C), BF16)],
        compiler_params=_compiler_params(1),
        name="gate_fourier",
    )(proj, proj, wcat, bs_full, g_b, *tables, g_c)


def _ffn_kernel(ma_ref, mb_ref, mc_ref, x_ref, wo_ref, gf_ref, wg_ref, wu_ref, wd_ref, gfin_ref,
                o_ref, h_ref, act_ref, *, final):
    x1_ref = o_ref
    mix = jnp.concatenate([ma_ref[...], mb_ref[...], mc_ref[...]], axis=-1)
    for c in range(D_MODEL // N_CHUNK):
        cols = slice(c * N_CHUNK, (c + 1) * N_CHUNK)
        x1_ref[:, cols] = x_ref[:, cols] + jnp.dot(mix, wo_ref[:, cols], preferred_element_type=F32)
    h_ref[...] = _rms_norm_rows(x1_ref[...], gf_ref[...]).astype(BF16)
    for c in range(D_FF // N_CHUNK):
        cols = slice(c * N_CHUNK, (c + 1) * N_CHUNK)
        gate = jnp.dot(h_ref[...], wg_ref[:, cols], preferred_element_type=F32)
        up = jnp.dot(h_ref[...], wu_ref[:, cols], preferred_element_type=F32)
        act_ref[:, cols] = (gate * jax.nn.sigmoid(gate) * up).astype(BF16)
    for c in range(D_MODEL // N_CHUNK):
        cols = slice(c * N_CHUNK, (c + 1) * N_CHUNK)
        x1_ref[:, cols] = x1_ref[:, cols] + jnp.dot(act_ref[...], wd_ref[:, cols],
                                                    preferred_element_type=F32)
    if final:
        o_ref[...] = _rms_norm_rows(x1_ref[...], gfin_ref[...])


def _out_proj_ffn(mix_a, mix_b, mix_c, x2d, w_out, g_ffn, w_gate, w_up, w_down, g_final, layer,
                  final):
    n = x2d.shape[0]
    row = lambda width: pl.BlockSpec((FFN_TM, width), lambda i: (i, 0))
    slab = lambda w: _layer_slab(w, layer, 1)
    return pl.pallas_call(
        functools.partial(_ffn_kernel, final=final),
        grid=(n // FFN_TM,),
        in_specs=[row(WIDTH_A), row(WIDTH_B), row(WIDTH_C), row(D_MODEL),
                  slab(w_out), _resident(), slab(w_gate), slab(w_up), slab(w_down), _resident()],
        out_specs=row(D_MODEL),
        out_shape=jax.ShapeDtypeStruct((n, D_MODEL), F32),
        scratch_shapes=[
            pltpu.VMEM((FFN_TM, D_MODEL), BF16),
            pltpu.VMEM((FFN_TM, D_FF), BF16),
        ],
        compiler_params=_compiler_params(1),
        name="out_proj_ffn",
    )(mix_a, mix_b, mix_c, x2d, w_out, g_ffn.reshape(1, D_MODEL), w_gate, w_up, w_down,
      g_final.reshape(1, D_MODEL))


def kernel(x, norm_mix_g, w_in, rpb, gmlp_ln_g, gmlp_ln_b, w_spatial, b_spatial, head_norm_g, w_out,
           norm_ffn_g, w_gate, w_up, w_down, final_norm_g):
    batch, seq, d_model = x.shape
    assert d_model == D_MODEL and seq % (2 * N_CHUNK) == 0 and seq // GRID_W >= NA_ROWS
    depth = w_in.shape[0]
    dft_tables = _dft_tables(seq)
    bias_tab = _attention_bias_table(rpb)
    w_in, w_out, w_gate, w_up, w_down = (w.astype(BF16) for w in (w_in, w_out, w_gate, w_up, w_down))
    x2d = x.reshape(batch * seq, D_MODEL)
    for l in range(depth):
        proj = _in_proj(x2d, norm_mix_g[l], w_in, gmlp_ln_g[l], gmlp_ln_b[l], l)
        mix_a = _attention(proj, bias_tab, head_norm_g[l], l, batch, seq)
        mix_b, mix_c = _gate_fourier(proj, w_spatial[l], b_spatial[l], dft_tables, head_norm_g[l],
                                     batch, seq)
        x2d = _out_proj_ffn(mix_a, mix_b, mix_c, x2d, w_out, norm_ffn_g[l], w_gate, w_up, w_down,
                            final_norm_g, l, final=(l == depth - 1))
    return x2d.reshape(batch, seq, D_MODEL)
```

```python
import functools
import math

import numpy as np
import jax
import jax.numpy as jnp
from jax import lax
from jax.experimental import pallas as pl
from jax.experimental.pallas import tpu as pltpu

F32 = jnp.float32
BF16 = jnp.bfloat16

D_MODEL = 1024
GRID_W = 64
HEAD_DIM = 64
WIDTH_A = D_MODEL // 2
WIDTH_B = D_MODEL // 4
WIDTH_C = D_MODEL // 4
N_HEADS_A = WIDTH_A // HEAD_DIM
N_HEADS_B = WIDTH_B // HEAD_DIM
IN_WIDTH = 3 * WIDTH_A + 2 * WIDTH_B + WIDTH_C
NA_ROWS = 8
NA_COLS = 16
CHUNK = 128
D_FF = -(-8 * D_MODEL // (3 * 256)) * 256
EPS = 1e-6
NEG_INF = -1e9

LANES = 128
VMEM_LIMIT_BYTES = 56 * 1024 * 1024

HEADS_PER_STEP = 4
KEY_SLAB = NA_ROWS * GRID_W
ATTN_GROUP = 16
LOG2_E = math.log2(math.e)
Q_SCALE = HEAD_DIM ** -0.5 * LOG2_E

IN_TM = 1024
IN_RB = 512
FFN_TM = 1024
N_CHUNK = 256
DFT_TAIL = 16


def _compiler_params(n_grid_dims):
    return pltpu.CompilerParams(
        dimension_semantics=("arbitrary",) * n_grid_dims,
        vmem_limit_bytes=VMEM_LIMIT_BYTES,
    )


def _resident():
    return pl.BlockSpec(memory_space=pltpu.VMEM)


def _layer_slab(stacked, layer, n_grid_dims):
    zeros = (0,) * (stacked.ndim - 1)
    index_map = {1: lambda i: (layer,) + zeros, 2: lambda i, j: (layer,) + zeros}[n_grid_dims]
    return pl.BlockSpec((None,) + stacked.shape[1:], index_map, pipeline_mode=pl.Buffered(1))


def _rms_norm_rows(x, g):
    ms = jnp.mean(x * x, axis=-1, keepdims=True)
    return x * lax.rsqrt(ms + EPS) * g


def _head_norm_lane_pairs(y, g):
    n_tiles = y.shape[-1] // LANES
    lane = lax.broadcasted_iota(jnp.int32, (1, LANES), 1)
    first = lane < HEAD_DIM
    outs = []
    for t in range(n_tiles):
        yt = y[:, t * LANES:(t + 1) * LANES]
        sq = yt * yt
        ss_a = jnp.sum(jnp.where(first, sq, 0.0), axis=-1, keepdims=True)
        ss_b = jnp.sum(jnp.where(first, 0.0, sq), axis=-1, keepdims=True)
        ss = jnp.where(first, ss_a, ss_b)
        gain = g[:, t * LANES:(t + 1) * LANES] * math.sqrt(HEAD_DIM)
        outs.append(yt * lax.rsqrt(ss + HEAD_DIM * EPS) * gain)
    return outs[0] if n_tiles == 1 else jnp.concatenate(outs, axis=-1)


def _in_proj_kernel(x_ref, g_ref, w_ref, lng_ref, lnb_ref, o_ref, h_ref):
    assert N_CHUNK == WIDTH_B
    u_chunk = 3 * WIDTH_A // N_CHUNK
    v_chunk = u_chunk + 1
    others = [c for c in range(IN_WIDTH // N_CHUNK) if c not in (u_chunk, v_chunk)]
    for rb in range(IN_TM // IN_RB):
        rows = slice(rb * IN_RB, (rb + 1) * IN_RB)
        h_ref[rows, :] = _rms_norm_rows(x_ref[rows, :], g_ref[...]).astype(BF16)
        for c in [v_chunk, u_chunk] + others:
            cols = slice(c * N_CHUNK, (c + 1) * N_CHUNK)
            acc = jnp.dot(h_ref[rows, :], w_ref[:, cols], preferred_element_type=F32)
            if (c + 1) * N_CHUNK <= WIDTH_A:
                acc = acc * Q_SCALE
            elif c in (u_chunk, v_chunk):
                acc = _gelu_tanh(acc)
                if c == v_chunk:
                    mu = jnp.mean(acc, axis=-1, keepdims=True)
                    d = acc - mu
                    var = jnp.mean(d * d, axis=-1, keepdims=True)
                    acc = d * lax.rsqrt(var + EPS) * lng_ref[...] + lnb_ref[...]
            o_ref[rows, cols] = acc.astype(o_ref.dtype)


def _in_proj(x2d, g, w_stack, ln_g, ln_b, layer):
    n = x2d.shape[0]
    return pl.pallas_call(
        _in_proj_kernel,
        grid=(n // IN_TM,),
        in_specs=[
            pl.BlockSpec((IN_TM, D_MODEL), lambda i: (i, 0)),
            _resident(),
            _layer_slab(w_stack, layer, 1),
            _resident(),
            _resident(),
        ],
        out_specs=pl.BlockSpec((IN_TM, IN_WIDTH), lambda i: (i, 0)),
        out_shape=jax.ShapeDtypeStruct((n, IN_WIDTH), BF16),
        scratch_shapes=[pltpu.VMEM((IN_TM, D_MODEL), BF16)],
        compiler_params=_compiler_params(1),
        name="in_proj",
    )(x2d, g.reshape(1, D_MODEL), w_stack, ln_g.reshape(1, WIDTH_B), ln_b.reshape(1, WIDTH_B))


def _attn_kernel(q_ref, k_ref, v_ref, bias_ref, g_ref, o_ref, p_ref, l_ref, *, n_rows):
    width = HEADS_PER_STEP * HEAD_DIM
    head_of_lane = lax.broadcasted_iota(jnp.int32, (1, width), 1) // HEAD_DIM
    g = g_ref[...]
    n_groups = n_rows // ATTN_GROUP

    def rows_of(group):
        for j in range(ATTN_GROUP):
            r = group * ATTN_GROUP + j
            rs = jnp.clip(r - NA_ROWS // 2, 0, n_rows - NA_ROWS)
            yield j, r, rs, pl.multiple_of(r * GRID_W, GRID_W), pl.multiple_of(rs * GRID_W, GRID_W)

    def probs(group, slot):
        for j, r, rs, q0, k0 in rows_of(group):
            q = q_ref[pl.ds(q0, GRID_W), :]
            zero = jnp.zeros_like(q)
            qm = jnp.concatenate(
                [jnp.where(head_of_lane == h, q, zero) for h in range(HEADS_PER_STEP)], axis=0)
            s = lax.dot_general(qm, k_ref[pl.ds(k0, KEY_SLAB), :], (((1,), (1,)), ((), ())),
                                preferred_element_type=F32)
            a0 = NA_ROWS - 1 - (r - rs)
            s = s + jnp.concatenate(
                [jnp.concatenate([bias_ref[h, a0 + 2 * i] for i in range(NA_ROWS // 2)], axis=-1)
                 for h in range(HEADS_PER_STEP)], axis=0)
            m = jnp.max(s, axis=-1, keepdims=True)
            p = jnp.exp2(s - m)
            l = jnp.sum(p, axis=-1, keepdims=True)
            l_ref[slot, j] = jnp.broadcast_to(l, (l.shape[0], LANES))
            p_ref[slot, j] = p.astype(BF16)

    def outputs(group, slot):
        for j, _, _, q0, k0 in rows_of(group):
            o4 = jnp.dot(p_ref[slot, j], v_ref[pl.ds(k0, KEY_SLAB), :], preferred_element_type=F32)
            inv_l = 1.0 / l_ref[slot, j]
            o4 = o4 * jnp.concatenate([inv_l] * (width // LANES), axis=-1)
            o = o4[:GRID_W]
            for h in range(1, HEADS_PER_STEP):
                o = jnp.where(head_of_lane == h, o4[h * GRID_W:(h + 1) * GRID_W], o)
            o_ref[pl.ds(q0, GRID_W), :] = _head_norm_lane_pairs(o, g).astype(o_ref.dtype)

    def two_groups(i, carry):
        for slot in range(2):
            probs(2 * i + slot, slot)
            outputs(2 * i + slot, slot)
        return carry

    lax.fori_loop(0, n_groups // 2, two_groups, 0)


def _attention(proj, bias_tab, g, layer, batch, seq):
    width = HEADS_PER_STEP * HEAD_DIM
    n_groups = WIDTH_A // width
    k_off = WIDTH_A // width
    v_off = 2 * WIDTH_A // width
    return pl.pallas_call(
        functools.partial(_attn_kernel, n_rows=seq // GRID_W),
        grid=(n_groups, batch),
        in_specs=[
            pl.BlockSpec((seq, width), lambda hg, b: (b, hg)),
            pl.BlockSpec((seq, width), lambda hg, b: (b, k_off + hg)),
            pl.BlockSpec((seq, width), lambda hg, b: (b, v_off + hg)),
            pl.BlockSpec((None, HEADS_PER_STEP) + bias_tab.shape[2:],
                         lambda hg, b: (layer, hg, 0, 0, 0)),
            pl.BlockSpec((1, width), lambda hg, b: (0, hg)),
        ],
        out_specs=pl.BlockSpec((seq, width), lambda hg, b: (b, hg)),
        out_shape=jax.ShapeDtypeStruct((batch * seq, WIDTH_A), BF16),
        scratch_shapes=[
            pltpu.VMEM((2, ATTN_GROUP, HEADS_PER_STEP * GRID_W, KEY_SLAB), BF16),
            pltpu.VMEM((2, ATTN_GROUP, HEADS_PER_STEP * GRID_W, LANES), F32),
        ],
        compiler_params=_compiler_params(2),
        name="nbr_attention",
    )(proj, proj, proj, bias_tab, g.reshape(1, D_MODEL)[:, :WIDTH_A])


def _attention_bias_table(rpb):
    qc = np.arange(GRID_W)[:, None]
    kc = np.arange(GRID_W)[None, :]
    win = np.clip(qc - NA_COLS // 2, 0, GRID_W - NA_COLS)
    col_ok = (kc >= win) & (kc < win + NA_COLS)
    n_rel = rpb.shape[-1]
    onehot = (kc - qc + NA_COLS - 1)[None] == np.arange(n_rel)[:, None, None]
    zeros = np.zeros_like(onehot)
    onehot2 = np.concatenate([np.concatenate([onehot, zeros], axis=-1),
                              np.concatenate([zeros, onehot], axis=-1)], axis=0)
    rows2 = jnp.concatenate([rpb[:, :, :-1], rpb[:, :, 1:]], axis=-1).astype(F32)
    tiles = jnp.einsum("lhab,bqk->lhaqk", rows2, jnp.asarray(onehot2, F32),
                       precision=lax.Precision.HIGHEST)
    return jnp.where(jnp.asarray(np.tile(col_ok, (1, 2))), tiles, NEG_INF) * LOG2_E


def _gelu_tanh(x):
    c2 = -2.0 * math.sqrt(2.0 / math.pi) * LOG2_E
    return x / (1.0 + jnp.exp2(x * (c2 + (c2 * 0.044715) * (x * x))))


def _gate_chunk(z_ref, wcat_ref, bs_ref, g_ref, o_ref, t0):
    head_of_lane = lax.broadcasted_iota(jnp.int32, (1, WIDTH_B), 1) // HEAD_DIM
    u = z_ref[pl.ds(t0, CHUNK), :WIDTH_B].astype(F32)
    vn = z_ref[pl.ds(t0, CHUNK), WIDTH_B:]
    stacked = jnp.concatenate(
        [jnp.where(head_of_lane == h, vn, jnp.zeros_like(vn)) for h in range(N_HEADS_B)], axis=0)
    mixed = jnp.dot(wcat_ref[...], stacked, preferred_element_type=F32) + bs_ref[...]
    o_ref[pl.ds(t0, CHUNK), :] = _head_norm_lane_pairs(u * mixed, g_ref[...]).astype(o_ref.dtype)


def _dft_tables(seq):
    def cos_sin(n):
        i = lax.broadcasted_iota(jnp.int32, (n, n), 0)
        j = lax.broadcasted_iota(jnp.int32, (n, n), 1)
        ang = ((i * j) % n).astype(F32) * (2.0 * math.pi / n)
        return jnp.cos(ang), jnp.sin(ang)

    def cos_sin_rows(row_step, n_rows):
        i = lax.broadcasted_iota(jnp.int32, (n_rows, seq), 0) * row_step
        j = lax.broadcasted_iota(jnp.int32, (n_rows, seq), 1)
        ang = ((i * j) % seq).astype(F32) * (2.0 * math.pi / seq)
        return jnp.cos(ang), jnp.sin(ang)

    n_rows = seq // 2 + DFT_TAIL
    n_outer = -(-n_rows // GRID_W)
    ca, sa = (t[:, None, :] for t in cos_sin_rows(GRID_W, n_outer))
    cb, sb = (t[None, :, :] for t in cos_sin_rows(1, GRID_W))
    c_seq = (ca * cb - sa * sb).reshape(n_outer * GRID_W, seq)[:n_rows].astype(BF16)
    s_seq = (sa * cb + ca * sb).reshape(n_outer * GRID_W, seq)[:n_rows].astype(BF16)
    c_ch, s_ch = cos_sin(HEAD_DIM)
    eye = jnp.eye(WIDTH_C // HEAD_DIM, dtype=F32)
    ch_tab = jnp.concatenate([jnp.kron(eye, c_ch), jnp.kron(eye, s_ch)], axis=1).astype(BF16)
    i = lax.broadcasted_iota(jnp.int32, (N_CHUNK, N_CHUNK + DFT_TAIL), 0)
    k = lax.broadcasted_iota(jnp.int32, (N_CHUNK, N_CHUNK + DFT_TAIL), 1)
    reversal = (k == N_CHUNK - i).astype(BF16)
    return c_seq, s_seq, ch_tab, reversal


def _gate_fourier_kernel(z_ref, x_ref, wcat_ref, bs_ref, gb_ref,
                         cos_ref, sin_ref, ch_tab_ref, rev_ref, gc_ref,
                         ob_ref, oc_ref, hi_ref, *, seq):
    gate_args = (z_ref, wcat_ref, bs_ref, gb_ref, ob_ref)
    n_gate = seq // CHUNK
    n_chunks = seq // 2 // N_CHUNK
    gate_done = 0

    def gate_until(n):
        nonlocal gate_done
        for c in range(gate_done, min(n, n_gate)):
            _gate_chunk(*gate_args, c * CHUNK)
        gate_done = max(gate_done, min(n, n_gate))

    xcs = jnp.dot(x_ref[...], ch_tab_ref[...], preferred_element_type=F32)
    xc = xcs[:, :WIDTH_C].astype(BF16)
    xs = xcs[:, WIDTH_C:].astype(BF16)
    scale = 1.0 / math.sqrt(seq * HEAD_DIM)
    g = gc_ref[...]
    per_step = -(-n_gate // (2 * n_chunks))
    for t in range(n_chunks):
        r0 = t * N_CHUNK
        m = N_CHUNK + (DFT_TAIL if t == n_chunks - 1 else 0)
        a = jnp.dot(cos_ref[r0:r0 + m, :], xc, preferred_element_type=F32)
        b = jnp.dot(sin_ref[r0:r0 + m, :], xs, preferred_element_type=F32)
        lo = (a[:N_CHUNK] - b[:N_CHUNK]) * scale
        oc_ref[r0:r0 + N_CHUNK, :] = _head_norm_lane_pairs(lo, g).astype(oc_ref.dtype)
        hi_ref[r0:r0 + m, :] = _head_norm_lane_pairs((a + b) * scale, g).astype(hi_ref.dtype)
        gate_until((t + 1) * per_step)
    for u in range(n_chunks):
        src = hi_ref[u * N_CHUNK:(u + 1) * N_CHUNK + DFT_TAIL, :]
        blk = jnp.dot(rev_ref[...], src, preferred_element_type=F32)
        oc_ref[seq - (u + 1) * N_CHUNK:seq - u * N_CHUNK, :] = blk.astype(oc_ref.dtype)
        gate_until((n_chunks + u + 1) * per_step)
    gate_until(n_gate)


def _gate_fourier(proj, w_s, b_s, tables, g, batch, seq):
    z_off = 3 * WIDTH_A // (2 * WIDTH_B)
    c_off = (3 * WIDTH_A + 2 * WIDTH_B) // WIDTH_C
    wcat = jnp.transpose(w_s, (1, 0, 2)).reshape(CHUNK, N_HEADS_B * CHUNK).astype(BF16)
    bs_full = jnp.repeat(b_s.T.astype(F32), HEAD_DIM, axis=1)
    g_row = g.reshape(1, D_MODEL)
    g_b = g_row[:, WIDTH_A:WIDTH_A + WIDTH_B]
    g_c = g_row[:, WIDTH_A + WIDTH_B:]
    n_resident = 3 + len(tables) + 1
    return pl.pallas_call(
        functools.partial(_gate_fourier_kernel, seq=seq),
        grid=(batch,),
        in_specs=[
            pl.BlockSpec((seq, 2 * WIDTH_B), lambda b: (b, z_off)),
            pl.BlockSpec((seq, WIDTH_C), lambda b: (b, c_off)),
        ] + [_resident()] * n_resident,
        out_specs=[
            pl.BlockSpec((seq, WIDTH_B), lambda b: (b, 0)),
            pl.BlockSpec((seq, WIDTH_C), lambda b: (b, 0)),
        ],
        out_shape=[
            jax.ShapeDtypeStruct((batch * seq, WIDTH_B), BF16),
            jax.ShapeDtypeStruct((batch * seq, WIDTH_C), BF16),
        ],
        scratch_shapes=[pltpu.VMEM((seq // 2 + DFT_TAIL, WIDTH_C), BF16)],
        compiler_params=_compiler_params(1),
        name="gate_fourier",
    )(proj, proj, wcat, bs_full, g_b, *tables, g_c)


def _ffn_kernel(ma_ref, mb_ref, mc_ref, x_ref, wo_ref, gf_ref, wg_ref, wu_ref, wd_ref, gfin_ref,
                o_ref, h_ref, act_ref, *, final):
    x1_ref = o_ref
    mix = jnp.concatenate([ma_ref[...], mb_ref[...], mc_ref[...]], axis=-1)
    for c in range(D_MODEL // N_CHUNK):
        cols = slice(c * N_CHUNK, (c + 1) * N_CHUNK)
        x1_ref[:, cols] = x_ref[:, cols] + jnp.dot(mix, wo_ref[:, cols], preferred_element_type=F32)
    h_ref[...] = _rms_norm_rows(x1_ref[...], gf_ref[...]).astype(BF16)
    for c in range(D_FF // N_CHUNK):
        cols = slice(c * N_CHUNK, (c + 1) * N_CHUNK)
        gate = jnp.dot(h_ref[...], wg_ref[:, cols], preferred_element_type=F32)
        up = jnp.dot(h_ref[...], wu_ref[:, cols], preferred_element_type=F32)
        act_ref[:, cols] = (gate * jax.nn.sigmoid(gate) * up).astype(BF16)
    for c in range(D_MODEL // N_CHUNK):
        cols = slice(c * N_CHUNK, (c + 1) * N_CHUNK)
        x1_ref[:, cols] = x1_ref[:, cols] + jnp.dot(act_ref[...], wd_ref[:, cols],
                                                    preferred_element_type=F32)
    if final:
        o_ref[...] = _rms_norm_rows(x1_ref[...], gfin_ref[...])


def _out_proj_ffn(mix_a, mix_b, mix_c, x2d, w_out, g_ffn, w_gate, w_up, w_down, g_final, layer,
                  final):
    n = x2d.shape[0]
    row = lambda width: pl.BlockSpec((FFN_TM, width), lambda i: (i, 0))
    slab = lambda w: _layer_slab(w, layer, 1)
    return pl.pallas_call(
        functools.partial(_ffn_kernel, final=final),
        grid=(n // FFN_TM,),
        in_specs=[row(WIDTH_A), row(WIDTH_B), row(WIDTH_C), row(D_MODEL),
                  slab(w_out), _resident(), slab(w_gate), slab(w_up), slab(w_down), _resident()],
        out_specs=row(D_MODEL),
        out_shape=jax.ShapeDtypeStruct((n, D_MODEL), F32),
        scratch_shapes=[
            pltpu.VMEM((FFN_TM, D_MODEL), BF16),
            pltpu.VMEM((FFN_TM, D_FF), BF16),
        ],
        compiler_params=_compiler_params(1),
        name="out_proj_ffn",
    )(mix_a, mix_b, mix_c, x2d, w_out, g_ffn.reshape(1, D_MODEL), w_gate, w_up, w_down,
      g_final.reshape(1, D_MODEL))


def kernel(x, norm_mix_g, w_in, rpb, gmlp_ln_g, gmlp_ln_b, w_spatial, b_spatial, head_norm_g, w_out,
           norm_ffn_g, w_gate, w_up, w_down, final_norm_g):
    batch, seq, d_model = x.shape
    assert d_model == D_MODEL and seq % (2 * N_CHUNK) == 0 and seq // GRID_W >= NA_ROWS
    depth = w_in.shape[0]
    dft_tables = _dft_tables(seq)
    bias_tab = _attention_bias_table(rpb)
    w_in, w_out, w_gate, w_up, w_down = (w.astype(BF16) for w in (w_in, w_out, w_gate, w_up, w_down))
    x2d = x.reshape(batch * seq, D_MODEL)
    for l in range(depth):
        proj = _in_proj(x2d, norm_mix_g[l], w_in, gmlp_ln_g[l], gmlp_ln_b[l], l)
        mix_a = _attention(proj, bias_tab, head_norm_g[l], l, batch, seq)
        mix_b, mix_c = _gate_fourier(proj, w_spatial[l], b_spatial[l], dft_tables, head_norm_g[l],
                                     batch, seq)
        x2d = _out_proj_ffn(mix_a, mix_b, mix_c, x2d, w_out, norm_ffn_g[l], w_gate, w_up, w_down,
                            final_norm_g, l, final=(l == depth - 1))
    return x2d.reshape(batch, seq, D_MODEL)
```

```python
import functools
import math

import numpy as np
import jax
import jax.numpy as jnp
from jax import lax
from jax.experimental import pallas as pl
from jax.experimental.pallas import tpu as pltpu

F32 = jnp.float32
BF16 = jnp.bfloat16

D_MODEL = 1024
GRID_W = 64
HEAD_DIM = 64
WIDTH_A = D_MODEL // 2
WIDTH_B = D_MODEL // 4
WIDTH_C = D_MODEL // 4
N_HEADS_A = WIDTH_A // HEAD_DIM
N_HEADS_B = WIDTH_B // HEAD_DIM
IN_WIDTH = 3 * WIDTH_A + 2 * WIDTH_B + WIDTH_C
NA_ROWS = 8
NA_COLS = 16
CHUNK = 128
D_FF = -(-8 * D_MODEL // (3 * 256)) * 256
EPS = 1e-6
NEG_INF = -1e9

LANES = 128
VMEM_LIMIT_BYTES = 56 * 1024 * 1024

HEADS_PER_STEP = 4
KEY_SLAB = NA_ROWS * GRID_W
ATTN_GROUP = 16
LOG2_E = math.log2(math.e)
Q_SCALE = HEAD_DIM ** -0.5 * LOG2_E

IN_TM = 1024
IN_RB = 512
FFN_TM = 1024
N_CHUNK = 256
DFT_TAIL = 16


def _compiler_params(n_grid_dims):
    return pltpu.CompilerParams(
        dimension_semantics=("arbitrary",) * n_grid_dims,
        vmem_limit_bytes=VMEM_LIMIT_BYTES,
    )


def _resident():
    return pl.BlockSpec(memory_space=pltpu.VMEM)


def _layer_slab(stacked, layer, n_grid_dims):
    zeros = (0,) * (stacked.ndim - 1)
    index_map = {1: lambda i: (layer,) + zeros, 2: lambda i, j: (layer,) + zeros}[n_grid_dims]
    return pl.BlockSpec((None,) + stacked.shape[1:], index_map, pipeline_mode=pl.Buffered(1))


def _rms_norm_rows(x, g):
    ms = jnp.mean(x * x, axis=-1, keepdims=True)
    return x * lax.rsqrt(ms + EPS) * g


def _head_norm_lane_pairs(y, g):
    n_tiles = y.shape[-1] // LANES
    lane = lax.broadcasted_iota(jnp.int32, (1, LANES), 1)
    first = lane < HEAD_DIM
    outs = []
    for t in range(n_tiles):
        yt = y[:, t * LANES:(t + 1) * LANES]
        sq = yt * yt
        ss_a = jnp.sum(jnp.where(first, sq, 0.0), axis=-1, keepdims=True)
        ss_b = jnp.sum(jnp.where(first, 0.0, sq), axis=-1, keepdims=True)
        ss = jnp.where(first, ss_a, ss_b)
        gain = g[:, t * LANES:(t + 1) * LANES] * math.sqrt(HEAD_DIM)
        outs.append(yt * lax.rsqrt(ss + HEAD_DIM * EPS) * gain)
    return outs[0] if n_tiles == 1 else jnp.concatenate(outs, axis=-1)


def _in_proj_kernel(x_ref, g_ref, w_ref, lng_ref, lnb_ref, o_ref, h_ref):
    assert N_CHUNK == WIDTH_B
    u_chunk = 3 * WIDTH_A // N_CHUNK
    v_chunk = u_chunk + 1
    others = [c for c in range(IN_WIDTH // N_CHUNK) if c not in (u_chunk, v_chunk)]
    for rb in range(IN_TM // IN_RB):
        rows = slice(rb * IN_RB, (rb + 1) * IN_RB)
        h_ref[rows, :] = _rms_norm_rows(x_ref[rows, :], g_ref[...]).astype(BF16)
        for c in [v_chunk, u_chunk] + others:
            cols = slice(c * N_CHUNK, (c + 1) * N_CHUNK)
            acc = jnp.dot(h_ref[rows, :], w_ref[:, cols], preferred_element_type=F32)
            if (c + 1) * N_CHUNK <= WIDTH_A:
                acc = acc * Q_SCALE
            elif c in (u_chunk, v_chunk):
                acc = _gelu_tanh(acc)
                if c == v_chunk:
                    mu = jnp.mean(acc, axis=-1, keepdims=True)
                    d = acc - mu
                    var = jnp.mean(d * d, axis=-1, keepdims=True)
                    acc = d * lax.rsqrt(var + EPS) * lng_ref[...] + lnb_ref[...]
            o_ref[rows, cols] = acc.astype(o_ref.dtype)


def _in_proj(x2d, g, w_stack, ln_g, ln_b, layer):
    n = x2d.shape[0]
    return pl.pallas_call(
        _in_proj_kernel,
        grid=(n // IN_TM,),
        in_specs=[
            pl.BlockSpec((IN_TM, D_MODEL), lambda i: (i, 0)),
            _resident(),
            _layer_slab(w_stack, layer, 1),
            _resident(),
            _resident(),
        ],
        out_specs=pl.BlockSpec((IN_TM, IN_WIDTH), lambda i: (i, 0)),
        out_shape=jax.ShapeDtypeStruct((n, IN_WIDTH), BF16),
        scratch_shapes=[pltpu.VMEM((IN_TM, D_MODEL), BF16)],
        compiler_params=_compiler_params(1),
        name="in_proj",
    )(x2d, g.reshape(1, D_MODEL), w_stack, ln_g.reshape(1, WIDTH_B), ln_b.reshape(1, WIDTH_B))


def _attn_kernel(q_ref, k_ref, v_ref, bias_ref, g_ref, o_ref, p_ref, l_ref, *, n_rows):
    width = HEADS_PER_STEP * HEAD_DIM
    for pas in range(WIDTH_A // width):
        lanes = pl.ds(pas * width, width)
        _attention_pass(q_ref.at[:, lanes], k_ref.at[:, lanes], v_ref.at[:, lanes],
                        bias_ref.at[pl.ds(pas * HEADS_PER_STEP, HEADS_PER_STEP)], g_ref[:, lanes],
                        o_ref.at[:, lanes], p_ref.at[pas], l_ref.at[pas], n_rows)


def _attention_pass(q_ref, k_ref, v_ref, bias_ref, g, o_ref, p_ref, l_ref, n_rows):
    width = HEADS_PER_STEP * HEAD_DIM
    head_of_lane = lax.broadcasted_iota(jnp.int32, (1, width), 1) // HEAD_DIM
    n_groups = n_rows // ATTN_GROUP

    def rows_of(group):
        for j in range(ATTN_GROUP):
            r = group * ATTN_GROUP + j
            rs = jnp.clip(r - NA_ROWS // 2, 0, n_rows - NA_ROWS)
            yield j, r, rs, pl.multiple_of(r * GRID_W, GRID_W), pl.multiple_of(rs * GRID_W, GRID_W)

    def probs(group, slot):
        for j, r, rs, q0, k0 in rows_of(group):
            q = q_ref[pl.ds(q0, GRID_W), :]
            zero = jnp.zeros_like(q)
            qm = jnp.concatenate(
                [jnp.where(head_of_lane == h, q, zero) for h in range(HEADS_PER_STEP)], axis=0)
            s = lax.dot_general(qm, k_ref[pl.ds(k0, KEY_SLAB), :], (((1,), (1,)), ((), ())),
                                preferred_element_type=F32)
            a0 = NA_ROWS - 1 - (r - rs)
            s = s + jnp.concatenate(
                [jnp.concatenate([bias_ref[h, a0 + 2 * i] for i in range(NA_ROWS // 2)], axis=-1)
                 for h in range(HEADS_PER_STEP)], axis=0)
            m = jnp.max(s, axis=-1, keepdims=True)
            p = jnp.exp2(s - m)
            l = jnp.sum(p, axis=-1, keepdims=True)
            l_ref[slot, j] = jnp.broadcast_to(l, (l.shape[0], LANES))
            p_ref[slot, j] = p.astype(BF16)

    def outputs(group, slot):
        for j, _, _, q0, k0 in rows_of(group):
            o4 = jnp.dot(p_ref[slot, j], v_ref[pl.ds(k0, KEY_SLAB), :], preferred_element_type=F32)
            inv_l = 1.0 / l_ref[slot, j]
            o4 = o4 * jnp.concatenate([inv_l] * (width // LANES), axis=-1)
            o = o4[:GRID_W]
            for h in range(1, HEADS_PER_STEP):
                o = jnp.where(head_of_lane == h, o4[h * GRID_W:(h + 1) * GRID_W], o)
            o_ref[pl.ds(q0, GRID_W), :] = _head_norm_lane_pairs(o, g).astype(o_ref.dtype)

    def two_groups(i, carry):
        for slot in range(2):
            probs(2 * i + slot, slot)
            outputs(2 * i + slot, slot)
        return carry

    lax.fori_loop(0, n_groups // 2, two_groups, 0)


def _attention(proj, bias_tab, g, layer, batch, seq):
    n_rows = seq // GRID_W
    assert n_rows % (2 * ATTN_GROUP) == 0
    n_passes = WIDTH_A // (HEADS_PER_STEP * HEAD_DIM)
    return pl.pallas_call(
        functools.partial(_attn_kernel, n_rows=n_rows),
        grid=(batch,),
        in_specs=[
            pl.BlockSpec((seq, WIDTH_A), lambda b: (b, 0)),
            pl.BlockSpec((seq, WIDTH_A), lambda b: (b, 1)),
            pl.BlockSpec((seq, WIDTH_A), lambda b: (b, 2)),
            _layer_slab(bias_tab, layer, 1),
            _resident(),
        ],
        out_specs=pl.BlockSpec((seq, WIDTH_A), lambda b: (b, 0)),
        out_shape=jax.ShapeDtypeStruct((batch * seq, WIDTH_A), BF16),
        scratch_shapes=[
            pltpu.VMEM((n_passes, 2, ATTN_GROUP, HEADS_PER_STEP * GRID_W, KEY_SLAB), BF16),
            pltpu.VMEM((n_passes, 2, ATTN_GROUP, HEADS_PER_STEP * GRID_W, LANES), F32),
        ],
        compiler_params=_compiler_params(1),
        name="nbr_attention",
    )(proj, proj, proj, bias_tab, g.reshape(1, D_MODEL)[:, :WIDTH_A])


def _attention_bias_table(rpb):
    qc = np.arange(GRID_W)[:, None]
    kc = np.arange(GRID_W)[None, :]
    win = np.clip(qc - NA_COLS // 2, 0, GRID_W - NA_COLS)
    col_ok = (kc >= win) & (kc < win + NA_COLS)
    n_rel = rpb.shape[-1]
    onehot = (kc - qc + NA_COLS - 1)[None] == np.arange(n_rel)[:, None, None]
    zeros = np.zeros_like(onehot)
    onehot2 = np.concatenate([np.concatenate([onehot, zeros], axis=-1),
                              np.concatenate([zeros, onehot], axis=-1)], axis=0)
    rows2 = jnp.concatenate([rpb[:, :, :-1], rpb[:, :, 1:]], axis=-1).astype(F32)
    tiles = jnp.einsum("lhab,bqk->lhaqk", rows2, jnp.asarray(onehot2, F32),
                       precision=lax.Precision.HIGHEST)
    return jnp.where(jnp.asarray(np.tile(col_ok, (1, 2))), tiles, NEG_INF) * LOG2_E


def _gelu_tanh(x):
    c2 = -2.0 * math.sqrt(2.0 / math.pi) * LOG2_E
    return x / (1.0 + jnp.exp2(x * (c2 + (c2 * 0.044715) * (x * x))))


def _gate_chunk(z_ref, wcat_ref, bs_ref, g_ref, o_ref, t0):
    head_of_lane = lax.broadcasted_iota(jnp.int32, (1, WIDTH_B), 1) // HEAD_DIM
    u = z_ref[pl.ds(t0, CHUNK), :WIDTH_B].astype(F32)
    vn = z_ref[pl.ds(t0, CHUNK), WIDTH_B:]
    stacked = jnp.concatenate(
        [jnp.where(head_of_lane == h, vn, jnp.zeros_like(vn)) for h in range(N_HEADS_B)], axis=0)
    mixed = jnp.dot(wcat_ref[...], stacked, preferred_element_type=F32) + bs_ref[...]
    o_ref[pl.ds(t0, CHUNK), :] = _head_norm_lane_pairs(u * mixed, g_ref[...]).astype(o_ref.dtype)


def _dft_tables(seq):
    def cos_sin(n):
        i = lax.broadcasted_iota(jnp.int32, (n, n), 0)
        j = lax.broadcasted_iota(jnp.int32, (n, n), 1)
        ang = ((i * j) % n).astype(F32) * (2.0 * math.pi / n)
        return jnp.cos(ang), jnp.sin(ang)

    def cos_sin_rows(row_step, n_rows):
        i = lax.broadcasted_iota(jnp.int32, (n_rows, seq), 0) * row_step
        j = lax.broadcasted_iota(jnp.int32, (n_rows, seq), 1)
        ang = ((i * j) % seq).astype(F32) * (2.0 * math.pi / seq)
        return jnp.cos(ang), jnp.sin(ang)

    n_rows = seq // 2 + DFT_TAIL
    n_outer = -(-n_rows // GRID_W)
    ca, sa = (t[:, None, :] for t in cos_sin_rows(GRID_W, n_outer))
    cb, sb = (t[None, :, :] for t in cos_sin_rows(1, GRID_W))
    c_seq = (ca * cb - sa * sb).reshape(n_outer * GRID_W, seq)[:n_rows].astype(BF16)
    s_seq = (sa * cb + ca * sb).reshape(n_outer * GRID_W, seq)[:n_rows].astype(BF16)
    c_ch, s_ch = cos_sin(HEAD_DIM)
    eye = jnp.eye(WIDTH_C // HEAD_DIM, dtype=F32)
    ch_tab = jnp.concatenate([jnp.kron(eye, c_ch), jnp.kron(eye, s_ch)], axis=1).astype(BF16)
    i = lax.broadcasted_iota(jnp.int32, (N_CHUNK, N_CHUNK + DFT_TAIL), 0)
    k = lax.broadcasted_iota(jnp.int32, (N_CHUNK, N_CHUNK + DFT_TAIL), 1)
    reversal = (k == N_CHUNK - i).astype(BF16)
    return c_seq, s_seq, ch_tab, reversal


def _gate_fourier_kernel(z_ref, x_ref, wcat_ref, bs_ref, gb_ref,
                         cos_ref, sin_ref, ch_tab_ref, rev_ref, gc_ref,
                         ob_ref, oc_ref, hi_ref, *, seq):
    gate_args = (z_ref, wcat_ref, bs_ref, gb_ref, ob_ref)
    n_gate = seq // CHUNK
    n_chunks = seq // 2 // N_CHUNK
    gate_done = 0

    def gate_until(n):
        nonlocal gate_done
        for c in range(gate_done, min(n, n_gate)):
            _gate_chunk(*gate_args, c * CHUNK)
        gate_done = max(gate_done, min(n, n_gate))

    xcs = jnp.dot(x_ref[...], ch_tab_ref[...], preferred_element_type=F32)
    xc = xcs[:, :WIDTH_C].astype(BF16)
    xs = xcs[:, WIDTH_C:].astype(BF16)
    scale = 1.0 / math.sqrt(seq * HEAD_DIM)
    g = gc_ref[...]
    per_step = -(-n_gate // (2 * n_chunks))
    for t in range(n_chunks):
        r0 = t * N_CHUNK
        m = N_CHUNK + (DFT_TAIL if t == n_chunks - 1 else 0)
        a = jnp.dot(cos_ref[r0:r0 + m, :], xc, preferred_element_type=F32)
        b = jnp.dot(sin_ref[r0:r0 + m, :], xs, preferred_element_type=F32)
        lo = (a[:N_CHUNK] - b[:N_CHUNK]) * scale
        oc_ref[r0:r0 + N_CHUNK, :] = _head_norm_lane_pairs(lo, g).astype(oc_ref.dtype)
        hi_ref[r0:r0 + m, :] = _head_norm_lane_pairs((a + b) * scale, g).astype(hi_ref.dtype)
        gate_until((t + 1) * per_step)
    for u in range(n_chunks):
        src = hi_ref[u * N_CHUNK:(u + 1) * N_CHUNK + DFT_TAIL, :]
        blk = jnp.dot(rev_ref[...], src, preferred_element_type=F32)
        oc_ref[seq - (u + 1) * N_CHUNK:seq - u * N_CHUNK, :] = blk.astype(oc_ref.dtype)
        gate_until((n_chunks + u + 1) * per_step)
    gate_until(n_gate)


def _gate_fourier(proj, w_s, b_s, tables, g, batch, seq):
    z_off = 3 * WIDTH_A // (2 * WIDTH_B)
    c_off = (3 * WIDTH_A + 2 * WIDTH_B) // WIDTH_C
    wcat = jnp.transpose(w_s, (1, 0, 2)).reshape(CHUNK, N_HEADS_B * CHUNK).astype(BF16)
    bs_full = jnp.repeat(b_s.T.astype(F32), HEAD_DIM, axis=1)
    g_row = g.reshape(1, D_MODEL)
    g_b = g_row[:, WIDTH_A:WIDTH_A + WIDTH_B]
    g_c = g_row[:, WIDTH_A + WIDTH_B:]
    n_resident = 3 + len(tables) + 1
    return pl.pallas_call(
        functools.partial(_gate_fourier_kernel, seq=seq),
        grid=(batch,),
        in_specs=[
            pl.BlockSpec((seq, 2 * WIDTH_B), lambda b: (b, z_off)),
            pl.BlockSpec((seq, WIDTH_C), lambda b: (b, c_off)),
        ] + [_resident()] * n_resident,
        out_specs=[
            pl.BlockSpec((seq, WIDTH_B), lambda b: (b, 0)),
            pl.BlockSpec((seq, WIDTH_C), lambda b: (b, 0)),
        ],
        out_shape=[
            jax.ShapeDtypeStruct((batch * seq, WIDTH_B), BF16),
            jax.ShapeDtypeStruct((batch * seq, WIDTH_C), BF16),
        ],
        scratch_shapes=[pltpu.VMEM((seq // 2 + DFT_TAIL, WIDTH_C), BF16)],
        compiler_params=_compiler_params(1),
        name="gate_fourier",
    )(proj, proj, wcat, bs_full, g_b, *tables, g_c)


def _ffn_kernel(ma_ref, mb_ref, mc_ref, x_ref, wo_ref, gf_ref, wg_ref, wu_ref, wd_ref, gfin_ref,
                o_ref, h_ref, act_ref, *, final):
    x1_ref = o_ref
    mix = jnp.concatenate([ma_ref[...], mb_ref[...], mc_ref[...]], axis=-1)
    for c in range(D_MODEL // N_CHUNK):
        cols = slice(c * N_CHUNK, (c + 1) * N_CHUNK)
        x1_ref[:, cols] = x_ref[:, cols] + jnp.dot(mix, wo_ref[:, cols], preferred_element_type=F32)
    h_ref[...] = _rms_norm_rows(x1_ref[...], gf_ref[...]).astype(BF16)
    for c in range(D_FF // N_CHUNK):
        cols = slice(c * N_CHUNK, (c + 1) * N_CHUNK)
        gate = jnp.dot(h_ref[...], wg_ref[:, cols], preferred_element_type=F32)
        up = jnp.dot(h_ref[...], wu_ref[:, cols], preferred_element_type=F32)
        act_ref[:, cols] = (gate * jax.nn.sigmoid(gate) * up).astype(BF16)
    for c in range(D_MODEL // N_CHUNK):
        cols = slice(c * N_CHUNK, (c + 1) * N_CHUNK)
        x1_ref[:, cols] = x1_ref[:, cols] + jnp.dot(act_ref[...], wd_ref[:, cols],
                                                    preferred_element_type=F32)
    if final:
        o_ref[...] = _rms_norm_rows(x1_ref[...], gfin_ref[...])


def _out_proj_ffn(mix_a, mix_b, mix_c, x2d, w_out, g_ffn, w_gate, w_up, w_down, g_final, layer,
                  final):
    n = x2d.shape[0]
    row = lambda width: pl.BlockSpec((FFN_TM, width), lambda i: (i, 0))
    slab = lambda w: _layer_slab(w, layer, 1)
    return pl.pallas_call(
        functools.partial(_ffn_kernel, final=final),
        grid=(n // FFN_TM,),
        in_specs=[row(WIDTH_A), row(WIDTH_B), row(WIDTH_C), row(D_MODEL),
                  slab(w_out), _resident(), slab(w_gate), slab(w_up), slab(w_down), _resident()],
        out_specs=row(D_MODEL),
        out_shape=jax.ShapeDtypeStruct((n, D_MODEL), F32),
        scratch_shapes=[
            pltpu.VMEM((FFN_TM, D_MODEL), BF16),
            pltpu.VMEM((FFN_TM, D_FF), BF16),
        ],
        compiler_params=_compiler_params(1),
        name="out_proj_ffn",
    )(mix_a, mix_b, mix_c, x2d, w_out, g_ffn.reshape(1, D_MODEL), w_gate, w_up, w_down,
      g_final.reshape(1, D_MODEL))


def kernel(x, norm_mix_g, w_in, rpb, gmlp_ln_g, gmlp_ln_b, w_spatial, b_spatial, head_norm_g, w_out,
           norm_ffn_g, w_gate, w_up, w_down, final_norm_g):
    batch, seq, d_model = x.shape
    assert d_model == D_MODEL and seq % (2 * N_CHUNK) == 0 and seq // GRID_W >= NA_ROWS
    depth = w_in.shape[0]
    dft_tables = _dft_tables(seq)
    bias_tab = _attention_bias_table(rpb)
    w_in, w_out, w_gate, w_up, w_down = (w.astype(BF16) for w in (w_in, w_out, w_gate, w_up, w_down))
    x2d = x.reshape(batch * seq, D_MODEL)
    for l in range(depth):
        proj = _in_proj(x2d, norm_mix_g[l], w_in, gmlp_ln_g[l], gmlp_ln_b[l], l)
        mix_a = _attention(proj, bias_tab, head_norm_g[l], l, batch, seq)
        mix_b, mix_c = _gate_fourier(proj, w_spatial[l], b_spatial[l], dft_tables, head_norm_g[l],
                                     batch, seq)
        x2d = _out_proj_ffn(mix_a, mix_b, mix_c, x2d, w_out, norm_ffn_g[l], w_gate, w_up, w_down,
                            final_norm_g, l, final=(l == depth - 1))
    return x2d.reshape(batch, seq, D_MODEL)
```

```python
import functools
import math

import numpy as np
import jax
import jax.numpy as jnp
from jax import lax
from jax.experimental import pallas as pl
from jax.experimental.pallas import tpu as pltpu

F32 = jnp.float32
BF16 = jnp.bfloat16

D_MODEL = 1024
GRID_W = 64
HEAD_DIM = 64
WIDTH_A = D_MODEL // 2
WIDTH_B = D_MODEL // 4
WIDTH_C = D_MODEL // 4
N_HEADS_A = WIDTH_A // HEAD_DIM
N_HEADS_B = WIDTH_B // HEAD_DIM
IN_WIDTH = 3 * WIDTH_A + 2 * WIDTH_B + WIDTH_C
NA_ROWS = 8
NA_COLS = 16
CHUNK = 128
D_FF = -(-8 * D_MODEL // (3 * 256)) * 256
EPS = 1e-6
NEG_INF = -1e9

LANES = 128
VMEM_LIMIT_BYTES = 56 * 1024 * 1024

HEADS_PER_STEP = 4
KEY_SLAB = NA_ROWS * GRID_W
ATTN_GROUP = 16
LOG2_E = math.log2(math.e)
Q_SCALE = HEAD_DIM ** -0.5 * LOG2_E

IN_TM = 1024
IN_RB = 512
FFN_TM = 1024
N_CHUNK = 256
DFT_TAIL = 16


def _compiler_params(n_grid_dims):
    return pltpu.CompilerParams(
        dimension_semantics=("arbitrary",) * n_grid_dims,
        vmem_limit_bytes=VMEM_LIMIT_BYTES,
    )


def _resident():
    return pl.BlockSpec(memory_space=pltpu.VMEM)


def _layer_slab(stacked, layer, n_grid_dims):
    zeros = (0,) * (stacked.ndim - 1)
    index_map = {1: lambda i: (layer,) + zeros, 2: lambda i, j: (layer,) + zeros}[n_grid_dims]
    return pl.BlockSpec((None,) + stacked.shape[1:], index_map, pipeline_mode=pl.Buffered(1))


def _rms_norm_rows(x, g):
    ms = jnp.mean(x * x, axis=-1, keepdims=True)
    return x * lax.rsqrt(ms + EPS) * g


def _head_norm_lane_pairs(y, g):
    n_tiles = y.shape[-1] // LANES
    lane = lax.broadcasted_iota(jnp.int32, (1, LANES), 1)
    first = lane < HEAD_DIM
    outs = []
    for t in range(n_tiles):
        yt = y[:, t * LANES:(t + 1) * LANES]
        sq = yt * yt
        ss_a = jnp.sum(jnp.where(first, sq, 0.0), axis=-1, keepdims=True)
        ss_b = jnp.sum(jnp.where(first, 0.0, sq), axis=-1, keepdims=True)
        ss = jnp.where(first, ss_a, ss_b)
        gain = g[:, t * LANES:(t + 1) * LANES] * math.sqrt(HEAD_DIM)
        outs.append(yt * lax.rsqrt(ss + HEAD_DIM * EPS) * gain)
    return outs[0] if n_tiles == 1 else jnp.concatenate(outs, axis=-1)


def _in_proj_kernel(x_ref, g_ref, w_ref, lng_ref, lnb_ref, o_ref, h_ref):
    assert N_CHUNK == WIDTH_B
    u_chunk = 3 * WIDTH_A // N_CHUNK
    v_chunk = u_chunk + 1
    others = [c for c in range(IN_WIDTH // N_CHUNK) if c not in (u_chunk, v_chunk)]
    for rb in range(IN_TM // IN_RB):
        rows = slice(rb * IN_RB, (rb + 1) * IN_RB)
        h_ref[rows, :] = _rms_norm_rows(x_ref[rows, :], g_ref[...]).astype(BF16)
        for c in [v_chunk, u_chunk] + others:
            cols = slice(c * N_CHUNK, (c + 1) * N_CHUNK)
            acc = jnp.dot(h_ref[rows, :], w_ref[:, cols], preferred_element_type=F32)
            if (c + 1) * N_CHUNK <= WIDTH_A:
                acc = acc * Q_SCALE
            elif c in (u_chunk, v_chunk):
                acc = _gelu_tanh(acc)
                if c == v_chunk:
                    mu = jnp.mean(acc, axis=-1, keepdims=True)
                    d = acc - mu
                    var = jnp.mean(d * d, axis=-1, keepdims=True)
                    acc = d * lax.rsqrt(var + EPS) * lng_ref[...] + lnb_ref[...]
            o_ref[rows, cols] = acc.astype(o_ref.dtype)


def _in_proj(x2d, g, w_stack, ln_g, ln_b, layer):
    n = x2d.shape[0]
    return pl.pallas_call(
        _in_proj_kernel,
        grid=(n // IN_TM,),
        in_specs=[
            pl.BlockSpec((IN_TM, D_MODEL), lambda i: (i, 0)),
            _resident(),
            _layer_slab(w_stack, layer, 1),
            _resident(),
            _resident(),
        ],
        out_specs=pl.BlockSpec((IN_TM, IN_WIDTH), lambda i: (i, 0)),
        out_shape=jax.ShapeDtypeStruct((n, IN_WIDTH), BF16),
        scratch_shapes=[pltpu.VMEM((IN_TM, D_MODEL), BF16)],
        compiler_params=_compiler_params(1),
        name="in_proj",
    )(x2d, g.reshape(1, D_MODEL), w_stack, ln_g.reshape(1, WIDTH_B), ln_b.reshape(1, WIDTH_B))


def _attn_kernel(q_ref, k_ref, v_ref, bias_ref, g_ref, o_ref, p_ref, l_ref, *, n_rows):
    width = HEADS_PER_STEP * HEAD_DIM
    for pas in range(WIDTH_A // width):
        lanes = pl.ds(pas * width, width)
        _attention_pass(q_ref.at[:, lanes], k_ref.at[:, lanes], v_ref.at[:, lanes],
                        bias_ref.at[pl.ds(pas * HEADS_PER_STEP, HEADS_PER_STEP)], g_ref[:, lanes],
                        o_ref.at[:, lanes], p_ref.at[pas], l_ref.at[pas], n_rows)


def _attention_pass(q_ref, k_ref, v_ref, bias_ref, g, o_ref, p_ref, l_ref, n_rows):
    width = HEADS_PER_STEP * HEAD_DIM
    head_of_lane = lax.broadcasted_iota(jnp.int32, (1, width), 1) // HEAD_DIM
    n_groups = n_rows // ATTN_GROUP

    def rows_of(group):
        for j in range(ATTN_GROUP):
            r = group * ATTN_GROUP + j
            rs = jnp.clip(r - NA_ROWS // 2, 0, n_rows - NA_ROWS)
            yield j, r, rs, pl.multiple_of(r * GRID_W, GRID_W), pl.multiple_of(rs * GRID_W, GRID_W)

    def probs(group, slot):
        for j, r, rs, q0, k0 in rows_of(group):
            q = q_ref[pl.ds(q0, GRID_W), :]
            zero = jnp.zeros_like(q)
            qm = jnp.concatenate(
                [jnp.where(head_of_lane == h, q, zero) for h in range(HEADS_PER_STEP)], axis=0)
            s = lax.dot_general(qm, k_ref[pl.ds(k0, KEY_SLAB), :], (((1,), (1,)), ((), ())),
                                preferred_element_type=F32)
            a0 = NA_ROWS - 1 - (r - rs)
            s = s + jnp.concatenate(
                [jnp.concatenate([bias_ref[h, a0 + 2 * i] for i in range(NA_ROWS // 2)], axis=-1)
                 for h in range(HEADS_PER_STEP)], axis=0)
            m = jnp.max(s, axis=-1, keepdims=True)
            p = jnp.exp2(s - m)
            l = jnp.sum(p, axis=-1, keepdims=True)
            l_ref[slot, j] = jnp.broadcast_to(l, (l.shape[0], LANES))
            p_ref[slot, j] = p.astype(BF16)

    def outputs(group, slot):
        for j, _, _, q0, k0 in rows_of(group):
            o4 = jnp.dot(p_ref[slot, j], v_ref[pl.ds(k0, KEY_SLAB), :], preferred_element_type=F32)
            inv_l = 1.0 / l_ref[slot, j]
            o4 = o4 * jnp.concatenate([inv_l] * (width // LANES), axis=-1)
            o = o4[:GRID_W]
            for h in range(1, HEADS_PER_STEP):
                o = jnp.where(head_of_lane == h, o4[h * GRID_W:(h + 1) * GRID_W], o)
            o_ref[pl.ds(q0, GRID_W), :] = _head_norm_lane_pairs(o, g).astype(o_ref.dtype)

    def two_groups(i, carry):
        probs(2 * i, 0)
        probs(2 * i + 1, 1)
        outputs(2 * i, 0)
        outputs(2 * i + 1, 1)
        return carry

    lax.fori_loop(0, n_groups // 2, two_groups, 0)


def _attention(proj, bias_tab, g, layer, batch, seq):
    n_rows = seq // GRID_W
    assert n_rows % (2 * ATTN_GROUP) == 0
    n_passes = WIDTH_A // (HEADS_PER_STEP * HEAD_DIM)
    return pl.pallas_call(
        functools.partial(_attn_kernel, n_rows=n_rows),
        grid=(batch,),
        in_specs=[
            pl.BlockSpec((seq, WIDTH_A), lambda b: (b, 0)),
            pl.BlockSpec((seq, WIDTH_A), lambda b: (b, 1)),
            pl.BlockSpec((seq, WIDTH_A), lambda b: (b, 2)),
            _layer_slab(bias_tab, layer, 1),
            _resident(),
        ],
        out_specs=pl.BlockSpec((seq, WIDTH_A), lambda b: (b, 0)),
        out_shape=jax.ShapeDtypeStruct((batch * seq, WIDTH_A), BF16),
        scratch_shapes=[
            pltpu.VMEM((n_passes, 2, ATTN_GROUP, HEADS_PER_STEP * GRID_W, KEY_SLAB), BF16),
            pltpu.VMEM((n_passes, 2, ATTN_GROUP, HEADS_PER_STEP * GRID_W, LANES), F32),
        ],
        compiler_params=_compiler_params(1),
        name="nbr_attention",
    )(proj, proj, proj, bias_tab, g.reshape(1, D_MODEL)[:, :WIDTH_A])


def _attention_bias_table(rpb):
    qc = np.arange(GRID_W)[:, None]
    kc = np.arange(GRID_W)[None, :]
    win = np.clip(qc - NA_COLS // 2, 0, GRID_W - NA_COLS)
    col_ok = (kc >= win) & (kc < win + NA_COLS)
    n_rel = rpb.shape[-1]
    onehot = (kc - qc + NA_COLS - 1)[None] == np.arange(n_rel)[:, None, None]
    zeros = np.zeros_like(onehot)
    onehot2 = np.concatenate([np.concatenate([onehot, zeros], axis=-1),
                              np.concatenate([zeros, onehot], axis=-1)], axis=0)
    rows2 = jnp.concatenate([rpb[:, :, :-1], rpb[:, :, 1:]], axis=-1).astype(F32)
    tiles = jnp.einsum("lhab,bqk->lhaqk", rows2, jnp.asarray(onehot2, F32),
                       precision=lax.Precision.HIGHEST)
    return jnp.where(jnp.asarray(np.tile(col_ok, (1, 2))), tiles, NEG_INF) * LOG2_E


def _gelu_tanh(x):
    c2 = -2.0 * math.sqrt(2.0 / math.pi) * LOG2_E
    return x / (1.0 + jnp.exp2(x * (c2 + (c2 * 0.044715) * (x * x))))


def _gate_chunk(z_ref, wcat_ref, bs_ref, g_ref, o_ref, t0):
    head_of_lane = lax.broadcasted_iota(jnp.int32, (1, WIDTH_B), 1) // HEAD_DIM
    u = z_ref[pl.ds(t0, CHUNK), :WIDTH_B].astype(F32)
    vn = z_ref[pl.ds(t0, CHUNK), WIDTH_B:]
    stacked = jnp.concatenate(
        [jnp.where(head_of_lane == h, vn, jnp.zeros_like(vn)) for h in range(N_HEADS_B)], axis=0)
    mixed = jnp.dot(wcat_ref[...], stacked, preferred_element_type=F32) + bs_ref[...]
    o_ref[pl.ds(t0, CHUNK), :] = _head_norm_lane_pairs(u * mixed, g_ref[...]).astype(o_ref.dtype)


def _dft_tables(seq):
    def cos_sin(n):
        i = lax.broadcasted_iota(jnp.int32, (n, n), 0)
        j = lax.broadcasted_iota(jnp.int32, (n, n), 1)
        ang = ((i * j) % n).astype(F32) * (2.0 * math.pi / n)
        return jnp.cos(ang), jnp.sin(ang)

    def cos_sin_rows(row_step, n_rows):
        i = lax.broadcasted_iota(jnp.int32, (n_rows, seq), 0) * row_step
        j = lax.broadcasted_iota(jnp.int32, (n_rows, seq), 1)
        ang = ((i * j) % seq).astype(F32) * (2.0 * math.pi / seq)
        return jnp.cos(ang), jnp.sin(ang)

    n_rows = seq // 2 + DFT_TAIL
    n_outer = -(-n_rows // GRID_W)
    ca, sa = (t[:, None, :] for t in cos_sin_rows(GRID_W, n_outer))
    cb, sb = (t[None, :, :] for t in cos_sin_rows(1, GRID_W))
    c_seq = (ca * cb - sa * sb).reshape(n_outer * GRID_W, seq)[:n_rows].astype(BF16)
    s_seq = (sa * cb + ca * sb).reshape(n_outer * GRID_W, seq)[:n_rows].astype(BF16)
    c_ch, s_ch = cos_sin(HEAD_DIM)
    eye = jnp.eye(WIDTH_C // HEAD_DIM, dtype=F32)
    ch_tab = jnp.concatenate([jnp.kron(eye, c_ch), jnp.kron(eye, s_ch)], axis=1).astype(BF16)
    i = lax.broadcasted_iota(jnp.int32, (N_CHUNK, N_CHUNK + DFT_TAIL), 0)
    k = lax.broadcasted_iota(jnp.int32, (N_CHUNK, N_CHUNK + DFT_TAIL), 1)
    reversal = (k == N_CHUNK - i).astype(BF16)
    return c_seq, s_seq, ch_tab, reversal


def _gate_fourier_kernel(z_ref, x_ref, wcat_ref, bs_ref, gb_ref,
                         cos_ref, sin_ref, ch_tab_ref, rev_ref, gc_ref,
                         ob_ref, oc_ref, hi_ref, *, seq):
    gate_args = (z_ref, wcat_ref, bs_ref, gb_ref, ob_ref)
    n_gate = seq // CHUNK
    n_chunks = seq // 2 // N_CHUNK
    gate_done = 0

    def gate_until(n):
        nonlocal gate_done
        for c in range(gate_done, min(n, n_gate)):
            _gate_chunk(*gate_args, c * CHUNK)
        gate_done = max(gate_done, min(n, n_gate))

    xcs = jnp.dot(x_ref[...], ch_tab_ref[...], preferred_element_type=F32)
    xc = xcs[:, :WIDTH_C].astype(BF16)
    xs = xcs[:, WIDTH_C:].astype(BF16)
    scale = 1.0 / math.sqrt(seq * HEAD_DIM)
    g = gc_ref[...]
    per_step = -(-n_gate // (2 * n_chunks))
    for t in range(n_chunks):
        r0 = t * N_CHUNK
        m = N_CHUNK + (DFT_TAIL if t == n_chunks - 1 else 0)
        a = jnp.dot(cos_ref[r0:r0 + m, :], xc, preferred_element_type=F32)
        b = jnp.dot(sin_ref[r0:r0 + m, :], xs, preferred_element_type=F32)
        lo = (a[:N_CHUNK] - b[:N_CHUNK]) * scale
        oc_ref[r0:r0 + N_CHUNK, :] = _head_norm_lane_pairs(lo, g).astype(oc_ref.dtype)
        hi_ref[r0:r0 + m, :] = _head_norm_lane_pairs((a + b) * scale, g).astype(hi_ref.dtype)
        gate_until((t + 1) * per_step)
    for u in range(n_chunks):
        src = hi_ref[u * N_CHUNK:(u + 1) * N_CHUNK + DFT_TAIL, :]
        blk = jnp.dot(rev_ref[...], src, preferred_element_type=F32)
        oc_ref[seq - (u + 1) * N_CHUNK:seq - u * N_CHUNK, :] = blk.astype(oc_ref.dtype)
        gate_until((n_chunks + u + 1) * per_step)
    gate_until(n_gate)


def _gate_fourier(proj, w_s, b_s, tables, g, batch, seq):
    z_off = 3 * WIDTH_A // (2 * WIDTH_B)
    c_off = (3 * WIDTH_A + 2 * WIDTH_B) // WIDTH_C
    wcat = jnp.transpose(w_s, (1, 0, 2)).reshape(CHUNK, N_HEADS_B * CHUNK).astype(BF16)
    bs_full = jnp.repeat(b_s.T.astype(F32), HEAD_DIM, axis=1)
    g_row = g.reshape(1, D_MODEL)
    g_b = g_row[:, WIDTH_A:WIDTH_A + WIDTH_B]
    g_c = g_row[:, WIDTH_A + WIDTH_B:]
    n_resident = 3 + len(tables) + 1
    return pl.pallas_call(
        functools.partial(_gate_fourier_kernel, seq=seq),
        grid=(batch,),
        in_specs=[
            pl.BlockSpec((seq, 2 * WIDTH_B), lambda b: (b, z_off)),
            pl.BlockSpec((seq, WIDTH_C), lambda b: (b, c_off)),
        ] + [_resident()] * n_resident,
        out_specs=[
            pl.BlockSpec((seq, WIDTH_B), lambda b: (b, 0)),
            pl.BlockSpec((seq, WIDTH_C), lambda b: (b, 0)),
        ],
        out_shape=[
            jax.ShapeDtypeStruct((batch * seq, WIDTH_B), BF16),
            jax.ShapeDtypeStruct((batch * seq, WIDTH_C), BF16),
        ],
        scratch_shapes=[pltpu.VMEM((seq // 2 + DFT_TAIL, WIDTH_C), BF16)],
        compiler_params=_compiler_params(1),
        name="gate_fourier",
    )(proj, proj, wcat, bs_full, g_b, *tables, g_c)


def _ffn_kernel(ma_ref, mb_ref, mc_ref, x_ref, wo_ref, gf_ref, wg_ref, wu_ref, wd_ref, gfin_ref,
                o_ref, h_ref, act_ref, *, final):
    x1_ref = o_ref
    mix = jnp.concatenate([ma_ref[...], mb_ref[...], mc_ref[...]], axis=-1)
    for c in range(D_MODEL // N_CHUNK):
        cols = slice(c * N_CHUNK, (c + 1) * N_CHUNK)
        x1_ref[:, cols] = x_ref[:, cols] + jnp.dot(mix, wo_ref[:, cols], preferred_element_type=F32)
    h_ref[...] = _rms_norm_rows(x1_ref[...], gf_ref[...]).astype(BF16)
    for c in range(D_FF // N_CHUNK):
        cols = slice(c * N_CHUNK, (c + 1) * N_CHUNK)
        gate = jnp.dot(h_ref[...], wg_ref[:, cols], preferred_element_type=F32)
        up = jnp.dot(h_ref[...], wu_ref[:, cols], preferred_element_type=F32)
        act_ref[:, cols] = (gate * jax.nn.sigmoid(gate) * up).astype(BF16)
    for c in range(D_MODEL // N_CHUNK):
        cols = slice(c * N_CHUNK, (c + 1) * N_CHUNK)
        x1_ref[:, cols] = x1_ref[:, cols] + jnp.dot(act_ref[...], wd_ref[:, cols],
                                                    preferred_element_type=F32)
    if final:
        o_ref[...] = _rms_norm_rows(x1_ref[...], gfin_ref[...])


def _out_proj_ffn(mix_a, mix_b, mix_c, x2d, w_out, g_ffn, w_gate, w_up, w_down, g_final, layer,
                  final):
    n = x2d.shape[0]
    row = lambda width: pl.BlockSpec((FFN_TM, width), lambda i: (i, 0))
    slab = lambda w: _layer_slab(w, layer, 1)
    return pl.pallas_call(
        functools.partial(_ffn_kernel, final=final),
        grid=(n // FFN_TM,),
        in_specs=[row(WIDTH_A), row(WIDTH_B), row(WIDTH_C), row(D_MODEL),
                  slab(w_out), _resident(), slab(w_gate), slab(w_up), slab(w_down), _resident()],
        out_specs=row(D_MODEL),
        out_shape=jax.ShapeDtypeStruct((n, D_MODEL), F32),
        scratch_shapes=[
            pltpu.VMEM((FFN_TM, D_MODEL), BF16),
            pltpu.VMEM((FFN_TM, D_FF), BF16),
        ],
        compiler_params=_compiler_params(1),
        name="out_proj_ffn",
    )(mix_a, mix_b, mix_c, x2d, w_out, g_ffn.reshape(1, D_MODEL), w_gate, w_up, w_down,
      g_final.reshape(1, D_MODEL))


def kernel(x, norm_mix_g, w_in, rpb, gmlp_ln_g, gmlp_ln_b, w_spatial, b_spatial, head_norm_g, w_out,
           norm_ffn_g, w_gate, w_up, w_down, final_norm_g):
    batch, seq, d_model = x.shape
    assert d_model == D_MODEL and seq % (2 * N_CHUNK) == 0 and seq // GRID_W >= NA_ROWS
    depth = w_in.shape[0]
    dft_tables = _dft_tables(seq)
    bias_tab = _attention_bias_table(rpb)
    w_in, w_out, w_gate, w_up, w_down = (w.astype(BF16) for w in (w_in, w_out, w_gate, w_up, w_down))
    x2d = x.reshape(batch * seq, D_MODEL)
    for l in range(depth):
        proj = _in_proj(x2d, norm_mix_g[l], w_in, gmlp_ln_g[l], gmlp_ln_b[l], l)
        mix_a = _attention(proj, bias_tab, head_norm_g[l], l, batch, seq)
        mix_b, mix_c = _gate_fourier(proj, w_spatial[l], b_spatial[l], dft_tables, head_norm_g[l],
                                     batch, seq)
        x2d = _out_proj_ffn(mix_a, mix_b, mix_c, x2d, w_out, norm_ffn_g[l], w_gate, w_up, w_down,
                            final_norm_g, l, final=(l == depth - 1))
    return x2d.reshape(batch, seq, D_MODEL)
```

```python
import functools
import math

import numpy as np
import jax
import jax.numpy as jnp
from jax import lax
from jax.experimental import pallas as pl
from jax.experimental.pallas import tpu as pltpu

F32 = jnp.float32
BF16 = jnp.bfloat16

D_MODEL = 1024
GRID_W = 64
HEAD_DIM = 64
WIDTH_A = D_MODEL // 2
WIDTH_B = D_MODEL // 4
WIDTH_C = D_MODEL // 4
N_HEADS_A = WIDTH_A // HEAD_DIM
N_HEADS_B = WIDTH_B // HEAD_DIM
IN_WIDTH = 3 * WIDTH_A + 2 * WIDTH_B + WIDTH_C
NA_ROWS = 8
NA_COLS = 16
CHUNK = 128
D_FF = -(-8 * D_MODEL // (3 * 256)) * 256
EPS = 1e-6
NEG_INF = -1e9

LANES = 128
VMEM_LIMIT_BYTES = 56 * 1024 * 1024

HEADS_PER_STEP = 4
KEY_SLAB = NA_ROWS * GRID_W
ATTN_GROUP = 16
LOG2_E = math.log2(math.e)
Q_SCALE = HEAD_DIM ** -0.5 * LOG2_E

IN_TM = 1024
IN_RB = 512
FFN_TM = 1024
N_CHUNK = 256
DFT_TAIL = 16


def _compiler_params(n_grid_dims):
    return pltpu.CompilerParams(
        dimension_semantics=("arbitrary",) * n_grid_dims,
        vmem_limit_bytes=VMEM_LIMIT_BYTES,
    )


def _resident():
    return pl.BlockSpec(memory_space=pltpu.VMEM)


def _layer_slab(stacked, layer, n_grid_dims):
    zeros = (0,) * (stacked.ndim - 1)
    index_map = {1: lambda i: (layer,) + zeros, 2: lambda i, j: (layer,) + zeros}[n_grid_dims]
    return pl.BlockSpec((None,) + stacked.shape[1:], index_map, pipeline_mode=pl.Buffered(1))


def _rms_norm_rows(x, g):
    ms = jnp.mean(x * x, axis=-1, keepdims=True)
    return x * lax.rsqrt(ms + EPS) * g


def _head_norm_lane_pairs(y, g):
    n_tiles = y.shape[-1] // LANES
    lane = lax.broadcasted_iota(jnp.int32, (1, LANES), 1)
    first = lane < HEAD_DIM
    outs = []
    for t in range(n_tiles):
        yt = y[:, t * LANES:(t + 1) * LANES]
        sq = yt * yt
        ss_a = jnp.sum(jnp.where(first, sq, 0.0), axis=-1, keepdims=True)
        ss_b = jnp.sum(jnp.where(first, 0.0, sq), axis=-1, keepdims=True)
        ss = jnp.where(first, ss_a, ss_b)
        gain = g[:, t * LANES:(t + 1) * LANES] * math.sqrt(HEAD_DIM)
        outs.append(yt * lax.rsqrt(ss + HEAD_DIM * EPS) * gain)
    return outs[0] if n_tiles == 1 else jnp.concatenate(outs, axis=-1)


def _in_proj_kernel(x_ref, g_ref, w_ref, lng_ref, lnb_ref, o_ref, h_ref):
    assert N_CHUNK == WIDTH_B
    u_chunk = 3 * WIDTH_A // N_CHUNK
    v_chunk = u_chunk + 1
    others = [c for c in range(IN_WIDTH // N_CHUNK) if c not in (u_chunk, v_chunk)]
    for rb in range(IN_TM // IN_RB):
        rows = slice(rb * IN_RB, (rb + 1) * IN_RB)
        h_ref[rows, :] = _rms_norm_rows(x_ref[rows, :], g_ref[...]).astype(BF16)
        for c in [v_chunk, u_chunk] + others:
            cols = slice(c * N_CHUNK, (c + 1) * N_CHUNK)
            acc = jnp.dot(h_ref[rows, :], w_ref[:, cols], preferred_element_type=F32)
            if (c + 1) * N_CHUNK <= WIDTH_A:
                acc = acc * Q_SCALE
            elif c in (u_chunk, v_chunk):
                acc = _gelu_tanh(acc)
                if c == v_chunk:
                    mu = jnp.mean(acc, axis=-1, keepdims=True)
                    d = acc - mu
                    var = jnp.mean(d * d, axis=-1, keepdims=True)
                    acc = d * lax.rsqrt(var + EPS) * lng_ref[...] + lnb_ref[...]
            o_ref[rows, cols] = acc.astype(o_ref.dtype)


def _in_proj(x2d, g, w_stack, ln_g, ln_b, layer):
    n = x2d.shape[0]
    return pl.pallas_call(
        _in_proj_kernel,
        grid=(n // IN_TM,),
        in_specs=[
            pl.BlockSpec((IN_TM, D_MODEL), lambda i: (i, 0)),
            _resident(),
            _layer_slab(w_stack, layer, 1),
            _resident(),
            _resident(),
        ],
        out_specs=pl.BlockSpec((IN_TM, IN_WIDTH), lambda i: (i, 0)),
        out_shape=jax.ShapeDtypeStruct((n, IN_WIDTH), BF16),
        scratch_shapes=[pltpu.VMEM((IN_TM, D_MODEL), BF16)],
        compiler_params=_compiler_params(1),
        name="in_proj",
    )(x2d, g.reshape(1, D_MODEL), w_stack, ln_g.reshape(1, WIDTH_B), ln_b.reshape(1, WIDTH_B))


def _attn_kernel(q_ref, k_ref, v_ref, bias_ref, g_ref, o_ref, p_ref, l_ref, *, n_rows):
    width = HEADS_PER_STEP * HEAD_DIM
    n_groups = n_rows // ATTN_GROUP
    phases = []
    for pas in range(WIDTH_A // width):
        lanes = pl.ds(pas * width, width)
        probs, outputs = _attention_pass(
            q_ref.at[:, lanes], k_ref.at[:, lanes], v_ref.at[:, lanes],
            bias_ref.at[pl.ds(pas * HEADS_PER_STEP, HEADS_PER_STEP)], g_ref[:, lanes],
            o_ref.at[:, lanes], p_ref.at[pas], l_ref.at[pas], n_rows)
        phases.append([functools.partial(probs, grp, grp) for grp in range(n_groups)]
                      + [functools.partial(outputs, grp, grp) for grp in range(n_groups)])
    skew = n_groups - 1
    for step in range(len(phases) * 2 * n_groups):
        for pas, todo in reversed(list(enumerate(phases))):
            k = step - pas * (2 * n_groups - skew)
            if 0 <= k < len(todo):
                todo[k]()


def _attention_pass(q_ref, k_ref, v_ref, bias_ref, g, o_ref, p_ref, l_ref, n_rows):
    width = HEADS_PER_STEP * HEAD_DIM
    head_of_lane = lax.broadcasted_iota(jnp.int32, (1, width), 1) // HEAD_DIM

    def rows_of(group):
        for j in range(ATTN_GROUP):
            r = group * ATTN_GROUP + j
            rs = min(max(r - NA_ROWS // 2, 0), n_rows - NA_ROWS)
            yield j, r, rs, r * GRID_W, rs * GRID_W

    def probs(group, slot):
        for j, r, rs, q0, k0 in rows_of(group):
            q = q_ref[pl.ds(q0, GRID_W), :]
            zero = jnp.zeros_like(q)
            qm = jnp.concatenate(
                [jnp.where(head_of_lane == h, q, zero) for h in range(HEADS_PER_STEP)], axis=0)
            s = lax.dot_general(qm, k_ref[pl.ds(k0, KEY_SLAB), :], (((1,), (1,)), ((), ())),
                                preferred_element_type=F32)
            a0 = NA_ROWS - 1 - (r - rs)
            s = s + jnp.concatenate(
                [jnp.concatenate([bias_ref[h, a0 + 2 * i] for i in range(NA_ROWS // 2)], axis=-1)
                 for h in range(HEADS_PER_STEP)], axis=0)
            m = jnp.max(s, axis=-1, keepdims=True)
            p = jnp.exp2(s - m)
            l = jnp.sum(p, axis=-1, keepdims=True)
            l_ref[slot, j] = jnp.broadcast_to(l, (l.shape[0], LANES))
            p_ref[slot, j] = p.astype(BF16)

    def outputs(group, slot):
        for j, _, _, q0, k0 in rows_of(group):
            o4 = jnp.dot(p_ref[slot, j], v_ref[pl.ds(k0, KEY_SLAB), :], preferred_element_type=F32)
            inv_l = 1.0 / l_ref[slot, j]
            o4 = o4 * jnp.concatenate([inv_l] * (width // LANES), axis=-1)
            o = o4[:GRID_W]
            for h in range(1, HEADS_PER_STEP):
                o = jnp.where(head_of_lane == h, o4[h * GRID_W:(h + 1) * GRID_W], o)
            o_ref[pl.ds(q0, GRID_W), :] = _head_norm_lane_pairs(o, g).astype(o_ref.dtype)

    return probs, outputs


def _attention(proj, bias_tab, g, layer, batch, seq):
    n_rows = seq // GRID_W
    assert n_rows % (2 * ATTN_GROUP) == 0
    n_passes = WIDTH_A // (HEADS_PER_STEP * HEAD_DIM)
    return pl.pallas_call(
        functools.partial(_attn_kernel, n_rows=n_rows),
        grid=(batch,),
        in_specs=[
            pl.BlockSpec((seq, WIDTH_A), lambda b: (b, 0)),
            pl.BlockSpec((seq, WIDTH_A), lambda b: (b, 1)),
            pl.BlockSpec((seq, WIDTH_A), lambda b: (b, 2)),
            _layer_slab(bias_tab, layer, 1),
            _resident(),
        ],
        out_specs=pl.BlockSpec((seq, WIDTH_A), lambda b: (b, 0)),
        out_shape=jax.ShapeDtypeStruct((batch * seq, WIDTH_A), BF16),
        scratch_shapes=[
            pltpu.VMEM((n_passes, 2, ATTN_GROUP, HEADS_PER_STEP * GRID_W, KEY_SLAB), BF16),
            pltpu.VMEM((n_passes, 2, ATTN_GROUP, HEADS_PER_STEP * GRID_W, LANES), F32),
        ],
        compiler_params=_compiler_params(1),
        name="nbr_attention",
    )(proj, proj, proj, bias_tab, g.reshape(1, D_MODEL)[:, :WIDTH_A])


def _attention_bias_table(rpb):
    qc = np.arange(GRID_W)[:, None]
    kc = np.arange(GRID_W)[None, :]
    win = np.clip(qc - NA_COLS // 2, 0, GRID_W - NA_COLS)
    col_ok = (kc >= win) & (kc < win + NA_COLS)
    n_rel = rpb.shape[-1]
    onehot = (kc - qc + NA_COLS - 1)[None] == np.arange(n_rel)[:, None, None]
    zeros = np.zeros_like(onehot)
    onehot2 = np.concatenate([np.concatenate([onehot, zeros], axis=-1),
                              np.concatenate([zeros, onehot], axis=-1)], axis=0)
    rows2 = jnp.concatenate([rpb[:, :, :-1], rpb[:, :, 1:]], axis=-1).astype(F32)
    tiles = jnp.einsum("lhab,bqk->lhaqk", rows2, jnp.asarray(onehot2, F32),
                       precision=lax.Precision.HIGHEST)
    return jnp.where(jnp.asarray(np.tile(col_ok, (1, 2))), tiles, NEG_INF) * LOG2_E


def _gelu_tanh(x):
    c2 = -2.0 * math.sqrt(2.0 / math.pi) * LOG2_E
    return x / (1.0 + jnp.exp2(x * (c2 + (c2 * 0.044715) * (x * x))))


def _gate_chunk(z_ref, wcat_ref, bs_ref, g_ref, o_ref, t0):
    head_of_lane = lax.broadcasted_iota(jnp.int32, (1, WIDTH_B), 1) // HEAD_DIM
    u = z_ref[pl.ds(t0, CHUNK), :WIDTH_B].astype(F32)
    vn = z_ref[pl.ds(t0, CHUNK), WIDTH_B:]
    stacked = jnp.concatenate(
        [jnp.where(head_of_lane == h, vn, jnp.zeros_like(vn)) for h in range(N_HEADS_B)], axis=0)
    mixed = jnp.dot(wcat_ref[...], stacked, preferred_element_type=F32) + bs_ref[...]
    o_ref[pl.ds(t0, CHUNK), :] = _head_norm_lane_pairs(u * mixed, g_ref[...]).astype(o_ref.dtype)


def _dft_tables(seq):
    def cos_sin(n):
        i = lax.broadcasted_iota(jnp.int32, (n, n), 0)
        j = lax.broadcasted_iota(jnp.int32, (n, n), 1)
        ang = ((i * j) % n).astype(F32) * (2.0 * math.pi / n)
        return jnp.cos(ang), jnp.sin(ang)

    def cos_sin_rows(row_step, n_rows):
        i = lax.broadcasted_iota(jnp.int32, (n_rows, seq), 0) * row_step
        j = lax.broadcasted_iota(jnp.int32, (n_rows, seq), 1)
        ang = ((i * j) % seq).astype(F32) * (2.0 * math.pi / seq)
        return jnp.cos(ang), jnp.sin(ang)

    n_rows = seq // 2 + DFT_TAIL
    n_outer = -(-n_rows // GRID_W)
    ca, sa = (t[:, None, :] for t in cos_sin_rows(GRID_W, n_outer))
    cb, sb = (t[None, :, :] for t in cos_sin_rows(1, GRID_W))
    c_seq = (ca * cb - sa * sb).reshape(n_outer * GRID_W, seq)[:n_rows].astype(BF16)
    s_seq = (sa * cb + ca * sb).reshape(n_outer * GRID_W, seq)[:n_rows].astype(BF16)
    c_ch, s_ch = cos_sin(HEAD_DIM)
    eye = jnp.eye(WIDTH_C // HEAD_DIM, dtype=F32)
    ch_tab = jnp.concatenate([jnp.kron(eye, c_ch), jnp.kron(eye, s_ch)], axis=1).astype(BF16)
    i = lax.broadcasted_iota(jnp.int32, (N_CHUNK, N_CHUNK + DFT_TAIL), 0)
    k = lax.broadcasted_iota(jnp.int32, (N_CHUNK, N_CHUNK + DFT_TAIL), 1)
    reversal = (k == N_CHUNK - i).astype(BF16)
    return c_seq, s_seq, ch_tab, reversal


def _gate_fourier_kernel(z_ref, x_ref, wcat_ref, bs_ref, gb_ref,
                         cos_ref, sin_ref, ch_tab_ref, rev_ref, gc_ref,
                         ob_ref, oc_ref, hi_ref, *, seq):
    gate_args = (z_ref, wcat_ref, bs_ref, gb_ref, ob_ref)
    n_gate = seq // CHUNK
    n_chunks = seq // 2 // N_CHUNK
    gate_done = 0

    def gate_until(n):
        nonlocal gate_done
        for c in range(gate_done, min(n, n_gate)):
            _gate_chunk(*gate_args, c * CHUNK)
        gate_done = max(gate_done, min(n, n_gate))

    xcs = jnp.dot(x_ref[...], ch_tab_ref[...], preferred_element_type=F32)
    xc = xcs[:, :WIDTH_C].astype(BF16)
    xs = xcs[:, WIDTH_C:].astype(BF16)
    scale = 1.0 / math.sqrt(seq * HEAD_DIM)
    g = gc_ref[...]
    per_step = -(-n_gate // (2 * n_chunks))
    for t in range(n_chunks):
        r0 = t * N_CHUNK
        m = N_CHUNK + (DFT_TAIL if t == n_chunks - 1 else 0)
        a = jnp.dot(cos_ref[r0:r0 + m, :], xc, preferred_element_type=F32)
        b = jnp.dot(sin_ref[r0:r0 + m, :], xs, preferred_element_type=F32)
        lo = (a[:N_CHUNK] - b[:N_CHUNK]) * scale
        oc_ref[r0:r0 + N_CHUNK, :] = _head_norm_lane_pairs(lo, g).astype(oc_ref.dtype)
        hi_ref[r0:r0 + m, :] = _head_norm_lane_pairs((a + b) * scale, g).astype(hi_ref.dtype)
        gate_until((t + 1) * per_step)
    for u in range(n_chunks):
        src = hi_ref[u * N_CHUNK:(u + 1) * N_CHUNK + DFT_TAIL, :]
        blk = jnp.dot(rev_ref[...], src, preferred_element_type=F32)
        oc_ref[seq - (u + 1) * N_CHUNK:seq - u * N_CHUNK, :] = blk.astype(oc_ref.dtype)
        gate_until((n_chunks + u + 1) * per_step)
    gate_until(n_gate)


def _gate_fourier(proj, w_s, b_s, tables, g, batch, seq):
    z_off = 3 * WIDTH_A // (2 * WIDTH_B)
    c_off = (3 * WIDTH_A + 2 * WIDTH_B) // WIDTH_C
    wcat = jnp.transpose(w_s, (1, 0, 2)).reshape(CHUNK, N_HEADS_B * CHUNK).astype(BF16)
    bs_full = jnp.repeat(b_s.T.astype(F32), HEAD_DIM, axis=1)
    g_row = g.reshape(1, D_MODEL)
    g_b = g_row[:, WIDTH_A:WIDTH_A + WIDTH_B]
    g_c = g_row[:, WIDTH_A + WIDTH_B:]
    n_resident = 3 + len(tables) + 1
    return pl.pallas_call(
        functools.partial(_gate_fourier_kernel, seq=seq),
        grid=(batch,),
        in_specs=[
            pl.BlockSpec((seq, 2 * WIDTH_B), lambda b: (b, z_off)),
            pl.BlockSpec((seq, WIDTH_C), lambda b: (b, c_off)),
        ] + [_resident()] * n_resident,
        out_specs=[
            pl.BlockSpec((seq, WIDTH_B), lambda b: (b, 0)),
            pl.BlockSpec((seq, WIDTH_C), lambda b: (b, 0)),
        ],
        out_shape=[
            jax.ShapeDtypeStruct((batch * seq, WIDTH_B), BF16),
            jax.ShapeDtypeStruct((batch * seq, WIDTH_C), BF16),
        ],
        scratch_shapes=[pltpu.VMEM((seq // 2 + DFT_TAIL, WIDTH_C), BF16)],
        compiler_params=_compiler_params(1),
        name="gate_fourier",
    )(proj, proj, wcat, bs_full, g_b, *tables, g_c)


def _ffn_kernel(ma_ref, mb_ref, mc_ref, x_ref, wo_ref, gf_ref, wg_ref, wu_ref, wd_ref, gfin_ref,
                o_ref, h_ref, act_ref, *, final):
    x1_ref = o_ref
    mix = jnp.concatenate([ma_ref[...], mb_ref[...], mc_ref[...]], axis=-1)
    for c in range(D_MODEL // N_CHUNK):
        cols = slice(c * N_CHUNK, (c + 1) * N_CHUNK)
        x1_ref[:, cols] = x_ref[:, cols] + jnp.dot(mix, wo_ref[:, cols], preferred_element_type=F32)
    h_ref[...] = _rms_norm_rows(x1_ref[...], gf_ref[...]).astype(BF16)
    for c in range(D_FF // N_CHUNK):
        cols = slice(c * N_CHUNK, (c + 1) * N_CHUNK)
        gate = jnp.dot(h_ref[...], wg_ref[:, cols], preferred_element_type=F32)
        up = jnp.dot(h_ref[...], wu_ref[:, cols], preferred_element_type=F32)
        act_ref[:, cols] = (gate * jax.nn.sigmoid(gate) * up).astype(BF16)
    for c in range(D_MODEL // N_CHUNK):
        cols = slice(c * N_CHUNK, (c + 1) * N_CHUNK)
        x1_ref[:, cols] = x1_ref[:, cols] + jnp.dot(act_ref[...], wd_ref[:, cols],
                                                    preferred_element_type=F32)
    if final:
        o_ref[...] = _rms_norm_rows(x1_ref[...], gfin_ref[...])


def _out_proj_ffn(mix_a, mix_b, mix_c, x2d, w_out, g_ffn, w_gate, w_up, w_down, g_final, layer,
                  final):
    n = x2d.shape[0]
    row = lambda width: pl.BlockSpec((FFN_TM, width), lambda i: (i, 0))
    slab = lambda w: _layer_slab(w, layer, 1)
    return pl.pallas_call(
        functools.partial(_ffn_kernel, final=final),
        grid=(n // FFN_TM,),
        in_specs=[row(WIDTH_A), row(WIDTH_B), row(WIDTH_C), row(D_MODEL),
                  slab(w_out), _resident(), slab(w_gate), slab(w_up), slab(w_down), _resident()],
        out_specs=row(D_MODEL),
        out_shape=jax.ShapeDtypeStruct((n, D_MODEL), F32),
        scratch_shapes=[
            pltpu.VMEM((FFN_TM, D_MODEL), BF16),
            pltpu.VMEM((FFN_TM, D_FF), BF16),
        ],
        compiler_params=_compiler_params(1),
        name="out_proj_ffn",
    )(mix_a, mix_b, mix_c, x2d, w_out, g_ffn.reshape(1, D_MODEL), w_gate, w_up, w_down,
      g_final.reshape(1, D_MODEL))


def kernel(x, norm_mix_g, w_in, rpb, gmlp_ln_g, gmlp_ln_b, w_spatial, b_spatial, head_norm_g, w_out,
           norm_ffn_g, w_gate, w_up, w_down, final_norm_g):
    batch, seq, d_model = x.shape
    assert d_model == D_MODEL and seq % (2 * N_CHUNK) == 0 and seq // GRID_W >= NA_ROWS
    depth = w_in.shape[0]
    dft_tables = _dft_tables(seq)
    bias_tab = _attention_bias_table(rpb)
    w_in, w_out, w_gate, w_up, w_down = (w.astype(BF16) for w in (w_in, w_out, w_gate, w_up, w_down))
    x2d = x.reshape(batch * seq, D_MODEL)
    for l in range(depth):
        proj = _in_proj(x2d, norm_mix_g[l], w_in, gmlp_ln_g[l], gmlp_ln_b[l], l)
        mix_a = _attention(proj, bias_tab, head_norm_g[l], l, batch, seq)
        mix_b, mix_c = _gate_fourier(proj, w_spatial[l], b_spatial[l], dft_tables, head_norm_g[l],
                                     batch, seq)
        x2d = _out_proj_ffn(mix_a, mix_b, mix_c, x2d, w_out, norm_ffn_g[l], w_gate, w_up, w_down,
                            final_norm_g, l, final=(l == depth - 1))
    return x2d.reshape(batch, seq, D_MODEL)
```

```python
import functools
import math

import numpy as np
import jax
import jax.numpy as jnp
from jax import lax
from jax.experimental import pallas as pl
from jax.experimental.pallas import tpu as pltpu

F32 = jnp.float32
BF16 = jnp.bfloat16

D_MODEL = 1024
GRID_W = 64
HEAD_DIM = 64
WIDTH_A = D_MODEL // 2
WIDTH_B = D_MODEL // 4
WIDTH_C = D_MODEL // 4
N_HEADS_A = WIDTH_A // HEAD_DIM
N_HEADS_B = WIDTH_B // HEAD_DIM
IN_WIDTH = 3 * WIDTH_A + 2 * WIDTH_B + WIDTH_C
NA_ROWS = 8
NA_COLS = 16
CHUNK = 128
D_FF = -(-8 * D_MODEL // (3 * 256)) * 256
EPS = 1e-6
NEG_INF = -1e9

LANES = 128
VMEM_LIMIT_BYTES = 56 * 1024 * 1024

HEADS_PER_STEP = 4
KEY_SLAB = NA_ROWS * GRID_W
ATTN_GROUP = 16
LOG2_E = math.log2(math.e)
Q_SCALE = HEAD_DIM ** -0.5 * LOG2_E

IN_TM = 1024
IN_RB = 512
FFN_TM = 1024
N_CHUNK = 256
DFT_TAIL = 16


def _compiler_params(n_grid_dims):
    return pltpu.CompilerParams(
        dimension_semantics=("arbitrary",) * n_grid_dims,
        vmem_limit_bytes=VMEM_LIMIT_BYTES,
    )


def _resident():
    return pl.BlockSpec(memory_space=pltpu.VMEM)


def _layer_slab(stacked, layer, n_grid_dims):
    zeros = (0,) * (stacked.ndim - 1)
    index_map = {1: lambda i: (layer,) + zeros, 2: lambda i, j: (layer,) + zeros}[n_grid_dims]
    return pl.BlockSpec((None,) + stacked.shape[1:], index_map, pipeline_mode=pl.Buffered(1))


def _rms_norm_rows(x, g):
    ms = jnp.mean(x * x, axis=-1, keepdims=True)
    return x * lax.rsqrt(ms + EPS) * g


def _head_norm_lane_pairs(y, g):
    n_tiles = y.shape[-1] // LANES
    lane = lax.broadcasted_iota(jnp.int32, (1, LANES), 1)
    first = lane < HEAD_DIM
    outs = []
    for t in range(n_tiles):
        yt = y[:, t * LANES:(t + 1) * LANES]
        sq = yt * yt
        ss_a = jnp.sum(jnp.where(first, sq, 0.0), axis=-1, keepdims=True)
        ss_b = jnp.sum(jnp.where(first, 0.0, sq), axis=-1, keepdims=True)
        ss = jnp.where(first, ss_a, ss_b)
        gain = g[:, t * LANES:(t + 1) * LANES] * math.sqrt(HEAD_DIM)
        outs.append(yt * lax.rsqrt(ss + HEAD_DIM * EPS) * gain)
    return outs[0] if n_tiles == 1 else jnp.concatenate(outs, axis=-1)


def _in_proj_kernel(x_ref, g_ref, w_ref, lng_ref, lnb_ref, o_ref, h_ref):
    assert N_CHUNK == WIDTH_B
    u_chunk = 3 * WIDTH_A // N_CHUNK
    v_chunk = u_chunk + 1
    others = [c for c in range(IN_WIDTH // N_CHUNK) if c not in (u_chunk, v_chunk)]
    for rb in range(IN_TM // IN_RB):
        rows = slice(rb * IN_RB, (rb + 1) * IN_RB)
        h_ref[rows, :] = _rms_norm_rows(x_ref[rows, :], g_ref[...]).astype(BF16)
        for c in [v_chunk, u_chunk] + others:
            cols = slice(c * N_CHUNK, (c + 1) * N_CHUNK)
            acc = jnp.dot(h_ref[rows, :], w_ref[:, cols], preferred_element_type=F32)
            if (c + 1) * N_CHUNK <= WIDTH_A:
                acc = acc * Q_SCALE
            elif c in (u_chunk, v_chunk):
                acc = _gelu_tanh(acc)
                if c == v_chunk:
                    mu = jnp.mean(acc, axis=-1, keepdims=True)
                    d = acc - mu
                    var = jnp.mean(d * d, axis=-1, keepdims=True)
                    acc = d * lax.rsqrt(var + EPS) * lng_ref[...] + lnb_ref[...]
            o_ref[rows, cols] = acc.astype(o_ref.dtype)


def _in_proj(x2d, g, w_stack, ln_g, ln_b, layer):
    n = x2d.shape[0]
    return pl.pallas_call(
        _in_proj_kernel,
        grid=(n // IN_TM,),
        in_specs=[
            pl.BlockSpec((IN_TM, D_MODEL), lambda i: (i, 0)),
            _resident(),
            _layer_slab(w_stack, layer, 1),
            _resident(),
            _resident(),
        ],
        out_specs=pl.BlockSpec((IN_TM, IN_WIDTH), lambda i: (i, 0)),
        out_shape=jax.ShapeDtypeStruct((n, IN_WIDTH), BF16),
        scratch_shapes=[pltpu.VMEM((IN_TM, D_MODEL), BF16)],
        compiler_params=_compiler_params(1),
        name="in_proj",
    )(x2d, g.reshape(1, D_MODEL), w_stack, ln_g.reshape(1, WIDTH_B), ln_b.reshape(1, WIDTH_B))


def _attn_kernel(q_ref, k_ref, v_ref, bias_ref, g_ref, o_ref, p_ref, l_ref, *, n_rows):
    width = HEADS_PER_STEP * HEAD_DIM
    n_groups = n_rows // ATTN_GROUP
    phases = []
    for pas in range(WIDTH_A // width):
        lanes = pl.ds(pas * width, width)
        probs, outputs = _attention_pass(
            q_ref.at[:, lanes], k_ref.at[:, lanes], v_ref.at[:, lanes],
            bias_ref.at[pl.ds(pas * HEADS_PER_STEP, HEADS_PER_STEP)], g_ref[:, lanes],
            o_ref.at[:, lanes], p_ref.at[pas], l_ref.at[pas], n_rows)
        phases.append([functools.partial(probs, grp, grp) for grp in range(n_groups)]
                      + [functools.partial(outputs, grp, grp) for grp in range(n_groups)])
    skew = n_groups
    for step in range(len(phases) * 2 * n_groups):
        for pas, todo in reversed(list(enumerate(phases))):
            k = step - pas * (2 * n_groups - skew)
            if 0 <= k < len(todo):
                todo[k]()


def _attention_pass(q_ref, k_ref, v_ref, bias_ref, g, o_ref, p_ref, l_ref, n_rows):
    width = HEADS_PER_STEP * HEAD_DIM
    head_of_lane = lax.broadcasted_iota(jnp.int32, (1, width), 1) // HEAD_DIM

    def rows_of(group):
        for j in range(ATTN_GROUP):
            r = group * ATTN_GROUP + j
            rs = min(max(r - NA_ROWS // 2, 0), n_rows - NA_ROWS)
            yield j, r, rs, r * GRID_W, rs * GRID_W

    def probs(group, slot):
        for j, r, rs, q0, k0 in rows_of(group):
            q = q_ref[pl.ds(q0, GRID_W), :]
            zero = jnp.zeros_like(q)
            qm = jnp.concatenate(
                [jnp.where(head_of_lane == h, q, zero) for h in range(HEADS_PER_STEP)], axis=0)
            s = lax.dot_general(qm, k_ref[pl.ds(k0, KEY_SLAB), :], (((1,), (1,)), ((), ())),
                                preferred_element_type=F32)
            a0 = NA_ROWS - 1 - (r - rs)
            s = s + jnp.concatenate(
                [jnp.concatenate([bias_ref[h, a0 + 2 * i] for i in range(NA_ROWS // 2)], axis=-1)
                 for h in range(HEADS_PER_STEP)], axis=0)
            m = jnp.max(s, axis=-1, keepdims=True)
            p = jnp.exp2(s - m)
            l = jnp.sum(p, axis=-1, keepdims=True)
            l_ref[slot, j] = jnp.broadcast_to(l, (l.shape[0], LANES))
            p_ref[slot, j] = p.astype(BF16)

    def outputs(group, slot):
        for j, _, _, q0, k0 in rows_of(group):
            o4 = jnp.dot(p_ref[slot, j], v_ref[pl.ds(k0, KEY_SLAB), :], preferred_element_type=F32)
            inv_l = 1.0 / l_ref[slot, j]
            o4 = o4 * jnp.concatenate([inv_l] * (width // LANES), axis=-1)
            o = o4[:GRID_W]
            for h in range(1, HEADS_PER_STEP):
                o = jnp.where(head_of_lane == h, o4[h * GRID_W:(h + 1) * GRID_W], o)
            o_ref[pl.ds(q0, GRID_W), :] = _head_norm_lane_pairs(o, g).astype(o_ref.dtype)

    return probs, outputs


def _attention(proj, bias_tab, g, layer, batch, seq):
    n_rows = seq // GRID_W
    assert n_rows % (2 * ATTN_GROUP) == 0
    n_passes = WIDTH_A // (HEADS_PER_STEP * HEAD_DIM)
    return pl.pallas_call(
        functools.partial(_attn_kernel, n_rows=n_rows),
        grid=(batch,),
        in_specs=[
            pl.BlockSpec((seq, WIDTH_A), lambda b: (b, 0)),
            pl.BlockSpec((seq, WIDTH_A), lambda b: (b, 1)),
            pl.BlockSpec((seq, WIDTH_A), lambda b: (b, 2)),
            _layer_slab(bias_tab, layer, 1),
            _resident(),
        ],
        out_specs=pl.BlockSpec((seq, WIDTH_A), lambda b: (b, 0)),
        out_shape=jax.ShapeDtypeStruct((batch * seq, WIDTH_A), BF16),
        scratch_shapes=[
            pltpu.VMEM((n_passes, 2, ATTN_GROUP, HEADS_PER_STEP * GRID_W, KEY_SLAB), BF16),
            pltpu.VMEM((n_passes, 2, ATTN_GROUP, HEADS_PER_STEP * GRID_W, LANES), F32),
        ],
        compiler_params=_compiler_params(1),
        name="nbr_attention",
    )(proj, proj, proj, bias_tab, g.reshape(1, D_MODEL)[:, :WIDTH_A])


def _attention_bias_table(rpb):
    qc = np.arange(GRID_W)[:, None]
    kc = np.arange(GRID_W)[None, :]
    win = np.clip(qc - NA_COLS // 2, 0, GRID_W - NA_COLS)
    col_ok = (kc >= win) & (kc < win + NA_COLS)
    n_rel = rpb.shape[-1]
    onehot = (kc - qc + NA_COLS - 1)[None] == np.arange(n_rel)[:, None, None]
    zeros = np.zeros_like(onehot)
    onehot2 = np.concatenate([np.concatenate([onehot, zeros], axis=-1),
                              np.concatenate([zeros, onehot], axis=-1)], axis=0)
    rows2 = jnp.concatenate([rpb[:, :, :-1], rpb[:, :, 1:]], axis=-1).astype(F32)
    tiles = jnp.einsum("lhab,bqk->lhaqk", rows2, jnp.asarray(onehot2, F32),
                       precision=lax.Precision.HIGHEST)
    return jnp.where(jnp.asarray(np.tile(col_ok, (1, 2))), tiles, NEG_INF) * LOG2_E


def _gelu_tanh(x):
    c2 = -2.0 * math.sqrt(2.0 / math.pi) * LOG2_E
    return x / (1.0 + jnp.exp2(x * (c2 + (c2 * 0.044715) * (x * x))))


def _gate_chunk(z_ref, wcat_ref, bs_ref, g_ref, o_ref, t0):
    head_of_lane = lax.broadcasted_iota(jnp.int32, (1, WIDTH_B), 1) // HEAD_DIM
    u = z_ref[pl.ds(t0, CHUNK), :WIDTH_B].astype(F32)
    vn = z_ref[pl.ds(t0, CHUNK), WIDTH_B:]
    stacked = jnp.concatenate(
        [jnp.where(head_of_lane == h, vn, jnp.zeros_like(vn)) for h in range(N_HEADS_B)], axis=0)
    mixed = jnp.dot(wcat_ref[...], stacked, preferred_element_type=F32) + bs_ref[...]
    o_ref[pl.ds(t0, CHUNK), :] = _head_norm_lane_pairs(u * mixed, g_ref[...]).astype(o_ref.dtype)


def _dft_tables(seq):
    def cos_sin(n):
        i = lax.broadcasted_iota(jnp.int32, (n, n), 0)
        j = lax.broadcasted_iota(jnp.int32, (n, n), 1)
        ang = ((i * j) % n).astype(F32) * (2.0 * math.pi / n)
        return jnp.cos(ang), jnp.sin(ang)

    def cos_sin_rows(row_step, n_rows):
        i = lax.broadcasted_iota(jnp.int32, (n_rows, seq), 0) * row_step
        j = lax.broadcasted_iota(jnp.int32, (n_rows, seq), 1)
        ang = ((i * j) % seq).astype(F32) * (2.0 * math.pi / seq)
        return jnp.cos(ang), jnp.sin(ang)

    n_rows = seq // 2 + DFT_TAIL
    n_outer = -(-n_rows // GRID_W)
    ca, sa = (t[:, None, :] for t in cos_sin_rows(GRID_W, n_outer))
    cb, sb = (t[None, :, :] for t in cos_sin_rows(1, GRID_W))
    c_seq = (ca * cb - sa * sb).reshape(n_outer * GRID_W, seq)[:n_rows].astype(BF16)
    s_seq = (sa * cb + ca * sb).reshape(n_outer * GRID_W, seq)[:n_rows].astype(BF16)
    c_ch, s_ch = cos_sin(HEAD_DIM)
    eye = jnp.eye(WIDTH_C // HEAD_DIM, dtype=F32)
    ch_tab = jnp.concatenate([jnp.kron(eye, c_ch), jnp.kron(eye, s_ch)], axis=1).astype(BF16)
    i = lax.broadcasted_iota(jnp.int32, (N_CHUNK, N_CHUNK + DFT_TAIL), 0)
    k = lax.broadcasted_iota(jnp.int32, (N_CHUNK, N_CHUNK + DFT_TAIL), 1)
    reversal = (k == N_CHUNK - i).astype(BF16)
    return c_seq, s_seq, ch_tab, reversal


def _gate_fourier_kernel(z_ref, x_ref, wcat_ref, bs_ref, gb_ref,
                         cos_ref, sin_ref, ch_tab_ref, rev_ref, gc_ref,
                         ob_ref, oc_ref, hi_ref, *, seq):
    gate_args = (z_ref, wcat_ref, bs_ref, gb_ref, ob_ref)
    n_gate = seq // CHUNK
    n_chunks = seq // 2 // N_CHUNK
    gate_done = 0

    def gate_until(n):
        nonlocal gate_done
        for c in range(gate_done, min(n, n_gate)):
            _gate_chunk(*gate_args, c * CHUNK)
        gate_done = max(gate_done, min(n, n_gate))

    xcs = jnp.dot(x_ref[...], ch_tab_ref[...], preferred_element_type=F32)
    xc = xcs[:, :WIDTH_C].astype(BF16)
    xs = xcs[:, WIDTH_C:].astype(BF16)
    scale = 1.0 / math.sqrt(seq * HEAD_DIM)
    g = gc_ref[...]
    per_step = -(-n_gate // (2 * n_chunks))
    for t in range(n_chunks):
        r0 = t * N_CHUNK
        m = N_CHUNK + (DFT_TAIL if t == n_chunks - 1 else 0)
        a = jnp.dot(cos_ref[r0:r0 + m, :], xc, preferred_element_type=F32)
        b = jnp.dot(sin_ref[r0:r0 + m, :], xs, preferred_element_type=F32)
        lo = (a[:N_CHUNK] - b[:N_CHUNK]) * scale
        oc_ref[r0:r0 + N_CHUNK, :] = _head_norm_lane_pairs(lo, g).astype(oc_ref.dtype)
        hi_ref[r0:r0 + m, :] = _head_norm_lane_pairs((a + b) * scale, g).astype(hi_ref.dtype)
        gate_until((t + 1) * per_step)
    for u in range(n_chunks):
        src = hi_ref[u * N_CHUNK:(u + 1) * N_CHUNK + DFT_TAIL, :]
        blk = jnp.dot(rev_ref[...], src, preferred_element_type=F32)
        oc_ref[seq - (u + 1) * N_CHUNK:seq - u * N_CHUNK, :] = blk.astype(oc_ref.dtype)
        gate_until((n_chunks + u + 1) * per_step)
    gate_until(n_gate)


def _gate_fourier(proj, w_s, b_s, tables, g, batch, seq):
    z_off = 3 * WIDTH_A // (2 * WIDTH_B)
    c_off = (3 * WIDTH_A + 2 * WIDTH_B) // WIDTH_C
    wcat = jnp.transpose(w_s, (1, 0, 2)).reshape(CHUNK, N_HEADS_B * CHUNK).astype(BF16)
    bs_full = jnp.repeat(b_s.T.astype(F32), HEAD_DIM, axis=1)
    g_row = g.reshape(1, D_MODEL)
    g_b = g_row[:, WIDTH_A:WIDTH_A + WIDTH_B]
    g_c = g_row[:, WIDTH_A + WIDTH_B:]
    n_resident = 3 + len(tables) + 1
    return pl.pallas_call(
        functools.partial(_gate_fourier_kernel, seq=seq),
        grid=(batch,),
        in_specs=[
            pl.BlockSpec((seq, 2 * WIDTH_B), lambda b: (b, z_off)),
            pl.BlockSpec((seq, WIDTH_C), lambda b: (b, c_off)),
        ] + [_resident()] * n_resident,
        out_specs=[
            pl.BlockSpec((seq, WIDTH_B), lambda b: (b, 0)),
            pl.BlockSpec((seq, WIDTH_C), lambda b: (b, 0)),
        ],
        out_shape=[
            jax.ShapeDtypeStruct((batch * seq, WIDTH_B), BF16),
            jax.ShapeDtypeStruct((batch * seq, WIDTH_C), BF16),
        ],
        scratch_shapes=[pltpu.VMEM((seq // 2 + DFT_TAIL, WIDTH_C), BF16)],
        compiler_params=_compiler_params(1),
        name="gate_fourier",
    )(proj, proj, wcat, bs_full, g_b, *tables, g_c)


def _ffn_kernel(ma_ref, mb_ref, mc_ref, x_ref, wo_ref, gf_ref, wg_ref, wu_ref, wd_ref, gfin_ref,
                o_ref, h_ref, act_ref, *, final):
    x1_ref = o_ref
    mix = jnp.concatenate([ma_ref[...], mb_ref[...], mc_ref[...]], axis=-1)
    for c in range(D_MODEL // N_CHUNK):
        cols = slice(c * N_CHUNK, (c + 1) * N_CHUNK)
        x1_ref[:, cols] = x_ref[:, cols] + jnp.dot(mix, wo_ref[:, cols], preferred_element_type=F32)
    h_ref[...] = _rms_norm_rows(x1_ref[...], gf_ref[...]).astype(BF16)
    for c in range(D_FF // N_CHUNK):
        cols = slice(c * N_CHUNK, (c + 1) * N_CHUNK)
        gate = jnp.dot(h_ref[...], wg_ref[:, cols], preferred_element_type=F32)
        up = jnp.dot(h_ref[...], wu_ref[:, cols], preferred_element_type=F32)
        act_ref[:, cols] = (gate * jax.nn.sigmoid(gate) * up).astype(BF16)
    for c in range(D_MODEL // N_CHUNK):
        cols = slice(c * N_CHUNK, (c + 1) * N_CHUNK)
        x1_ref[:, cols] = x1_ref[:, cols] + jnp.dot(act_ref[...], wd_ref[:, cols],
                                                    preferred_element_type=F32)
    if final:
        o_ref[...] = _rms_norm_rows(x1_ref[...], gfin_ref[...])


def _out_proj_ffn(mix_a, mix_b, mix_c, x2d, w_out, g_ffn, w_gate, w_up, w_down, g_final, layer,
                  final):
    n = x2d.shape[0]
    row = lambda width: pl.BlockSpec((FFN_TM, width), lambda i: (i, 0))
    slab = lambda w: _layer_slab(w, layer, 1)
    return pl.pallas_call(
        functools.partial(_ffn_kernel, final=final),
        grid=(n // FFN_TM,),
        in_specs=[row(WIDTH_A), row(WIDTH_B), row(WIDTH_C), row(D_MODEL),
                  slab(w_out), _resident(), slab(w_gate), slab(w_up), slab(w_down), _resident()],
        out_specs=row(D_MODEL),
        out_shape=jax.ShapeDtypeStruct((n, D_MODEL), F32),
        scratch_shapes=[
            pltpu.VMEM((FFN_TM, D_MODEL), BF16),
            pltpu.VMEM((FFN_TM, D_FF), BF16),
        ],
        compiler_params=_compiler_params(1),
        name="out_proj_ffn",
    )(mix_a, mix_b, mix_c, x2d, w_out, g_ffn.reshape(1, D_MODEL), w_gate, w_up, w_down,
      g_final.reshape(1, D_MODEL))


def kernel(x, norm_mix_g, w_in, rpb, gmlp_ln_g, gmlp_ln_b, w_spatial, b_spatial, head_norm_g, w_out,
           norm_ffn_g, w_gate, w_up, w_down, final_norm_g):
    batch, seq, d_model = x.shape
    assert d_model == D_MODEL and seq % (2 * N_CHUNK) == 0 and seq // GRID_W >= NA_ROWS
    depth = w_in.shape[0]
    dft_tables = _dft_tables(seq)
    bias_tab = _attention_bias_table(rpb)
    w_in, w_out, w_gate, w_up, w_down = (w.astype(BF16) for w in (w_in, w_out, w_gate, w_up, w_down))
    x2d = x.reshape(batch * seq, D_MODEL)
    for l in range(depth):
        proj = _in_proj(x2d, norm_mix_g[l], w_in, gmlp_ln_g[l], gmlp_ln_b[l], l)
        mix_a = _attention(proj, bias_tab, head_norm_g[l], l, batch, seq)
        mix_b, mix_c = _gate_fourier(proj, w_spatial[l], b_spatial[l], dft_tables, head_norm_g[l],
                                     batch, seq)
        x2d = _out_proj_ffn(mix_a, mix_b, mix_c, x2d, w_out, norm_ffn_g[l], w_gate, w_up, w_down,
                            final_norm_g, l, final=(l == depth - 1))
    return x2d.reshape(batch, seq, D_MODEL)
```

```python
import functools
import math

import numpy as np
import jax
import jax.numpy as jnp
from jax import lax
from jax.experimental import pallas as pl
from jax.experimental.pallas import tpu as pltpu

F32 = jnp.float32
BF16 = jnp.bfloat16

D_MODEL = 1024
GRID_W = 64
HEAD_DIM = 64
WIDTH_A = D_MODEL // 2
WIDTH_B = D_MODEL // 4
WIDTH_C = D_MODEL // 4
N_HEADS_A = WIDTH_A // HEAD_DIM
N_HEADS_B = WIDTH_B // HEAD_DIM
IN_WIDTH = 3 * WIDTH_A + 2 * WIDTH_B + WIDTH_C
NA_ROWS = 8
NA_COLS = 16
CHUNK = 128
D_FF = -(-8 * D_MODEL // (3 * 256)) * 256
EPS = 1e-6
NEG_INF = -1e9

LANES = 128
VMEM_LIMIT_BYTES = 56 * 1024 * 1024

HEADS_PER_STEP = 4
KEY_SLAB = NA_ROWS * GRID_W
ATTN_GROUP = 8
LOG2_E = math.log2(math.e)
Q_SCALE = HEAD_DIM ** -0.5 * LOG2_E

IN_TM = 1024
IN_RB = 512
FFN_TM = 1024
N_CHUNK = 256
DFT_TAIL = 16


def _compiler_params(n_grid_dims):
    return pltpu.CompilerParams(
        dimension_semantics=("arbitrary",) * n_grid_dims,
        vmem_limit_bytes=VMEM_LIMIT_BYTES,
    )


def _resident():
    return pl.BlockSpec(memory_space=pltpu.VMEM)


def _layer_slab(stacked, layer, n_grid_dims):
    zeros = (0,) * (stacked.ndim - 1)
    index_map = {1: lambda i: (layer,) + zeros, 2: lambda i, j: (layer,) + zeros}[n_grid_dims]
    return pl.BlockSpec((None,) + stacked.shape[1:], index_map, pipeline_mode=pl.Buffered(1))


def _rms_norm_rows(x, g):
    ms = jnp.mean(x * x, axis=-1, keepdims=True)
    return x * lax.rsqrt(ms + EPS) * g


def _head_norm_lane_pairs(y, g):
    n_tiles = y.shape[-1] // LANES
    lane = lax.broadcasted_iota(jnp.int32, (1, LANES), 1)
    first = lane < HEAD_DIM
    outs = []
    for t in range(n_tiles):
        yt = y[:, t * LANES:(t + 1) * LANES]
        sq = yt * yt
        ss_a = jnp.sum(jnp.where(first, sq, 0.0), axis=-1, keepdims=True)
        ss_b = jnp.sum(jnp.where(first, 0.0, sq), axis=-1, keepdims=True)
        ss = jnp.where(first, ss_a, ss_b)
        gain = g[:, t * LANES:(t + 1) * LANES] * math.sqrt(HEAD_DIM)
        outs.append(yt * lax.rsqrt(ss + HEAD_DIM * EPS) * gain)
    return outs[0] if n_tiles == 1 else jnp.concatenate(outs, axis=-1)


def _in_proj_kernel(x_ref, g_ref, w_ref, lng_ref, lnb_ref, o_ref, h_ref):
    assert N_CHUNK == WIDTH_B
    u_chunk = 3 * WIDTH_A // N_CHUNK
    v_chunk = u_chunk + 1
    others = [c for c in range(IN_WIDTH // N_CHUNK) if c not in (u_chunk, v_chunk)]
    for rb in range(IN_TM // IN_RB):
        rows = slice(rb * IN_RB, (rb + 1) * IN_RB)
        h_ref[rows, :] = _rms_norm_rows(x_ref[rows, :], g_ref[...]).astype(BF16)
        for c in [v_chunk, u_chunk] + others:
            cols = slice(c * N_CHUNK, (c + 1) * N_CHUNK)
            acc = jnp.dot(h_ref[rows, :], w_ref[:, cols], preferred_element_type=F32)
            if (c + 1) * N_CHUNK <= WIDTH_A:
                acc = acc * Q_SCALE
            elif c in (u_chunk, v_chunk):
                acc = _gelu_tanh(acc)
                if c == v_chunk:
                    mu = jnp.mean(acc, axis=-1, keepdims=True)
                    d = acc - mu
                    var = jnp.mean(d * d, axis=-1, keepdims=True)
                    acc = d * lax.rsqrt(var + EPS) * lng_ref[...] + lnb_ref[...]
            o_ref[rows, cols] = acc.astype(o_ref.dtype)


def _in_proj(x2d, g, w_stack, ln_g, ln_b, layer):
    n = x2d.shape[0]
    return pl.pallas_call(
        _in_proj_kernel,
        grid=(n // IN_TM,),
        in_specs=[
            pl.BlockSpec((IN_TM, D_MODEL), lambda i: (i, 0)),
            _resident(),
            _layer_slab(w_stack, layer, 1),
            _resident(),
            _resident(),
        ],
        out_specs=pl.BlockSpec((IN_TM, IN_WIDTH), lambda i: (i, 0)),
        out_shape=jax.ShapeDtypeStruct((n, IN_WIDTH), BF16),
        scratch_shapes=[pltpu.VMEM((IN_TM, D_MODEL), BF16)],
        compiler_params=_compiler_params(1),
        name="in_proj",
    )(x2d, g.reshape(1, D_MODEL), w_stack, ln_g.reshape(1, WIDTH_B), ln_b.reshape(1, WIDTH_B))


def _attn_kernel(q_ref, k_ref, v_ref, bias_ref, g_ref, o_ref, p_ref, l_ref, *, n_rows):
    width = HEADS_PER_STEP * HEAD_DIM
    n_groups = n_rows // ATTN_GROUP
    phases = []
    for pas in range(WIDTH_A // width):
        lanes = pl.ds(pas * width, width)
        probs, outputs = _attention_pass(
            q_ref.at[:, lanes], k_ref.at[:, lanes], v_ref.at[:, lanes],
            bias_ref.at[pl.ds(pas * HEADS_PER_STEP, HEADS_PER_STEP)], g_ref[:, lanes],
            o_ref.at[:, lanes], p_ref.at[pas], l_ref.at[pas], n_rows)
        phases.append([functools.partial(probs, grp, grp) for grp in range(n_groups)]
                      + [functools.partial(outputs, grp, grp) for grp in range(n_groups)])
    skew = n_groups
    for step in range(len(phases) * 2 * n_groups):
        for pas, todo in reversed(list(enumerate(phases))):
            k = step - pas * (2 * n_groups - skew)
            if 0 <= k < len(todo):
                todo[k]()


def _attention_pass(q_ref, k_ref, v_ref, bias_ref, g, o_ref, p_ref, l_ref, n_rows):
    width = HEADS_PER_STEP * HEAD_DIM
    head_of_lane = lax.broadcasted_iota(jnp.int32, (1, width), 1) // HEAD_DIM

    def rows_of(group):
        for j in range(ATTN_GROUP):
            r = group * ATTN_GROUP + j
            rs = min(max(r - NA_ROWS // 2, 0), n_rows - NA_ROWS)
            yield j, r, rs, r * GRID_W, rs * GRID_W

    def probs(group, slot):
        for j, r, rs, q0, k0 in rows_of(group):
            q = q_ref[pl.ds(q0, GRID_W), :]
            zero = jnp.zeros_like(q)
            qm = jnp.concatenate(
                [jnp.where(head_of_lane == h, q, zero) for h in range(HEADS_PER_STEP)], axis=0)
            s = lax.dot_general(qm, k_ref[pl.ds(k0, KEY_SLAB), :], (((1,), (1,)), ((), ())),
                                preferred_element_type=F32)
            a0 = NA_ROWS - 1 - (r - rs)
            s = s + jnp.concatenate(
                [jnp.concatenate([bias_ref[h, a0 + 2 * i] for i in range(NA_ROWS // 2)], axis=-1)
                 for h in range(HEADS_PER_STEP)], axis=0)
            m = jnp.max(s, axis=-1, keepdims=True)
            p = jnp.exp2(s - m)
            l = jnp.sum(p, axis=-1, keepdims=True)
            l_ref[slot, j] = jnp.broadcast_to(l, (l.shape[0], LANES))
            p_ref[slot, j] = p.astype(BF16)

    def outputs(group, slot):
        for j, _, _, q0, k0 in rows_of(group):
            o4 = jnp.dot(p_ref[slot, j], v_ref[pl.ds(k0, KEY_SLAB), :], preferred_element_type=F32)
            inv_l = 1.0 / l_ref[slot, j]
            o4 = o4 * jnp.concatenate([inv_l] * (width // LANES), axis=-1)
            o = o4[:GRID_W]
            for h in range(1, HEADS_PER_STEP):
                o = jnp.where(head_of_lane == h, o4[h * GRID_W:(h + 1) * GRID_W], o)
            o_ref[pl.ds(q0, GRID_W), :] = _head_norm_lane_pairs(o, g).astype(o_ref.dtype)

    return probs, outputs


def _attention(proj, bias_tab, g, layer, batch, seq):
    n_rows = seq // GRID_W
    assert n_rows % (2 * ATTN_GROUP) == 0
    n_passes = WIDTH_A // (HEADS_PER_STEP * HEAD_DIM)
    return pl.pallas_call(
        functools.partial(_attn_kernel, n_rows=n_rows),
        grid=(batch,),
        in_specs=[
            pl.BlockSpec((seq, WIDTH_A), lambda b: (b, 0)),
            pl.BlockSpec((seq, WIDTH_A), lambda b: (b, 1)),
            pl.BlockSpec((seq, WIDTH_A), lambda b: (b, 2)),
            _layer_slab(bias_tab, layer, 1),
            _resident(),
        ],
        out_specs=pl.BlockSpec((seq, WIDTH_A), lambda b: (b, 0)),
        out_shape=jax.ShapeDtypeStruct((batch * seq, WIDTH_A), BF16),
        scratch_shapes=[
            pltpu.VMEM((n_passes, n_rows // ATTN_GROUP, ATTN_GROUP, HEADS_PER_STEP * GRID_W, KEY_SLAB), BF16),
            pltpu.VMEM((n_passes, n_rows // ATTN_GROUP, ATTN_GROUP, HEADS_PER_STEP * GRID_W, LANES), F32),
        ],
        compiler_params=_compiler_params(1),
        name="nbr_attention",
    )(proj, proj, proj, bias_tab, g.reshape(1, D_MODEL)[:, :WIDTH_A])


def _attention_bias_table(rpb):
    qc = np.arange(GRID_W)[:, None]
    kc = np.arange(GRID_W)[None, :]
    win = np.clip(qc - NA_COLS // 2, 0, GRID_W - NA_COLS)
    col_ok = (kc >= win) & (kc < win + NA_COLS)
    n_rel = rpb.shape[-1]
    onehot = (kc - qc + NA_COLS - 1)[None] == np.arange(n_rel)[:, None, None]
    zeros = np.zeros_like(onehot)
    onehot2 = np.concatenate([np.concatenate([onehot, zeros], axis=-1),
                              np.concatenate([zeros, onehot], axis=-1)], axis=0)
    rows2 = jnp.concatenate([rpb[:, :, :-1], rpb[:, :, 1:]], axis=-1).astype(F32)
    tiles = jnp.einsum("lhab,bqk->lhaqk", rows2, jnp.asarray(onehot2, F32),
                       precision=lax.Precision.HIGHEST)
    return jnp.where(jnp.asarray(np.tile(col_ok, (1, 2))), tiles, NEG_INF) * LOG2_E


def _gelu_tanh(x):
    c2 = -2.0 * math.sqrt(2.0 / math.pi) * LOG2_E
    return x / (1.0 + jnp.exp2(x * (c2 + (c2 * 0.044715) * (x * x))))


def _gate_chunk(z_ref, wcat_ref, bs_ref, g_ref, o_ref, t0):
    head_of_lane = lax.broadcasted_iota(jnp.int32, (1, WIDTH_B), 1) // HEAD_DIM
    u = z_ref[pl.ds(t0, CHUNK), :WIDTH_B].astype(F32)
    vn = z_ref[pl.ds(t0, CHUNK), WIDTH_B:]
    stacked = jnp.concatenate(
        [jnp.where(head_of_lane == h, vn, jnp.zeros_like(vn)) for h in range(N_HEADS_B)], axis=0)
    mixed = jnp.dot(wcat_ref[...], stacked, preferred_element_type=F32) + bs_ref[...]
    o_ref[pl.ds(t0, CHUNK), :] = _head_norm_lane_pairs(u * mixed, g_ref[...]).astype(o_ref.dtype)


def _dft_tables(seq):
    def cos_sin(n):
        i = lax.broadcasted_iota(jnp.int32, (n, n), 0)
        j = lax.broadcasted_iota(jnp.int32, (n, n), 1)
        ang = ((i * j) % n).astype(F32) * (2.0 * math.pi / n)
        return jnp.cos(ang), jnp.sin(ang)

    def cos_sin_rows(row_step, n_rows):
        i = lax.broadcasted_iota(jnp.int32, (n_rows, seq), 0) * row_step
        j = lax.broadcasted_iota(jnp.int32, (n_rows, seq), 1)
        ang = ((i * j) % seq).astype(F32) * (2.0 * math.pi / seq)
        return jnp.cos(ang), jnp.sin(ang)

    n_rows = seq // 2 + DFT_TAIL
    n_outer = -(-n_rows // GRID_W)
    ca, sa = (t[:, None, :] for t in cos_sin_rows(GRID_W, n_outer))
    cb, sb = (t[None, :, :] for t in cos_sin_rows(1, GRID_W))
    c_seq = (ca * cb - sa * sb).reshape(n_outer * GRID_W, seq)[:n_rows].astype(BF16)
    s_seq = (sa * cb + ca * sb).reshape(n_outer * GRID_W, seq)[:n_rows].astype(BF16)
    c_ch, s_ch = cos_sin(HEAD_DIM)
    eye = jnp.eye(WIDTH_C // HEAD_DIM, dtype=F32)
    ch_tab = jnp.concatenate([jnp.kron(eye, c_ch), jnp.kron(eye, s_ch)], axis=1).astype(BF16)
    i = lax.broadcasted_iota(jnp.int32, (N_CHUNK, N_CHUNK + DFT_TAIL), 0)
    k = lax.broadcasted_iota(jnp.int32, (N_CHUNK, N_CHUNK + DFT_TAIL), 1)
    reversal = (k == N_CHUNK - i).astype(BF16)
    return c_seq, s_seq, ch_tab, reversal


def _gate_fourier_kernel(z_ref, x_ref, wcat_ref, bs_ref, gb_ref,
                         cos_ref, sin_ref, ch_tab_ref, rev_ref, gc_ref,
                         ob_ref, oc_ref, hi_ref, *, seq):
    gate_args = (z_ref, wcat_ref, bs_ref, gb_ref, ob_ref)
    n_gate = seq // CHUNK
    n_chunks = seq // 2 // N_CHUNK
    gate_done = 0

    def gate_until(n):
        nonlocal gate_done
        for c in range(gate_done, min(n, n_gate)):
            _gate_chunk(*gate_args, c * CHUNK)
        gate_done = max(gate_done, min(n, n_gate))

    xcs = jnp.dot(x_ref[...], ch_tab_ref[...], preferred_element_type=F32)
    xc = xcs[:, :WIDTH_C].astype(BF16)
    xs = xcs[:, WIDTH_C:].astype(BF16)
    scale = 1.0 / math.sqrt(seq * HEAD_DIM)
    g = gc_ref[...]
    per_step = -(-n_gate // (2 * n_chunks))
    for t in range(n_chunks):
        r0 = t * N_CHUNK
        m = N_CHUNK + (DFT_TAIL if t == n_chunks - 1 else 0)
        a = jnp.dot(cos_ref[r0:r0 + m, :], xc, preferred_element_type=F32)
        b = jnp.dot(sin_ref[r0:r0 + m, :], xs, preferred_element_type=F32)
        lo = (a[:N_CHUNK] - b[:N_CHUNK]) * scale
        oc_ref[r0:r0 + N_CHUNK, :] = _head_norm_lane_pairs(lo, g).astype(oc_ref.dtype)
        hi_ref[r0:r0 + m, :] = _head_norm_lane_pairs((a + b) * scale, g).astype(hi_ref.dtype)
        gate_until((t + 1) * per_step)
    for u in range(n_chunks):
        src = hi_ref[u * N_CHUNK:(u + 1) * N_CHUNK + DFT_TAIL, :]
        blk = jnp.dot(rev_ref[...], src, preferred_element_type=F32)
        oc_ref[seq - (u + 1) * N_CHUNK:seq - u * N_CHUNK, :] = blk.astype(oc_ref.dtype)
        gate_until((n_chunks + u + 1) * per_step)
    gate_until(n_gate)


def _gate_fourier(proj, w_s, b_s, tables, g, batch, seq):
    z_off = 3 * WIDTH_A // (2 * WIDTH_B)
    c_off = (3 * WIDTH_A + 2 * WIDTH_B) // WIDTH_C
    wcat = jnp.transpose(w_s, (1, 0, 2)).reshape(CHUNK, N_HEADS_B * CHUNK).astype(BF16)
    bs_full = jnp.repeat(b_s.T.astype(F32), HEAD_DIM, axis=1)
    g_row = g.reshape(1, D_MODEL)
    g_b = g_row[:, WIDTH_A:WIDTH_A + WIDTH_B]
    g_c = g_row[:, WIDTH_A + WIDTH_B:]
    n_resident = 3 + len(tables) + 1
    return pl.pallas_call(
        functools.partial(_gate_fourier_kernel, seq=seq),
        grid=(batch,),
        in_specs=[
            pl.BlockSpec((seq, 2 * WIDTH_B), lambda b: (b, z_off)),
            pl.BlockSpec((seq, WIDTH_C), lambda b: (b, c_off)),
        ] + [_resident()] * n_resident,
        out_specs=[
            pl.BlockSpec((seq, WIDTH_B), lambda b: (b, 0)),
            pl.BlockSpec((seq, WIDTH_C), lambda b: (b, 0)),
        ],
        out_shape=[
            jax.ShapeDtypeStruct((batch * seq, WIDTH_B), BF16),
            jax.ShapeDtypeStruct((batch * seq, WIDTH_C), BF16),
        ],
        scratch_shapes=[pltpu.VMEM((seq // 2 + DFT_TAIL, WIDTH_C), BF16)],
        compiler_params=_compiler_params(1),
        name="gate_fourier",
    )(proj, proj, wcat, bs_full, g_b, *tables, g_c)


def _ffn_kernel(ma_ref, mb_ref, mc_ref, x_ref, wo_ref, gf_ref, wg_ref, wu_ref, wd_ref, gfin_ref,
                o_ref, h_ref, act_ref, *, final):
    x1_ref = o_ref
    mix = jnp.concatenate([ma_ref[...], mb_ref[...], mc_ref[...]], axis=-1)
    for c in range(D_MODEL // N_CHUNK):
        cols = slice(c * N_CHUNK, (c + 1) * N_CHUNK)
        x1_ref[:, cols] = x_ref[:, cols] + jnp.dot(mix, wo_ref[:, cols], preferred_element_type=F32)
    h_ref[...] = _rms_norm_rows(x1_ref[...], gf_ref[...]).astype(BF16)
    for c in range(D_FF // N_CHUNK):
        cols = slice(c * N_CHUNK, (c + 1) * N_CHUNK)
        gate = jnp.dot(h_ref[...], wg_ref[:, cols], preferred_element_type=F32)
        up = jnp.dot(h_ref[...], wu_ref[:, cols], preferred_element_type=F32)
        act_ref[:, cols] = (gate * jax.nn.sigmoid(gate) * up).astype(BF16)
    for c in range(D_MODEL // N_CHUNK):
        cols = slice(c * N_CHUNK, (c + 1) * N_CHUNK)
        x1_ref[:, cols] = x1_ref[:, cols] + jnp.dot(act_ref[...], wd_ref[:, cols],
                                                    preferred_element_type=F32)
    if final:
        o_ref[...] = _rms_norm_rows(x1_ref[...], gfin_ref[...])


def _out_proj_ffn(mix_a, mix_b, mix_c, x2d, w_out, g_ffn, w_gate, w_up, w_down, g_final, layer,
                  final):
    n = x2d.shape[0]
    row = lambda width: pl.BlockSpec((FFN_TM, width), lambda i: (i, 0))
    slab = lambda w: _layer_slab(w, layer, 1)
    return pl.pallas_call(
        functools.partial(_ffn_kernel, final=final),
        grid=(n // FFN_TM,),
        in_specs=[row(WIDTH_A), row(WIDTH_B), row(WIDTH_C), row(D_MODEL),
                  slab(w_out), _resident(), slab(w_gate), slab(w_up), slab(w_down), _resident()],
        out_specs=row(D_MODEL),
        out_shape=jax.ShapeDtypeStruct((n, D_MODEL), F32),
        scratch_shapes=[
            pltpu.VMEM((FFN_TM, D_MODEL), BF16),
            pltpu.VMEM((FFN_TM, D_FF), BF16),
        ],
        compiler_params=_compiler_params(1),
        name="out_proj_ffn",
    )(mix_a, mix_b, mix_c, x2d, w_out, g_ffn.reshape(1, D_MODEL), w_gate, w_up, w_down,
      g_final.reshape(1, D_MODEL))


def kernel(x, norm_mix_g, w_in, rpb, gmlp_ln_g, gmlp_ln_b, w_spatial, b_spatial, head_norm_g, w_out,
           norm_ffn_g, w_gate, w_up, w_down, final_norm_g):
    batch, seq, d_model = x.shape
    assert d_model == D_MODEL and seq % (2 * N_CHUNK) == 0 and seq // GRID_W >= NA_ROWS
    depth = w_in.shape[0]
    dft_tables = _dft_tables(seq)
    bias_tab = _attention_bias_table(rpb)
    w_in, w_out, w_gate, w_up, w_down = (w.astype(BF16) for w in (w_in, w_out, w_gate, w_up, w_down))
    x2d = x.reshape(batch * seq, D_MODEL)
    for l in range(depth):
        proj = _in_proj(x2d, norm_mix_g[l], w_in, gmlp_ln_g[l], gmlp_ln_b[l], l)
        mix_a = _attention(proj, bias_tab, head_norm_g[l], l, batch, seq)
        mix_b, mix_c = _gate_fourier(proj, w_spatial[l], b_spatial[l], dft_tables, head_norm_g[l],
                                     batch, seq)
        x2d = _out_proj_ffn(mix_a, mix_b, mix_c, x2d, w_out, norm_ffn_g[l], w_gate, w_up, w_down,
                            final_norm_g, l, final=(l == depth - 1))
    return x2d.reshape(batch, seq, D_MODEL)
```
